```python
import math
import jax, jax.numpy as jnp
from jax import lax
import numpy as np

D_MODEL = 2048
BATCH = 2
SEQ = 4096
DEPTH = 1
DEC_BATCH = 4
DEC_SEQ = 2048
PAST_LEN = 128

N_HEADS = 8
HEAD_DIM = 64
V_DIM = 2 * HEAD_DIM
ATT_W = N_HEADS * 2 * HEAD_DIM
POOL_WINDOWS = (2, 4, 8, 16)
N_POOL_GROUPS = len(POOL_WINDOWS)
POOL_W = D_MODEL // 2
POOL_GROUP_W = POOL_W // N_POOL_GROUPS
IN_COLS = 3 * ATT_W + POOL_W + 2 * D_MODEL
N_BUCKETS = 32
MAX_DISTANCE = 128
N_EXPERTS = 32
TOP_K = 4
D_FF = D_MODEL
SWIGLU_LIMIT = 7.0
SWIGLU_ALPHA = 1.702
NORM_EPS = 1e-6
Q_BLOCK = 128
ROW_BLOCK = 128

kernel_name = "hybrid_diffattn_pool_moe_encoder"

F32 = jnp.float32


def rmsnorm(x, g):
    xf = x.astype(F32)
    y = xf * lax.rsqrt(jnp.mean(xf * xf, axis=-1, keepdims=True) + NORM_EPS) * g.astype(F32)
    return y.astype(x.dtype)


def t5_bucket(rel):
    half = N_BUCKETS // 2
    max_exact = half // 2
    ret = jnp.where(rel > 0, half, 0)
    n = jnp.abs(rel)
    nf = jnp.maximum(n, 1).astype(F32)
    large = max_exact + (jnp.log(nf / max_exact) / math.log(MAX_DISTANCE / max_exact)
                         * (half - max_exact)).astype(jnp.int32)
    large = jnp.minimum(large, half - 1)
    return ret + jnp.where(n < max_exact, n, large)


def diff_attention(q1, q2, k1, k2, v, lam, rel_bias):
    B, S, H, Dh = q1.shape
    nblk = S // Q_BLOCK
    k_pos = jnp.arange(S)

    def block(args):
        i, qb1, qb2 = args
        q_pos = i * Q_BLOCK + jnp.arange(Q_BLOCK)
        bias = rel_bias[t5_bucket(k_pos[None, :] - q_pos[:, None])]
        bias = jnp.transpose(bias, (2, 0, 1)).astype(F32)[None]
        s1 = jnp.einsum('bqhd,bkhd->bhqk', qb1, k1).astype(F32) + bias
        s2 = jnp.einsum('bqhd,bkhd->bhqk', qb2, k2).astype(F32) + bias
        p = jax.nn.softmax(s1, axis=-1) - lam * jax.nn.softmax(s2, axis=-1)
        return jnp.einsum('bhqk,bkhd->bqhd', p.astype(v.dtype), v)

    qs1 = q1.reshape(B, nblk, Q_BLOCK, H, Dh).transpose(1, 0, 2, 3, 4)
    qs2 = q2.reshape(B, nblk, Q_BLOCK, H, Dh).transpose(1, 0, 2, 3, 4)
    out = lax.map(block, (jnp.arange(nblk), qs1, qs2))
    return out.transpose(1, 0, 2, 3, 4).reshape(B, S, H, v.shape[-1])


def centred_pool_minus_self(u, w):
    B, S, C = u.shape
    uf = u.astype(F32)
    c = jnp.concatenate([jnp.zeros((B, 1, C), F32), lax.cumsum(uf, axis=1)], axis=1)
    t = jnp.arange(S)
    lo = jnp.clip(t - w // 2, 0, S)
    hi = jnp.clip(t + (w - w // 2), 0, S)
    s = c[:, hi] - c[:, lo]
    cnt = (hi - lo).astype(F32)
    return (s / cnt[None, :, None] - uf).astype(u.dtype)


def moe(x, w_router, b_router, w_gate, b_gate, w_up, b_up, w_down, b_down):
    T, D = x.shape
    logits = (x @ w_router).astype(F32) + b_router.astype(F32)
    top_vals, top_idx = lax.top_k(logits, TOP_K)
    gates = jax.nn.softmax(top_vals, axis=-1)
    n_assign = T * TOP_K
    e_flat = top_idx.reshape(-1)
    tok_flat = jnp.arange(n_assign) // TOP_K
    g_flat = gates.reshape(-1)
    order = jnp.argsort(e_flat)
    e_s, tok_s, g_s = e_flat[order], tok_flat[order], g_flat[order]
    counts = jnp.bincount(e_flat, length=N_EXPERTS)
    padded = (counts + ROW_BLOCK - 1) // ROW_BLOCK * ROW_BLOCK
    start_raw = jnp.cumsum(counts) - counts
    pad_end = jnp.cumsum(padded)
    start_pad = pad_end - padded
    dest = start_pad[e_s] + (jnp.arange(n_assign) - start_raw[e_s])
    n_rows = (n_assign + ROW_BLOCK - 1) // ROW_BLOCK * ROW_BLOCK + N_EXPERTS * ROW_BLOCK
    n_blocks = n_rows // ROW_BLOCK
    row_tok = jnp.zeros((n_rows,), jnp.int32).at[dest].set(tok_s)
    row_gate = jnp.zeros((n_rows,), F32).at[dest].set(g_s)
    blk_expert = jnp.minimum(
        jnp.searchsorted(pad_end, jnp.arange(n_blocks) * ROW_BLOCK, side='right'), N_EXPERTS - 1)
    xs = x[row_tok].reshape(n_blocks, ROW_BLOCK, D)

    def expert_block(args):
        xb, e = args
        g = xb @ w_gate[e] + b_gate[e]
        up = xb @ w_up[e] + b_up[e]
        g = jnp.minimum(g, SWIGLU_LIMIT)
        up = jnp.clip(up, -SWIGLU_LIMIT, SWIGLU_LIMIT)
        h = (up + 1.0) * (g * jax.nn.sigmoid(SWIGLU_ALPHA * g))
        return h @ w_down[e] + b_down[e]

    ys = lax.map(expert_block, (xs, blk_expert)).reshape(n_rows, D)
    y = jnp.zeros((T, D), F32).at[row_tok].add(ys.astype(F32) * row_gate[:, None])
    return y.astype(x.dtype)


def encoder_layer(x, layer_idx, norm1_g, w_in, q_norm_g, k_norm_g, lambda_q1, lambda_k1,
                  lambda_q2, lambda_k2, subln_g, rel_bias, w_pool_grp, pool_scale, w_att_out,
                  w_pool_out, w_o, norm2_g, w_router, b_router, w_gate, b_gate, w_up, b_up,
                  w_down, b_down):
    B, S, D = x.shape
    xn = rmsnorm(x, norm1_g)
    proj = xn @ w_in
    c1 = ATT_W
    c2 = 2 * ATT_W
    c3 = 3 * ATT_W
    c4 = c3 + POOL_W
    c5 = c4 + D_MODEL
    q, k, v, u, ga, gb = jnp.split(proj, [c1, c2, c3, c4, c5], axis=-1)

    q = q.reshape(B, S, N_HEADS, 2, HEAD_DIM)
    k = k.reshape(B, S, N_HEADS, 2, HEAD_DIM)
    v = v.reshape(B, S, N_HEADS, V_DIM)
    q = rmsnorm(q, q_norm_g) * (HEAD_DIM ** -0.5)
    k = rmsnorm(k, k_norm_g)
    lam_init = 0.8 - 0.6 * math.exp(-0.3 * layer_idx)
    lam = (jnp.exp(jnp.sum(lambda_q1.astype(F32) * lambda_k1.astype(F32)))
           - jnp.exp(jnp.sum(lambda_q2.astype(F32) * lambda_k2.astype(F32))) + lam_init)
    att = diff_attention(q[..., 0, :], q[..., 1, :], k[..., 0, :], k[..., 1, :], v, lam, rel_bias)
    att = rmsnorm(att, subln_g) * (1.0 - lam_init)
    y_a = att.reshape(B, S, ATT_W) @ w_att_out

    u = u.reshape(B, S, N_POOL_GROUPS, POOL_GROUP_W)
    pooled = jnp.stack([centred_pool_minus_self(u[:, :, gi], w)
                        for gi, w in enumerate(POOL_WINDOWS)], axis=2)
    mixed = jnp.einsum('bsgc,gcd->bsgd', pooled, w_pool_grp).reshape(B, S, POOL_W) * pool_scale
    y_b = mixed @ w_pool_out

    merged = jax.nn.sigmoid(ga) * y_a + jax.nn.sigmoid(gb) * y_b
    h = x + merged @ w_o

    hn = rmsnorm(h, norm2_g).reshape(B * S, D)
    ff = moe(hn, w_router, b_router, w_gate, b_gate, w_up, b_up, w_down, b_down)
    return h + ff.reshape(B, S, D)


def forward(x, norm1_g, w_in, q_norm_g, k_norm_g, lambda_q1, lambda_k1, lambda_q2, lambda_k2,
            subln_g, rel_bias, w_pool_grp, pool_scale, w_att_out, w_pool_out, w_o, norm2_g,
            w_router, b_router, w_gate, b_gate, w_up, b_up, w_down, b_down):
    for l in range(DEPTH):
        x = encoder_layer(x, l, norm1_g[l], w_in[l], q_norm_g[l], k_norm_g[l], lambda_q1[l],
                          lambda_k1[l], lambda_q2[l], lambda_k2[l], subln_g[l], rel_bias,
                          w_pool_grp[l], pool_scale[l], w_att_out[l], w_pool_out[l], w_o[l],
                          norm2_g[l], w_router[l], b_router[l], w_gate[l], b_gate[l], w_up[l],
                          b_up[l], w_down[l], b_down[l])
    return x


def setup_inputs(seed: int = 0) -> dict:
    key = jax.random.key(seed)
    ks = jax.random.split(key, 26)
    nrm = lambda k, shape, s: jax.random.normal(k, shape, F32) * s
    L, D, E, F = DEPTH, D_MODEL, N_EXPERTS, D_FF
    return {
        "x_prompt": nrm(ks[0], (BATCH, SEQ, D), 1.0),
        "x_sample": nrm(ks[1], (DEC_BATCH, DEC_SEQ, D), 1.0),
        "norm1_g": 1.0 + nrm(ks[2], (L, D), 0.02),
        "w_in": nrm(ks[3], (L, D, IN_COLS), D ** -0.5),
        "q_norm_g": 1.0 + nrm(ks[4], (L, HEAD_DIM), 0.02),
        "k_norm_g": 1.0 + nrm(ks[5], (L, HEAD_DIM), 0.02),
        "lambda_q1": nrm(ks[6], (L, HEAD_DIM), 0.1),
        "lambda_k1": nrm(ks[7], (L, HEAD_DIM), 0.1),
        "lambda_q2": nrm(ks[8], (L, HEAD_DIM), 0.1),
        "lambda_k2": nrm(ks[9], (L, HEAD_DIM), 0.1),
        "subln_g": 1.0 + nrm(ks[10], (L, V_DIM), 0.02),
        "rel_bias": nrm(ks[11], (N_BUCKETS, N_HEADS), 0.2),
        "w_pool_grp": nrm(ks[12], (L, N_POOL_GROUPS, POOL_GROUP_W, POOL_GROUP_W), POOL_GROUP_W ** -0.5),
        "pool_scale": 1.0 + nrm(ks[13], (L, POOL_W), 0.1),
        "w_att_out": nrm(ks[14], (L, ATT_W, D), ATT_W ** -0.5),
        "w_pool_out": nrm(ks[15], (L, POOL_W, D), POOL_W ** -0.5),
        "w_o": nrm(ks[16], (L, D, D), D ** -0.5),
        "norm2_g": 1.0 + nrm(ks[17], (L, D), 0.02),
        "w_router": nrm(ks[18], (L, D, E), D ** -0.5),
        "b_router": nrm(ks[19], (L, E), 0.01),
        "w_gate": nrm(ks[20], (L, E, D, F), D ** -0.5),
        "b_gate": nrm(ks[21], (L, E, F), 0.02),
        "w_up": nrm(ks[22], (L, E, D, F), D ** -0.5),
        "b_up": nrm(ks[23], (L, E, F), 0.02),
        "w_down": nrm(ks[24], (L, E, F, D), F ** -0.5),
        "b_down": nrm(ks[25], (L, E, D), 0.02),
    }


def reference(x_prompt, x_sample, norm1_g, w_in, q_norm_g, k_norm_g, lambda_q1, lambda_k1,
              lambda_q2, lambda_k2, subln_g, rel_bias, w_pool_grp, pool_scale, w_att_out,
              w_pool_out, w_o, norm2_g, w_router, b_router, w_gate, b_gate, w_up, b_up,
              w_down, b_down):
    params = (norm1_g, w_in, q_norm_g, k_norm_g, lambda_q1, lambda_k1, lambda_q2, lambda_k2,
              subln_g, rel_bias, w_pool_grp, pool_scale, w_att_out, w_pool_out, w_o, norm2_g,
              w_router, b_router, w_gate, b_gate, w_up, b_up, w_down, b_down)
    y_prompt = forward(x_prompt, *params)
    y_sample = forward(x_sample, *params)
    return (y_prompt, y_sample)
```

```python
import functools
import math

import jax
import jax.numpy as jnp
from jax import lax
from jax.experimental import pallas as pl
from jax.experimental.pallas import tpu as pltpu

F32 = jnp.float32
BF16 = jnp.bfloat16
I32 = jnp.int32

D_MODEL = 2048
N_HEADS = 8
HEAD_DIM = 64
V_DIM = 2 * HEAD_DIM
ATT_W = N_HEADS * V_DIM
POOL_WINDOWS = (2, 4, 8, 16)
POOL_W = D_MODEL // 2
POOL_GROUP_W = POOL_W // len(POOL_WINDOWS)
IN_COLS = 3 * ATT_W + POOL_W + 2 * D_MODEL
N_BUCKETS = 32
MAX_DISTANCE = 128
N_EXPERTS = 32
TOP_K = 4
D_FF = D_MODEL
SWIGLU_LIMIT = 7.0
SWIGLU_ALPHA = 1.702
NORM_EPS = 1e-6
LAM_INIT = 0.8 - 0.6 * math.exp(-0.3 * 0)
LOG2E = math.log2(math.e)

LANES = 128
VMEM_LIMIT = 56 * 1024 * 1024

TM_IN = 512
TN_IN = 1024
T_ATT = 256
TM_OUT = 256
POOL_HALO = 16
TR_ROUTE = 512
TM_MOE = 512
TF_MOE = 256
TD_DISP = 256
TC_COMB = 128


def _t5_thresholds():
    half = N_BUCKETS // 2
    max_exact = half // 2
    steps = half - max_exact
    ratio = MAX_DISTANCE // max_exact
    out = []
    for k in range(1, steps):
        n = max_exact
        while n ** steps < (max_exact ** steps) * (ratio ** k):
            n += 1
        out.append(n)
    return tuple(out)


T5_THRESHOLDS = _t5_thresholds()
T5_FAR = T5_THRESHOLDS[-1]


def _inproj_kernel(n_prompt_tiles, xp_ref, xs_ref, g1_ref, w_ref, gsum_ref, qg_ref, kg_ref, o_ref, xn_ref):
    i = pl.program_id(0)
    j = pl.program_id(1)

    @pl.when(j == 0)
    def _():
        x = jnp.where(i < n_prompt_tiles, xp_ref[...], xs_ref[...])
        ms = jnp.mean(x * x, axis=-1, keepdims=True)
        xn_ref[...] = (x * lax.rsqrt(ms + NORM_EPS) * g1_ref[...]).astype(BF16)

    acc = jnp.dot(xn_ref[...], w_ref[...], preferred_element_type=F32)

    def head_norm(g_ref, scale):
        sq = (acc * acc).astype(BF16)
        gw = gsum_ref.shape[0]
        parts = [jnp.dot(sq[:, c * gw:(c + 1) * gw], gsum_ref[...], preferred_element_type=F32)
                 for c in range(TN_IN // gw)]
        ss = jnp.concatenate(parts, axis=1)
        return acc * lax.rsqrt(ss * (1.0 / HEAD_DIM) + NORM_EPS) * (g_ref[...] * scale)

    @pl.when(j == 0)
    def _():
        o_ref[...] = head_norm(qg_ref, HEAD_DIM ** -0.5 * LOG2E).astype(BF16)

    @pl.when(j == 1)
    def _():
        o_ref[...] = head_norm(kg_ref, 1.0).astype(BF16)

    @pl.when((j == 2) | (j == 3))
    def _():
        o_ref[...] = acc.astype(BF16)

    @pl.when(j >= 4)
    def _():
        o_ref[...] = jax.nn.sigmoid(acc).astype(BF16)


def _in_projection(xp, xs, norm1_g, w_in_bf, q_norm_g, k_norm_g):
    n_p, n_s = xp.shape[0], xs.shape[0]
    n_tok = n_p + n_s
    npt, nst = n_p // TM_IN, n_s // TM_IN
    gw = 2 * LANES
    gid = jnp.arange(gw) // HEAD_DIM
    gsum = (gid[:, None] == gid[None, :]).astype(BF16)
    qg = jnp.tile(q_norm_g.astype(F32), TN_IN // HEAD_DIM)[None, :]
    kg = jnp.tile(k_norm_g.astype(F32), TN_IN // HEAD_DIM)[None, :]
    return pl.pallas_call(
        functools.partial(_inproj_kernel, npt),
        grid=(n_tok // TM_IN, IN_COLS // TN_IN),
        in_specs=[
            pl.BlockSpec((TM_IN, D_MODEL), lambda i, j: (jnp.minimum(i, npt - 1), 0)),
            pl.BlockSpec((TM_IN, D_MODEL), lambda i, j: (jnp.maximum(i - npt, 0), 0)),
            pl.BlockSpec((1, D_MODEL), lambda i, j: (0, 0)),
            pl.BlockSpec((D_MODEL, TN_IN), lambda i, j: (0, j)),
            pl.BlockSpec((gw, gw), lambda i, j: (0, 0)),
            pl.BlockSpec((1, TN_IN), lambda i, j: (0, 0)),
            pl.BlockSpec((1, TN_IN), lambda i, j: (0, 0)),
        ],
        out_specs=pl.BlockSpec((TM_IN, TN_IN), lambda i, j: (i, j)),
        out_shape=jax.ShapeDtypeStruct((n_tok, IN_COLS), BF16),
        scratch_shapes=[pltpu.VMEM((TM_IN, D_MODEL), BF16)],
        compiler_params=pltpu.CompilerParams(
            dimension_semantics=("arbitrary", "arbitrary"), vmem_limit_bytes=VMEM_LIMIT),
        name="in_projection",
    )(xp, xs, norm1_g.astype(F32)[None, :], w_in_bf, gsum, qg, kg)


def _band_kernel(rb_ref, o_ref):
    h = pl.program_id(0)
    t = T_ATT
    kj = lax.broadcasted_iota(I32, (t, t), 0)
    qi = lax.broadcasted_iota(I32, (t, t), 1)
    half = N_BUCKETS // 2
    max_exact = half // 2
    for d in range(3):
        rel = (d - 1) * t + kj - qi
        n = jnp.abs(rel)
        large = jnp.full((t, t), max_exact, I32)
        for th in T5_THRESHOLDS:
            large = large + jnp.where(n >= th, 1, 0)
        bucket = jnp.where(rel > 0, half, 0) + jnp.where(n < max_exact, n, large)
        val = jnp.zeros((t, t), F32)
        for b in range(N_BUCKETS):
            val = jnp.where(bucket == b, rb_ref[b, h], val)
        o_ref[0, d] = val * LOG2E


def _bias_band(rel_bias):
    return pl.pallas_call(
        _band_kernel,
        grid=(N_HEADS,),
        in_specs=[pl.BlockSpec(memory_space=pltpu.SMEM)],
        out_specs=pl.BlockSpec((1, 3, T_ATT, T_ATT), lambda h: (h, 0, 0, 0)),
        out_shape=jax.ShapeDtypeStruct((N_HEADS, 3, T_ATT, T_ATT), F32),
        compiler_params=pltpu.CompilerParams(dimension_semantics=("arbitrary",)),
        name="bias_band",
    )(rel_bias.astype(F32))


def _attn_kernel(seq, rb_ref, q_ref, k_ref, v_ref, band_ref, lq1_ref, lk1_ref, lq2_ref, lk2_ref, sg_ref,
                 o_ref, vt_ref, qz_ref, m_ref, l_ref, acc_ref):
    h = pl.program_id(1)
    qi = pl.program_id(2)
    t = T_ATT
    nk = seq // t

    @pl.when(qi == 0)
    def _():
        for c in range(nk):
            vt_ref[c] = v_ref[c * t:(c + 1) * t, :].astype(F32).T.astype(BF16)

    q = q_ref[...]
    lane = lax.broadcasted_iota(I32, q.shape, 1)
    qz_ref[0:t, :] = jnp.where(lane < HEAD_DIM, q, jnp.zeros_like(q))
    qz_ref[t:2 * t, :] = jnp.where(lane >= HEAD_DIM, q, jnp.zeros_like(q))
    m_ref[...] = jnp.full(m_ref.shape, -1e30, F32)
    l_ref[...] = jnp.zeros(l_ref.shape, F32)
    acc_ref[...] = jnp.zeros(acc_ref.shape, F32)

    def chunk(ki, bias_tile, bias_const):
        k_c = k_ref[pl.ds(pl.multiple_of(ki * t, t), t), :]
        s = lax.dot_general(k_c, qz_ref[...], (((1,), (1,)), ((), ())), preferred_element_type=F32)
        if bias_tile is not None:
            s = s + jnp.concatenate([bias_tile, bias_tile], axis=1)
        m_old = m_ref[...]
        m_new = jnp.maximum(m_old, jnp.max(s, axis=0, keepdims=True) + bias_const)
        alpha = jnp.exp2(m_old - m_new)
        p = jnp.exp2(s - (m_new - bias_const))
        l_ref[...] = alpha * l_ref[...] + jnp.sum(p, axis=0, keepdims=True)
        m_ref[...] = m_new
        pv = jnp.dot(vt_ref[ki], p.astype(BF16), preferred_element_type=F32)
        acc_ref[...] = acc_ref[...] * alpha + pv

    far_bucket = N_BUCKETS // 2 - 1
    c_left = rb_ref[far_bucket, h] * LOG2E
    c_right = rb_ref[N_BUCKETS // 2 + far_bucket, h] * LOG2E
    lo_near = jnp.maximum(qi - 1, 0)
    hi_near = jnp.minimum(qi + 2, nk)

    def far_left(ki, carry):
        chunk(ki, None, c_left)
        return carry

    def near(ki, carry):
        chunk(ki, band_ref[0, ki - qi + 1], 0.0)
        return carry

    def far_right(ki, carry):
        chunk(ki, None, c_right)
        return carry

    lax.fori_loop(0, lo_near, far_left, 0)
    lax.fori_loop(lo_near, hi_near, near, 0)
    lax.fori_loop(hi_near, nk, far_right, 0)

    lam = (jnp.exp(jnp.sum(lq1_ref[...] * lk1_ref[...], axis=1, keepdims=True))
           - jnp.exp(jnp.sum(lq2_ref[...] * lk2_ref[...], axis=1, keepdims=True)) + LAM_INIT)
    acc = acc_ref[...]
    l = l_ref[...]
    o = acc[:, 0:t] / l[:, 0:t] - lam * (acc[:, t:2 * t] / l[:, t:2 * t])
    ms = jnp.mean(o * o, axis=0, keepdims=True)
    on = o * lax.rsqrt(ms + NORM_EPS) * sg_ref[...] * (1.0 - LAM_INIT)
    o_ref[...] = on.T.astype(BF16)


def _attention(proj, band, rel_bias, lam_vecs, subln_g, batch, seq, row_off):
    t = T_ATT
    nq = seq // t
    qkv_cols = ATT_W // V_DIM
    lq1, lk1, lq2, lk2 = [v.astype(F32)[None, :] for v in lam_vecs]
    vec_spec = pl.BlockSpec((1, HEAD_DIM), lambda b, h, qi: (0, 0))
    return pl.pallas_call(
        functools.partial(_attn_kernel, seq),
        grid=(batch, N_HEADS, nq),
        in_specs=[
            pl.BlockSpec(memory_space=pltpu.SMEM),
            pl.BlockSpec((t, V_DIM), lambda b, h, qi: (row_off // t + b * nq + qi, h)),
            pl.BlockSpec((seq, V_DIM), lambda b, h, qi: (row_off // seq + b, qkv_cols + h)),
            pl.BlockSpec((seq, V_DIM), lambda b, h, qi: (row_off // seq + b, 2 * qkv_cols + h)),
            pl.BlockSpec((1, 3, t, t), lambda b, h, qi: (h, 0, 0, 0)),
            vec_spec, vec_spec, vec_spec, vec_spec,
            pl.BlockSpec((V_DIM, 1), lambda b, h, qi: (0, 0)),
        ],
        out_specs=pl.BlockSpec((t, V_DIM), lambda b, h, qi: (b * nq + qi, h)),
        out_shape=jax.ShapeDtypeStruct((batch * seq, ATT_W), BF16),
        scratch_shapes=[
            pltpu.VMEM((seq // t, V_DIM, t), BF16),
            pltpu.VMEM((2 * t, V_DIM), BF16),
            pltpu.VMEM((1, 2 * t), F32),
            pltpu.VMEM((1, 2 * t), F32),
            pltpu.VMEM((V_DIM, 2 * t), F32),
        ],
        compiler_params=pltpu.CompilerParams(
            dimension_semantics=("arbitrary", "arbitrary", "arbitrary"), vmem_limit_bytes=VMEM_LIMIT),
        name=f"diff_attention_s{seq}",
    )(rel_bias.astype(F32), proj, proj, proj, band, lq1, lk1, lq2, lk2, subln_g.astype(F32)[:, None])


def _merge_kernel(n_prompt_tiles, seq_p, seq_s,
                  attp_ref, atts_ref, u_ref, ul_ref, ur_ref, ga_ref, gb_ref, xp_ref, xs_ref,
                  wpg_ref, psc_ref, wao_ref, wpo_ref, wo_ref, g2_ref, wr_ref, br_ref,
                  h_ref, hn_ref, lg_ref):
    i = pl.program_id(0)
    tm = TM_OUT
    is_p = i < n_prompt_tiles
    seq = jnp.where(is_p, seq_p, seq_s)
    t0 = jnp.where(is_p, i, i - n_prompt_tiles) * tm
    pos0 = t0 % seq
    first = pos0 == 0
    last = pos0 + tm == seq

    u = u_ref[...]
    zero_halo = jnp.zeros(ul_ref.shape, BF16)
    ul = jnp.where(first, zero_halo, ul_ref[...])
    ur = jnp.where(last, zero_halo, ur_ref[...])
    uext = jnp.concatenate([ul, u, ur], axis=0)
    r = lax.broadcasted_iota(I32, (tm, tm + 2 * POOL_HALO), 0)
    c = lax.broadcasted_iota(I32, (tm, tm + 2 * POOL_HALO), 1) - POOL_HALO
    pos = pos0 + lax.broadcasted_iota(I32, (tm, 1), 0)
    mixed = []
    for gi, w in enumerate(POOL_WINDOWS):
        sl = slice(gi * POOL_GROUP_W, (gi + 1) * POOL_GROUP_W)
        band = jnp.where(c >= r - w // 2, jnp.where(c < r + (w - w // 2), 1.0, 0.0), 0.0).astype(BF16)
        wsum = jnp.dot(band, uext[:, sl], preferred_element_type=F32)
        cnt = (jnp.minimum(pos + (w - w // 2), seq) - jnp.maximum(pos - w // 2, 0)).astype(F32)
        pooled = wsum / cnt - u[:, sl].astype(F32)
        mg = jnp.dot(pooled.astype(BF16), wpg_ref[gi], preferred_element_type=F32)
        mixed.append((mg * psc_ref[:, sl]).astype(BF16))
    mixed = jnp.concatenate(mixed, axis=1)

    att = jnp.where(is_p, attp_ref[...], atts_ref[...])
    y_a = jnp.dot(att, wao_ref[...], preferred_element_type=F32)
    y_b = jnp.dot(mixed, wpo_ref[...], preferred_element_type=F32)
    merged = ga_ref[...].astype(F32) * y_a + gb_ref[...].astype(F32) * y_b
    x = jnp.where(is_p, xp_ref[...], xs_ref[...])
    h = x + jnp.dot(merged.astype(BF16), wo_ref[...], preferred_element_type=F32)
    h_ref[...] = h
    ms = jnp.mean(h * h, axis=-1, keepdims=True)
    hn = h * lax.rsqrt(ms + NORM_EPS) * g2_ref[...]
    hn_ref[...] = hn
    lg_ref[...] = jnp.dot(hn.astype(BF16), wr_ref[...], preferred_element_type=F32) + br_ref[...]


def _merge_project(att_p, att_s, proj, xp, xs, seq_p, seq_s, w_pool_grp, pool_scale, w_att_out, w_pool_out,
                   w_o, norm2_g, w_router, b_router):
    tm = TM_OUT
    n_p, n_s = xp.shape[0], xs.shape[0]
    n_tok = n_p + n_s
    npt = n_p // tm
    hb = tm // POOL_HALO
    n_hblk = n_tok // POOL_HALO
    u_col = 3 * ATT_W // POOL_W
    const = dict(pipeline_mode=pl.Buffered(1))
    return pl.pallas_call(
        functools.partial(_merge_kernel, npt, seq_p, seq_s),
        grid=(n_tok // tm,),
        in_specs=[
            pl.BlockSpec((tm, ATT_W), lambda i: (jnp.minimum(i, npt - 1), 0)),
            pl.BlockSpec((tm, ATT_W), lambda i: (jnp.maximum(i - npt, 0), 0)),
            pl.BlockSpec((tm, POOL_W), lambda i: (i, u_col)),
            pl.BlockSpec((POOL_HALO, POOL_W), lambda i: (jnp.maximum(i * hb - 1, 0), u_col)),
            pl.BlockSpec((POOL_HALO, POOL_W), lambda i: (jnp.minimum((i + 1) * hb, n_hblk - 1), u_col)),
            pl.BlockSpec((tm, D_MODEL), lambda i: (i, 2)),
            pl.BlockSpec((tm, D_MODEL), lambda i: (i, 3)),
            pl.BlockSpec((tm, D_MODEL), lambda i: (jnp.minimum(i, npt - 1), 0)),
            pl.BlockSpec((tm, D_MODEL), lambda i: (jnp.maximum(i - npt, 0), 0)),
            pl.BlockSpec((len(POOL_WINDOWS), POOL_GROUP_W, POOL_GROUP_W), lambda i: (0, 0, 0), **const),
            pl.BlockSpec((1, POOL_W), lambda i: (0, 0), **const),
            pl.BlockSpec((ATT_W, D_MODEL), lambda i: (0, 0), **const),
            pl.BlockSpec((POOL_W, D_MODEL), lambda i: (0, 0), **const),
            pl.BlockSpec((D_MODEL, D_MODEL), lambda i: (0, 0), **const),
            pl.BlockSpec((1, D_MODEL), lambda i: (0, 0), **const),
            pl.BlockSpec((D_MODEL, N_EXPERTS), lambda i: (0, 0), **const),
            pl.BlockSpec((1, N_EXPERTS), lambda i: (0, 0), **const),
        ],
        out_specs=[
            pl.BlockSpec((tm, D_MODEL), lambda i: (i, 0)),
            pl.BlockSpec((tm, D_MODEL), lambda i: (i, 0)),
            pl.BlockSpec((tm, N_EXPERTS), lambda i: (i, 0)),
        ],
        out_shape=[
            jax.ShapeDtypeStruct((n_tok, D_MODEL), F32),
            jax.ShapeDtypeStruct((n_tok, D_MODEL), F32),
            jax.ShapeDtypeStruct((n_tok, N_EXPERTS), F32),
        ],
        compiler_params=pltpu.CompilerParams(dimension_semantics=("arbitrary",), vmem_limit_bytes=VMEM_LIMIT),
        name="merge_project",
    )(att_p, att_s, proj, proj, proj, proj, proj, xp, xs,
      w_pool_grp.astype(BF16), pool_scale.astype(F32)[None, :], w_att_out.astype(BF16),
      w_pool_out.astype(BF16), w_o.astype(BF16), norm2_g.astype(F32)[None, :],
      w_router.astype(BF16), b_router.astype(F32)[None, :])


def _route_kernel(lg_ref, idx_ref, gate_ref, rank_ref, cnt_ref, run_ref):
    i = pl.program_id(0)
    tr = TR_ROUTE

    @pl.when(i == 0)
    def _():
        run_ref[...] = jnp.zeros(run_ref.shape, F32)

    cur = lg_ref[...]
    e_iota = lax.broadcasted_iota(I32, cur.shape, 1).astype(F32)
    member = jnp.zeros(cur.shape, F32)
    vals, idxs = [], []
    for _ in range(TOP_K):
        mx = jnp.max(cur, axis=1, keepdims=True)
        am = jnp.min(jnp.where(cur == mx, e_iota, float(N_EXPERTS)), axis=1, keepdims=True)
        hit = e_iota == am
        vals.append(mx)
        idxs.append(am)
        member = member + jnp.where(hit, 1.0, 0.0)
        cur = jnp.where(hit, -jnp.inf, cur)
    exps = [jnp.exp(v - vals[0]) for v in vals]
    denom = exps[0]
    for e in exps[1:]:
        denom = denom + e

    rr = lax.broadcasted_iota(I32, (tr, tr), 0)
    cc = lax.broadcasted_iota(I32, (tr, tr), 1)
    tri = jnp.where(cc < rr, 1.0, 0.0).astype(BF16)
    before = jnp.dot(tri, member.astype(BF16), preferred_element_type=F32) + run_ref[...]
    run_ref[...] = run_ref[...] + jnp.sum(member, axis=0, keepdims=True)
    cnt_ref[...] = run_ref[...]

    lane = lax.broadcasted_iota(I32, (tr, TOP_K), 1)
    idx_out = jnp.zeros((tr, TOP_K), F32)
    gate_out = jnp.zeros((tr, TOP_K), F32)
    rank_out = jnp.zeros((tr, TOP_K), F32)
    for k in range(TOP_K):
        rk = jnp.sum(jnp.where(e_iota == idxs[k], before, 0.0), axis=1, keepdims=True)
        idx_out = jnp.where(lane == k, idxs[k], idx_out)
        gate_out = jnp.where(lane == k, exps[k] / denom, gate_out)
        rank_out = jnp.where(lane == k, rk, rank_out)
    idx_ref[...] = idx_out.astype(I32)
    gate_ref[...] = gate_out
    rank_ref[...] = rank_out.astype(I32)


def _route(logits):
    n_tok = logits.shape[0]
    tr = TR_ROUTE
    return pl.pallas_call(
        _route_kernel,
        grid=(n_tok // tr,),
        in_specs=[pl.BlockSpec((tr, N_EXPERTS), lambda i: (i, 0))],
        out_specs=[
            pl.BlockSpec((tr, TOP_K), lambda i: (i, 0)),
            pl.BlockSpec((tr, TOP_K), lambda i: (i, 0)),
            pl.BlockSpec((tr, TOP_K), lambda i: (i, 0)),
            pl.BlockSpec((1, N_EXPERTS), lambda i: (0, 0)),
        ],
        out_shape=[
            jax.ShapeDtypeStruct((n_tok, TOP_K), I32),
            jax.ShapeDtypeStruct((n_tok, TOP_K), F32),
            jax.ShapeDtypeStruct((n_tok, TOP_K), I32),
            jax.ShapeDtypeStruct((1, N_EXPERTS), F32),
        ],
        scratch_shapes=[pltpu.VMEM((1, N_EXPERTS), F32)],
        compiler_params=pltpu.CompilerParams(dimension_semantics=("arbitrary",)),
        name="route",
    )(logits)


def _dispatch_kernel(dest_ref, hn_ref, xs_ref, sem):
    td = TD_DISP

    def row_copy(r, d):
        return pltpu.make_async_copy(hn_ref.at[pl.ds(r, 1)], xs_ref.at[pl.ds(d, 1)], sem)

    def issue(r, carry):
        for k in range(TOP_K):
            row_copy(r, dest_ref[r * TOP_K + k]).start()
        return carry

    def drain(r, carry):
        for k in range(TOP_K):
            row_copy(r, dest_ref[r * TOP_K + k]).wait()
        return carry

    lax.fori_loop(0, td, issue, 0)
    lax.fori_loop(0, td, drain, 0)


def _dispatch(hn, dest_flat, n_rows):
    n_tok = hn.shape[0]
    td = TD_DISP
    return pl.pallas_call(
        _dispatch_kernel,
        grid=(n_tok // td,),
        in_specs=[
            pl.BlockSpec((td * TOP_K,), lambda i: (i,), memory_space=pltpu.SMEM),
            pl.BlockSpec((td, D_MODEL), lambda i: (i, 0)),
        ],
        out_specs=pl.BlockSpec(memory_space=pl.ANY),
        out_shape=jax.ShapeDtypeStruct((n_rows, D_MODEL), F32),
        scratch_shapes=[pltpu.SemaphoreType.DMA(())],
        compiler_params=pltpu.CompilerParams(dimension_semantics=("arbitrary",), has_side_effects=True),
        name="dispatch",
    )(dest_flat, hn)


def _ffn_kernel(be_ref, nu_ref, bv_ref, x_ref, wg_ref, bg_ref, wu_ref, bu_ref, wd_ref, bd_ref,
                o_ref, xb_ref, acc_ref):
    b = pl.program_id(0)
    f = pl.program_id(1)
    nf = pl.num_programs(1)

    @pl.when(b < nu_ref[0])
    def _():
        @pl.when(f == 0)
        def _():
            row = lax.broadcasted_iota(I32, (TM_MOE, 1), 0)
            xb_ref[...] = jnp.where(row < bv_ref[b], x_ref[...], 0.0).astype(BF16)

        xb = xb_ref[...]
        g = jnp.dot(xb, wg_ref[0].astype(BF16), preferred_element_type=F32) + bg_ref[0]
        up = jnp.dot(xb, wu_ref[0].astype(BF16), preferred_element_type=F32) + bu_ref[0]
        g = jnp.minimum(g, SWIGLU_LIMIT)
        up = jnp.clip(up, -SWIGLU_LIMIT, SWIGLU_LIMIT)
        hmid = (up + 1.0) * (g * jax.nn.sigmoid(SWIGLU_ALPHA * g))
        part = jnp.dot(hmid.astype(BF16), wd_ref[0].astype(BF16), preferred_element_type=F32)

        @pl.when(f == 0)
        def _():
            acc_ref[...] = part + bd_ref[0]

        @pl.when(f > 0)
        def _():
            acc_ref[...] = acc_ref[...] + part

        @pl.when(f == nf - 1)
        def _():
            o_ref[...] = acc_ref[...]


def _expert_ffn(xs, blk_expert, n_used, blk_valid, w_gate, b_gate, w_up, b_up, w_down, b_down):
    n_rows = xs.shape[0]
    nb = n_rows // TM_MOE
    nf = D_FF // TF_MOE

    def blk(b, nu):
        return jnp.minimum(b, nu[0] - 1)

    def ftile(b, f, nu):
        return jnp.where(b < nu[0], f, nf - 1)

    grid_spec = pltpu.PrefetchScalarGridSpec(
        num_scalar_prefetch=3,
        grid=(nb, nf),
        in_specs=[
            pl.BlockSpec((TM_MOE, D_MODEL), lambda b, f, be, nu, bv: (blk(b, nu), 0)),
            pl.BlockSpec((1, D_MODEL, TF_MOE), lambda b, f, be, nu, bv: (be[blk(b, nu)], 0, ftile(b, f, nu))),
            pl.BlockSpec((1, 1, TF_MOE), lambda b, f, be, nu, bv: (be[blk(b, nu)], 0, ftile(b, f, nu))),
            pl.BlockSpec((1, D_MODEL, TF_MOE), lambda b, f, be, nu, bv: (be[blk(b, nu)], 0, ftile(b, f, nu))),
            pl.BlockSpec((1, 1, TF_MOE), lambda b, f, be, nu, bv: (be[blk(b, nu)], 0, ftile(b, f, nu))),
            pl.BlockSpec((1, TF_MOE, D_MODEL), lambda b, f, be, nu, bv: (be[blk(b, nu)], ftile(b, f, nu), 0)),
            pl.BlockSpec((1, 1, D_MODEL), lambda b, f, be, nu, bv: (be[blk(b, nu)], 0, 0)),
        ],
        out_specs=pl.BlockSpec((TM_MOE, D_MODEL), lambda b, f, be, nu, bv: (blk(b, nu), 0)),
        scratch_shapes=[pltpu.VMEM((TM_MOE, D_MODEL), BF16), pltpu.VMEM((TM_MOE, D_MODEL), F32)],
    )
    return pl.pallas_call(
        _ffn_kernel,
        grid_spec=grid_spec,
        out_shape=jax.ShapeDtypeStruct((n_rows, D_MODEL), F32),
        compiler_params=pltpu.CompilerParams(
            dimension_semantics=("arbitrary", "arbitrary"), vmem_limit_bytes=VMEM_LIMIT),
        name="expert_ffn",
    )(blk_expert, n_used, blk_valid, xs, w_gate, b_gate[:, None, :], w_up, b_up[:, None, :],
      w_down, b_down[:, None, :])


def _combine_kernel(dest_ref, h_ref, gate_ref, ys_ref, o_ref, buf_ref, sem):
    tc = TC_COMB

    def row_copy(r, k, d):
        return pltpu.make_async_copy(ys_ref.at[pl.ds(d, 1)], buf_ref.at[k, pl.ds(r, 1)], sem)

    def issue(r, carry):
        for k in range(TOP_K):
            row_copy(r, k, dest_ref[r * TOP_K + k]).start()
        return carry

    def drain(r, carry):
        for k in range(TOP_K):
            row_copy(r, k, dest_ref[r * TOP_K + k]).wait()
        return carry

    lax.fori_loop(0, tc, issue, 0)
    lax.fori_loop(0, tc, drain, 0)
    gates = gate_ref[...]
    y = h_ref[...]
    for k in range(TOP_K):
        y = y + gates[:, k:k + 1] * buf_ref[k]
    o_ref[...] = y


def _combine(h, gates, dest_flat, ys, tok_off, n_out):
    tc = TC_COMB
    off = tok_off // tc
    return pl.pallas_call(
        _combine_kernel,
        grid=(n_out // tc,),
        in_specs=[
            pl.BlockSpec((tc * TOP_K,), lambda i: (i + off,), memory_space=pltpu.SMEM),
            pl.BlockSpec((tc, D_MODEL), lambda i: (i + off, 0)),
            pl.BlockSpec((tc, TOP_K), lambda i: (i + off, 0)),
            pl.BlockSpec(memory_space=pl.ANY),
        ],
        out_specs=pl.BlockSpec((tc, D_MODEL), lambda i: (i, 0)),
        out_shape=jax.ShapeDtypeStruct((n_out, D_MODEL), F32),
        scratch_shapes=[pltpu.VMEM((TOP_K, tc, D_MODEL), F32), pltpu.SemaphoreType.DMA(())],
        compiler_params=pltpu.CompilerParams(dimension_semantics=("arbitrary",), vmem_limit_bytes=VMEM_LIMIT),
        name="combine",
    )(dest_flat, h, gates, ys)


def kernel(x_prompt, x_sample, norm1_g, w_in, q_norm_g, k_norm_g, lambda_q1, lambda_k1, lambda_q2, lambda_k2,
           subln_g, rel_bias, w_pool_grp, pool_scale, w_att_out, w_pool_out, w_o, norm2_g, w_router, b_router,
           w_gate, b_gate, w_up, b_up, w_down, b_down):
    bp, sp, _ = x_prompt.shape
    bs, ss, _ = x_sample.shape
    n_p, n_s = bp * sp, bs * ss
    n_tok = n_p + n_s
    xp = x_prompt.reshape(n_p, D_MODEL)
    xs = x_sample.reshape(n_s, D_MODEL)

    proj = _in_projection(xp, xs, norm1_g[0], w_in[0].astype(BF16), q_norm_g[0], k_norm_g[0])
    band = _bias_band(rel_bias)
    lam_vecs = (lambda_q1[0], lambda_k1[0], lambda_q2[0], lambda_k2[0])
    att_p = _attention(proj, band, rel_bias, lam_vecs, subln_g[0], bp, sp, 0)
    att_s = _attention(proj, band, rel_bias, lam_vecs, subln_g[0], bs, ss, n_p)
    h, hn, logits = _merge_project(att_p, att_s, proj, xp, xs, sp, ss, w_pool_grp[0], pool_scale[0],
                                   w_att_out[0], w_pool_out[0], w_o[0], norm2_g[0], w_router[0], b_router[0])

    idx, gates, rank, counts = _route(logits)

    n_blocks = (n_tok * TOP_K + N_EXPERTS * (TM_MOE - 1)) // TM_MOE
    counts = counts[0].astype(I32)
    padded = (counts + TM_MOE - 1) // TM_MOE * TM_MOE
    pad_end = jnp.cumsum(padded)
    start_pad = pad_end - padded
    dest = (start_pad[idx] + rank).reshape(-1)
    blk_start = jnp.arange(n_blocks, dtype=I32) * TM_MOE
    blk_expert = jnp.minimum(jnp.searchsorted(pad_end, blk_start, side="right"), N_EXPERTS - 1).astype(I32)
    blk_valid = jnp.clip(counts[blk_expert] - (blk_start - start_pad[blk_expert]), 0, TM_MOE).astype(I32)
    n_used = (pad_end[-1:] // TM_MOE).astype(I32)

    xs_sorted = _dispatch(hn, dest, n_blocks * TM_MOE)
    ys = _expert_ffn(xs_sorted, blk_expert, n_used, blk_valid,
                     w_gate[0], b_gate[0], w_up[0], b_up[0], w_down[0], b_down[0])
    y_p = _combine(h, gates, dest, ys, 0, n_p)
    y_s = _combine(h, gates, dest, ys, n_p, n_s)
    return (y_p.reshape(bp, sp, D_MODEL), y_s.reshape(bs, ss, D_MODEL))
```

```python
import functools
import math

import jax
import jax.numpy as jnp
from jax import lax
from jax.experimental import pallas as pl
from jax.experimental.pallas import tpu as pltpu

F32 = jnp.float32
BF16 = jnp.bfloat16
I32 = jnp.int32

D_MODEL = 2048
N_HEADS = 8
HEAD_DIM = 64
V_DIM = 2 * HEAD_DIM
ATT_W = N_HEADS * V_DIM
POOL_WINDOWS = (2, 4, 8, 16)
POOL_W = D_MODEL // 2
POOL_GROUP_W = POOL_W // len(POOL_WINDOWS)
IN_COLS = 3 * ATT_W + POOL_W + 2 * D_MODEL
N_BUCKETS = 32
MAX_DISTANCE = 128
N_EXPERTS = 32
TOP_K = 4
D_FF = D_MODEL
SWIGLU_LIMIT = 7.0
SWIGLU_ALPHA = 1.702
NORM_EPS = 1e-6
LAM_INIT = 0.8 - 0.6 * math.exp(-0.3 * 0)
LOG2E = math.log2(math.e)

LANES = 128
VMEM_LIMIT = 56 * 1024 * 1024

TM_IN = 512
TN_IN = 1024
T_ATT = 256
G_ATT = 4
TM_OUT = 256
POOL_HALO = 16
TR_ROUTE = 512
TM_MOE = 512
TF_MOE = 512
TD_DISP = 256
TC_COMB = 128


def _t5_thresholds():
    half = N_BUCKETS // 2
    max_exact = half // 2
    steps = half - max_exact
    ratio = MAX_DISTANCE // max_exact
    out = []
    for k in range(1, steps):
        n = max_exact
        while n ** steps < (max_exact ** steps) * (ratio ** k):
            n += 1
        out.append(n)
    return tuple(out)


T5_THRESHOLDS = _t5_thresholds()
T5_FAR = T5_THRESHOLDS[-1]


def _inproj_kernel(n_prompt_tiles, xp_ref, xs_ref, g1_ref, w_ref, gsum_ref, qg_ref, kg_ref, o_ref, xn_ref):
    i = pl.program_id(0)
    j = pl.program_id(1)

    @pl.when(j == 0)
    def _():
        x = jnp.where(i < n_prompt_tiles, xp_ref[...], xs_ref[...])
        ms = jnp.mean(x * x, axis=-1, keepdims=True)
        xn_ref[...] = (x * lax.rsqrt(ms + NORM_EPS) * g1_ref[...]).astype(BF16)

    acc = jnp.dot(xn_ref[...], w_ref[...], preferred_element_type=F32)

    def head_norm(g_ref, scale):
        sq = (acc * acc).astype(BF16)
        gw = gsum_ref.shape[0]
        parts = [jnp.dot(sq[:, c * gw:(c + 1) * gw], gsum_ref[...], preferred_element_type=F32)
                 for c in range(TN_IN // gw)]
        ss = jnp.concatenate(parts, axis=1)
        return acc * lax.rsqrt(ss * (1.0 / HEAD_DIM) + NORM_EPS) * (g_ref[...] * scale)

    @pl.when(j == 0)
    def _():
        o_ref[...] = head_norm(qg_ref, HEAD_DIM ** -0.5 * LOG2E).astype(BF16)

    @pl.when(j == 1)
    def _():
        o_ref[...] = head_norm(kg_ref, 1.0).astype(BF16)

    @pl.when((j == 2) | (j == 3))
    def _():
        o_ref[...] = acc.astype(BF16)

    @pl.when(j >= 4)
    def _():
        o_ref[...] = jax.nn.sigmoid(acc).astype(BF16)


def _in_projection(xp, xs, norm1_g, w_in_bf, q_norm_g, k_norm_g):
    n_p, n_s = xp.shape[0], xs.shape[0]
    n_tok = n_p + n_s
    npt, nst = n_p // TM_IN, n_s // TM_IN
    gw = 2 * LANES
    gid = jnp.arange(gw) // HEAD_DIM
    gsum = (gid[:, None] == gid[None, :]).astype(BF16)
    qg = jnp.tile(q_norm_g.astype(F32), TN_IN // HEAD_DIM)[None, :]
    kg = jnp.tile(k_norm_g.astype(F32), TN_IN // HEAD_DIM)[None, :]
    return pl.pallas_call(
        functools.partial(_inproj_kernel, npt),
        grid=(n_tok // TM_IN, IN_COLS // TN_IN),
        in_specs=[
            pl.BlockSpec((TM_IN, D_MODEL), lambda i, j: (jnp.minimum(i, npt - 1), 0)),
            pl.BlockSpec((TM_IN, D_MODEL), lambda i, j: (jnp.maximum(i - npt, 0), 0)),
            pl.BlockSpec((1, D_MODEL), lambda i, j: (0, 0)),
            pl.BlockSpec((D_MODEL, TN_IN), lambda i, j: (0, j)),
            pl.BlockSpec((gw, gw), lambda i, j: (0, 0)),
            pl.BlockSpec((1, TN_IN), lambda i, j: (0, 0)),
            pl.BlockSpec((1, TN_IN), lambda i, j: (0, 0)),
        ],
        out_specs=pl.BlockSpec((TM_IN, TN_IN), lambda i, j: (i, j)),
        out_shape=jax.ShapeDtypeStruct((n_tok, IN_COLS), BF16),
        scratch_shapes=[pltpu.VMEM((TM_IN, D_MODEL), BF16)],
        compiler_params=pltpu.CompilerParams(
            dimension_semantics=("arbitrary", "arbitrary"), vmem_limit_bytes=VMEM_LIMIT),
        name="in_projection",
    )(xp, xs, norm1_g.astype(F32)[None, :], w_in_bf, gsum, qg, kg)


N_BAND = 5


def _band_kernel(rb_ref, o_ref):
    h = pl.program_id(0)
    t = T_ATT
    kj = lax.broadcasted_iota(I32, (t, t), 0)
    qi = lax.broadcasted_iota(I32, (t, t), 1)
    half = N_BUCKETS // 2
    max_exact = half // 2
    for d in range(N_BAND):
        rel = (d - N_BAND // 2) * t + kj - qi
        n = jnp.abs(rel)
        large = jnp.full((t, t), max_exact, I32)
        for th in T5_THRESHOLDS:
            large = large + jnp.where(n >= th, 1, 0)
        bucket = jnp.where(rel > 0, half, 0) + jnp.where(n < max_exact, n, large)
        val = jnp.zeros((t, t), F32)
        for b in range(N_BUCKETS):
            val = jnp.where(bucket == b, rb_ref[b, h], val)
        o_ref[0, d] = val * LOG2E


def _bias_band(rel_bias):
    assert T_ATT >= T5_FAR, "tiles two or more away from the diagonal must lie in the constant-bias region"
    return pl.pallas_call(
        _band_kernel,
        grid=(N_HEADS,),
        in_specs=[pl.BlockSpec(memory_space=pltpu.SMEM)],
        out_specs=pl.BlockSpec((1, N_BAND, T_ATT, T_ATT), lambda h: (h, 0, 0, 0)),
        out_shape=jax.ShapeDtypeStruct((N_HEADS, N_BAND, T_ATT, T_ATT), F32),
        compiler_params=pltpu.CompilerParams(dimension_semantics=("arbitrary",)),
        name="bias_band",
    )(rel_bias.astype(F32))


def _attn_kernel(seq, q_ref, k_ref, v_ref, band_ref, lq1_ref, lk1_ref, lq2_ref, lk2_ref, sg_ref,
                 o_ref, vt_ref, qz_ref, m_ref, l_ref, acc_ref, s0_ref, s1_ref):
    qi = pl.program_id(2)
    t = T_ATT
    nk = seq // t
    heads = range(G_ATT)

    def cols(g):
        return slice(g * V_DIM, (g + 1) * V_DIM)

    @pl.when(qi == 0)
    def _():
        for g in heads:
            for c in range(nk):
                vt_ref[g, c] = v_ref[c * t:(c + 1) * t, cols(g)].astype(F32).T.astype(BF16)

    for g in heads:
        q = q_ref[:, cols(g)]
        lane = lax.broadcasted_iota(I32, q.shape, 1)
        qz_ref[g, 0:t, :] = jnp.where(lane < HEAD_DIM, q, jnp.zeros_like(q))
        qz_ref[g, t:2 * t, :] = jnp.where(lane >= HEAD_DIM, q, jnp.zeros_like(q))
    m_ref[...] = jnp.full(m_ref.shape, -1e30, F32)
    l_ref[...] = jnp.zeros(l_ref.shape, F32)
    acc_ref[...] = jnp.zeros(acc_ref.shape, F32)

    def score_tiles(ki, s_ref):
        for g in heads:
            k_c = k_ref[pl.ds(pl.multiple_of(ki * t, t), t), cols(g)]
            s_ref[g] = lax.dot_general(k_c, qz_ref[g], (((1,), (1,)), ((), ())),
                                       preferred_element_type=F32)

    def softmax_pv(ki, s_ref):
        band = jnp.clip(ki - qi + N_BAND // 2, 0, N_BAND - 1)
        probs, alphas = [], []
        for g in heads:
            bias = band_ref[g, band]
            s = s_ref[g] + jnp.concatenate([bias, bias], axis=1)
            m_old = m_ref[g]
            m_new = jnp.maximum(m_old, jnp.max(s, axis=0, keepdims=True))
            alpha = jnp.exp2(m_old - m_new)
            p = jnp.exp2(s - m_new)
            l_ref[g] = alpha * l_ref[g] + jnp.sum(p, axis=0, keepdims=True)
            m_ref[g] = m_new
            probs.append(p.astype(BF16))
            alphas.append(alpha)
        for g in heads:
            pv = jnp.dot(vt_ref[g, ki], probs[g], preferred_element_type=F32)
            acc_ref[g] = acc_ref[g] * alphas[g] + pv

    score_tiles(0, s0_ref)

    def pair(j, carry):
        k0 = 2 * j
        score_tiles(k0 + 1, s1_ref)
        softmax_pv(k0, s0_ref)
        score_tiles(jnp.minimum(k0 + 2, nk - 1), s0_ref)
        softmax_pv(k0 + 1, s1_ref)
        return carry

    lax.fori_loop(0, nk // 2, pair, 0)

    lam = (jnp.exp(jnp.sum(lq1_ref[...] * lk1_ref[...], axis=1, keepdims=True))
           - jnp.exp(jnp.sum(lq2_ref[...] * lk2_ref[...], axis=1, keepdims=True)) + LAM_INIT)
    for g in heads:
        acc = acc_ref[g]
        l = l_ref[g]
        o = acc[:, 0:t] / l[:, 0:t] - lam * (acc[:, t:2 * t] / l[:, t:2 * t])
        ms = jnp.mean(o * o, axis=0, keepdims=True)
        on = o * lax.rsqrt(ms + NORM_EPS) * sg_ref[...] * (1.0 - LAM_INIT)
        o_ref[:, cols(g)] = on.T.astype(BF16)


def _attention(proj, band, lam_vecs, subln_g, batch, seq, row_off):
    t = T_ATT
    nq = seq // t
    assert nq % 2 == 0, "key tiles are processed in pairs"
    gw = G_ATT * V_DIM
    sec = ATT_W // gw
    lq1, lk1, lq2, lk2 = [v.astype(F32)[None, :] for v in lam_vecs]
    vec_spec = pl.BlockSpec((1, HEAD_DIM), lambda b, h, qi: (0, 0))
    return pl.pallas_call(
        functools.partial(_attn_kernel, seq),
        grid=(batch, N_HEADS // G_ATT, nq),
        in_specs=[
            pl.BlockSpec((t, gw), lambda b, h, qi: (row_off // t + b * nq + qi, h)),
            pl.BlockSpec((seq, gw), lambda b, h, qi: (row_off // seq + b, sec + h)),
            pl.BlockSpec((seq, gw), lambda b, h, qi: (row_off // seq + b, 2 * sec + h)),
            pl.BlockSpec((G_ATT, N_BAND, t, t), lambda b, h, qi: (h, 0, 0, 0)),
            vec_spec, vec_spec, vec_spec, vec_spec,
            pl.BlockSpec((V_DIM, 1), lambda b, h, qi: (0, 0)),
        ],
        out_specs=pl.BlockSpec((t, gw), lambda b, h, qi: (b * nq + qi, h)),
        out_shape=jax.ShapeDtypeStruct((batch * seq, ATT_W), BF16),
        scratch_shapes=[
            pltpu.VMEM((G_ATT, seq // t, V_DIM, t), BF16),
            pltpu.VMEM((G_ATT, 2 * t, V_DIM), BF16),
            pltpu.VMEM((G_ATT, 1, 2 * t), F32),
            pltpu.VMEM((G_ATT, 1, 2 * t), F32),
            pltpu.VMEM((G_ATT, V_DIM, 2 * t), F32),
            pltpu.VMEM((G_ATT, t, 2 * t), F32),
            pltpu.VMEM((G_ATT, t, 2 * t), F32),
        ],
        compiler_params=pltpu.CompilerParams(
            dimension_semantics=("arbitrary", "arbitrary", "arbitrary"), vmem_limit_bytes=VMEM_LIMIT),
        name=f"diff_attention_s{seq}",
    )(proj, proj, proj, band, lq1, lk1, lq2, lk2, subln_g.astype(F32)[:, None])


def _merge_kernel(n_prompt_tiles, seq_p, seq_s,
                  attp_ref, atts_ref, u_ref, ul_ref, ur_ref, ga_ref, gb_ref, xp_ref, xs_ref,
                  wpg_ref, psc_ref, wao_ref, wpo_ref, wo_ref, g2_ref, wr_ref, br_ref,
                  h_ref, hn_ref, lg_ref):
    i = pl.program_id(0)
    tm = TM_OUT
    is_p = i < n_prompt_tiles
    seq = jnp.where(is_p, seq_p, seq_s)
    t0 = jnp.where(is_p, i, i - n_prompt_tiles) * tm
    pos0 = t0 % seq
    first = pos0 == 0
    last = pos0 + tm == seq

    u = u_ref[...]
    zero_halo = jnp.zeros(ul_ref.shape, BF16)
    ul = jnp.where(first, zero_halo, ul_ref[...])
    ur = jnp.where(last, zero_halo, ur_ref[...])
    uext = jnp.concatenate([ul, u, ur], axis=0)
    r = lax.broadcasted_iota(I32, (tm, tm + 2 * POOL_HALO), 0)
    c = lax.broadcasted_iota(I32, (tm, tm + 2 * POOL_HALO), 1) - POOL_HALO
    pos = pos0 + lax.broadcasted_iota(I32, (tm, 1), 0)
    mixed = []
    for gi, w in enumerate(POOL_WINDOWS):
        sl = slice(gi * POOL_GROUP_W, (gi + 1) * POOL_GROUP_W)
        band = jnp.where(c >= r - w // 2, jnp.where(c < r + (w - w // 2), 1.0, 0.0), 0.0).astype(BF16)
        wsum = jnp.dot(band, uext[:, sl], preferred_element_type=F32)
        cnt = (jnp.minimum(pos + (w - w // 2), seq) - jnp.maximum(pos - w // 2, 0)).astype(F32)
        pooled = wsum / cnt - u[:, sl].astype(F32)
        mg = jnp.dot(pooled.astype(BF16), wpg_ref[gi], preferred_element_type=F32)
        mixed.append((mg * psc_ref[:, sl]).astype(BF16))
    mixed = jnp.concatenate(mixed, axis=1)

    att = jnp.where(is_p, attp_ref[...], atts_ref[...])
    y_a = jnp.dot(att, wao_ref[...], preferred_element_type=F32)
    y_b = jnp.dot(mixed, wpo_ref[...], preferred_element_type=F32)
    merged = ga_ref[...].astype(F32) * y_a + gb_ref[...].astype(F32) * y_b
    x = jnp.where(is_p, xp_ref[...], xs_ref[...])
    h = x + jnp.dot(merged.astype(BF16), wo_ref[...], preferred_element_type=F32)
    h_ref[...] = h
    ms = jnp.mean(h * h, axis=-1, keepdims=True)
    hn = h * lax.rsqrt(ms + NORM_EPS) * g2_ref[...]
    hn_ref[...] = hn
    lg_ref[...] = jnp.dot(hn.astype(BF16), wr_ref[...], preferred_element_type=F32) + br_ref[...]


def _merge_project(att_p, att_s, proj, xp, xs, seq_p, seq_s, w_pool_grp, pool_scale, w_att_out, w_pool_out,
                   w_o, norm2_g, w_router, b_router):
    tm = TM_OUT
    n_p, n_s = xp.shape[0], xs.shape[0]
    n_tok = n_p + n_s
    npt = n_p // tm
    hb = tm // POOL_HALO
    n_hblk = n_tok // POOL_HALO
    u_col = 3 * ATT_W // POOL_W
    const = dict(pipeline_mode=pl.Buffered(1))
    return pl.pallas_call(
        functools.partial(_merge_kernel, npt, seq_p, seq_s),
        grid=(n_tok // tm,),
        in_specs=[
            pl.BlockSpec((tm, ATT_W), lambda i: (jnp.minimum(i, npt - 1), 0)),
            pl.BlockSpec((tm, ATT_W), lambda i: (jnp.maximum(i - npt, 0), 0)),
            pl.BlockSpec((tm, POOL_W), lambda i: (i, u_col)),
            pl.BlockSpec((POOL_HALO, POOL_W), lambda i: (jnp.maximum(i * hb - 1, 0), u_col)),
            pl.BlockSpec((POOL_HALO, POOL_W), lambda i: (jnp.minimum((i + 1) * hb, n_hblk - 1), u_col)),
            pl.BlockSpec((tm, D_MODEL), lambda i: (i, 2)),
            pl.BlockSpec((tm, D_MODEL), lambda i: (i, 3)),
            pl.BlockSpec((tm, D_MODEL), lambda i: (jnp.minimum(i, npt - 1), 0)),
            pl.BlockSpec((tm, D_MODEL), lambda i: (jnp.maximum(i - npt, 0), 0)),
            pl.BlockSpec((len(POOL_WINDOWS), POOL_GROUP_W, POOL_GROUP_W), lambda i: (0, 0, 0), **const),
            pl.BlockSpec((1, POOL_W), lambda i: (0, 0), **const),
            pl.BlockSpec((ATT_W, D_MODEL), lambda i: (0, 0), **const),
            pl.BlockSpec((POOL_W, D_MODEL), lambda i: (0, 0), **const),
            pl.BlockSpec((D_MODEL, D_MODEL), lambda i: (0, 0), **const),
            pl.BlockSpec((1, D_MODEL), lambda i: (0, 0), **const),
            pl.BlockSpec((D_MODEL, N_EXPERTS), lambda i: (0, 0), **const),
            pl.BlockSpec((1, N_EXPERTS), lambda i: (0, 0), **const),
        ],
        out_specs=[
            pl.BlockSpec((tm, D_MODEL), lambda i: (i, 0)),
            pl.BlockSpec((tm, D_MODEL), lambda i: (i, 0)),
            pl.BlockSpec((tm, N_EXPERTS), lambda i: (i, 0)),
        ],
        out_shape=[
            jax.ShapeDtypeStruct((n_tok, D_MODEL), F32),
            jax.ShapeDtypeStruct((n_tok, D_MODEL), F32),
            jax.ShapeDtypeStruct((n_tok, N_EXPERTS), F32),
        ],
        compiler_params=pltpu.CompilerParams(dimension_semantics=("arbitrary",), vmem_limit_bytes=VMEM_LIMIT),
        name="merge_project",
    )(att_p, att_s, proj, proj, proj, proj, proj, xp, xs,
      w_pool_grp.astype(BF16), pool_scale.astype(F32)[None, :], w_att_out.astype(BF16),
      w_pool_out.astype(BF16), w_o.astype(BF16), norm2_g.astype(F32)[None, :],
      w_router.astype(BF16), b_router.astype(F32)[None, :])


def _route_kernel(lg_ref, idx_ref, gate_ref, rank_ref, cnt_ref, run_ref):
    i = pl.program_id(0)
    tr = TR_ROUTE

    @pl.when(i == 0)
    def _():
        run_ref[...] = jnp.zeros(run_ref.shape, F32)

    cur = lg_ref[...]
    e_iota = lax.broadcasted_iota(I32, cur.shape, 1).astype(F32)
    member = jnp.zeros(cur.shape, F32)
    vals, idxs = [], []
    for _ in range(TOP_K):
        mx = jnp.max(cur, axis=1, keepdims=True)
        am = jnp.min(jnp.where(cur == mx, e_iota, float(N_EXPERTS)), axis=1, keepdims=True)
        hit = e_iota == am
        vals.append(mx)
        idxs.append(am)
        member = member + jnp.where(hit, 1.0, 0.0)
        cur = jnp.where(hit, -jnp.inf, cur)
    exps = [jnp.exp(v - vals[0]) for v in vals]
    denom = exps[0]
    for e in exps[1:]:
        denom = denom + e

    rr = lax.broadcasted_iota(I32, (tr, tr), 0)
    cc = lax.broadcasted_iota(I32, (tr, tr), 1)
    tri = jnp.where(cc < rr, 1.0, 0.0).astype(BF16)
    before = jnp.dot(tri, member.astype(BF16), preferred_element_type=F32) + run_ref[...]
    run_ref[...] = run_ref[...] + jnp.sum(member, axis=0, keepdims=True)
    cnt_ref[...] = run_ref[...]

    lane = lax.broadcasted_iota(I32, (tr, TOP_K), 1)
    idx_out = jnp.zeros((tr, TOP_K), F32)
    gate_out = jnp.zeros((tr, TOP_K), F32)
    rank_out = jnp.zeros((tr, TOP_K), F32)
    for k in range(TOP_K):
        rk = jnp.sum(jnp.where(e_iota == idxs[k], before, 0.0), axis=1, keepdims=True)
        idx_out = jnp.where(lane == k, idxs[k], idx_out)
        gate_out = jnp.where(lane == k, exps[k] / denom, gate_out)
        rank_out = jnp.where(lane == k, rk, rank_out)
    idx_ref[...] = idx_out.astype(I32)
    gate_ref[...] = gate_out
    rank_ref[...] = rank_out.astype(I32)


def _route(logits):
    n_tok = logits.shape[0]
    tr = TR_ROUTE
    return pl.pallas_call(
        _route_kernel,
        grid=(n_tok // tr,),
        in_specs=[pl.BlockSpec((tr, N_EXPERTS), lambda i: (i, 0))],
        out_specs=[
            pl.BlockSpec((tr, TOP_K), lambda i: (i, 0)),
            pl.BlockSpec((tr, TOP_K), lambda i: (i, 0)),
            pl.BlockSpec((tr, TOP_K), lambda i: (i, 0)),
            pl.BlockSpec((1, N_EXPERTS), lambda i: (0, 0)),
        ],
        out_shape=[
            jax.ShapeDtypeStruct((n_tok, TOP_K), I32),
            jax.ShapeDtypeStruct((n_tok, TOP_K), F32),
            jax.ShapeDtypeStruct((n_tok, TOP_K), I32),
            jax.ShapeDtypeStruct((1, N_EXPERTS), F32),
        ],
        scratch_shapes=[pltpu.VMEM((1, N_EXPERTS), F32)],
        compiler_params=pltpu.CompilerParams(dimension_semantics=("arbitrary",)),
        name="route",
    )(logits)


def _dispatch_kernel(dest_ref, hn_ref, xs_ref, sem):
    td = TD_DISP

    def row_copy(r, d):
        return pltpu.make_async_copy(hn_ref.at[pl.ds(r, 1)], xs_ref.at[pl.ds(d, 1)], sem)

    def issue(r, carry):
        for k in range(TOP_K):
            row_copy(r, dest_ref[r * TOP_K + k]).start()
        return carry

    def drain(r, carry):
        for k in range(TOP_K):
            row_copy(r, dest_ref[r * TOP_K + k]).wait()
        return carry

    lax.fori_loop(0, td, issue, 0)
    lax.fori_loop(0, td, drain, 0)


def _dispatch(hn, dest_flat, n_rows):
    n_tok = hn.shape[0]
    td = TD_DISP
    return pl.pallas_call(
        _dispatch_kernel,
        grid=(n_tok // td,),
        in_specs=[
            pl.BlockSpec((td * TOP_K,), lambda i: (i,), memory_space=pltpu.SMEM),
            pl.BlockSpec((td, D_MODEL), lambda i: (i, 0)),
        ],
        out_specs=pl.BlockSpec(memory_space=pl.ANY),
        out_shape=jax.ShapeDtypeStruct((n_rows, D_MODEL), F32),
        scratch_shapes=[pltpu.SemaphoreType.DMA(())],
        compiler_params=pltpu.CompilerParams(dimension_semantics=("arbitrary",), has_side_effects=True),
        name="dispatch",
    )(dest_flat, hn)


def _ffn_kernel(be_ref, nu_ref, bv_ref, x_ref, wg_ref, bg_ref, wu_ref, bu_ref, wd_ref, bd_ref,
                o_ref, xb_ref):
    b = pl.program_id(0)
    f = pl.program_id(1)

    @pl.when(b < nu_ref[0])
    def _():
        @pl.when(f == 0)
        def _():
            row = lax.broadcasted_iota(I32, (TM_MOE, 1), 0)
            xb_ref[...] = jnp.where(row < bv_ref[b], x_ref[...], 0.0).astype(BF16)
            o_ref[...] = jnp.broadcast_to(bd_ref[0], o_ref.shape)

        xb = xb_ref[...]
        g = jnp.dot(xb, wg_ref[0].astype(BF16), preferred_element_type=F32) + bg_ref[0]
        up = jnp.dot(xb, wu_ref[0].astype(BF16), preferred_element_type=F32) + bu_ref[0]
        g = jnp.minimum(g, SWIGLU_LIMIT)
        up = jnp.clip(up, -SWIGLU_LIMIT, SWIGLU_LIMIT)
        hmid = (up + 1.0) * (g * jax.nn.sigmoid(SWIGLU_ALPHA * g))
        o_ref[...] += jnp.dot(hmid.astype(BF16), wd_ref[0].astype(BF16), preferred_element_type=F32)


def _expert_ffn(xs, blk_expert, n_used, blk_valid, w_gate, b_gate, w_up, b_up, w_down, b_down):
    n_rows = xs.shape[0]
    nb = n_rows // TM_MOE
    nf = D_FF // TF_MOE

    def blk(b, nu):
        return jnp.minimum(b, nu[0] - 1)

    def ftile(b, f, nu):
        return jnp.where(b < nu[0], f, nf - 1)

    grid_spec = pltpu.PrefetchScalarGridSpec(
        num_scalar_prefetch=3,
        grid=(nb, nf),
        in_specs=[
            pl.BlockSpec((TM_MOE, D_MODEL), lambda b, f, be, nu, bv: (blk(b, nu), 0)),
            pl.BlockSpec((1, D_MODEL, TF_MOE), lambda b, f, be, nu, bv: (be[blk(b, nu)], 0, ftile(b, f, nu))),
            pl.BlockSpec((1, 1, TF_MOE), lambda b, f, be, nu, bv: (be[blk(b, nu)], 0, ftile(b, f, nu))),
            pl.BlockSpec((1, D_MODEL, TF_MOE), lambda b, f, be, nu, bv: (be[blk(b, nu)], 0, ftile(b, f, nu))),
            pl.BlockSpec((1, 1, TF_MOE), lambda b, f, be, nu, bv: (be[blk(b, nu)], 0, ftile(b, f, nu))),
            pl.BlockSpec((1, TF_MOE, D_MODEL), lambda b, f, be, nu, bv: (be[blk(b, nu)], ftile(b, f, nu), 0)),
            pl.BlockSpec((1, 1, D_MODEL), lambda b, f, be, nu, bv: (be[blk(b, nu)], 0, 0)),
        ],
        out_specs=pl.BlockSpec((TM_MOE, D_MODEL), lambda b, f, be, nu, bv: (blk(b, nu), 0)),
        scratch_shapes=[pltpu.VMEM((TM_MOE, D_MODEL), BF16)],
    )
    return pl.pallas_call(
        _ffn_kernel,
        grid_spec=grid_spec,
        out_shape=jax.ShapeDtypeStruct((n_rows, D_MODEL), F32),
        compiler_params=pltpu.CompilerParams(
            dimension_semantics=("arbitrary", "arbitrary"), vmem_limit_bytes=VMEM_LIMIT),
        name="expert_ffn",
    )(blk_expert, n_used, blk_valid, xs, w_gate, b_gate[:, None, :], w_up, b_up[:, None, :],
      w_down, b_down[:, None, :])


def _combine_kernel(dest_ref, h_ref, gate_ref, ys_ref, o_ref, buf_ref, sem):
    tc = TC_COMB

    def row_copy(r, k, d):
        return pltpu.make_async_copy(ys_ref.at[pl.ds(d, 1)], buf_ref.at[k, pl.ds(r, 1)], sem)

    def issue(r, carry):
        for k in range(TOP_K):
            row_copy(r, k, dest_ref[r * TOP_K + k]).start()
        return carry

    def drain(r, carry):
        for k in range(TOP_K):
            row_copy(r, k, dest_ref[r * TOP_K + k]).wait()
        return carry

    lax.fori_loop(0, tc, issue, 0)
    lax.fori_loop(0, tc, drain, 0)
    gates = gate_ref[...]
    y = h_ref[...]
    for k in range(TOP_K):
        y = y + gates[:, k:k + 1] * buf_ref[k]
    o_ref[...] = y


def _combine(h, gates, dest_flat, ys, tok_off, n_out):
    tc = TC_COMB
    off = tok_off // tc
    return pl.pallas_call(
        _combine_kernel,
        grid=(n_out // tc,),
        in_specs=[
            pl.BlockSpec((tc * TOP_K,), lambda i: (i + off,), memory_space=pltpu.SMEM),
            pl.BlockSpec((tc, D_MODEL), lambda i: (i + off, 0)),
            pl.BlockSpec((tc, TOP_K), lambda i: (i + off, 0)),
            pl.BlockSpec(memory_space=pl.ANY),
        ],
        out_specs=pl.BlockSpec((tc, D_MODEL), lambda i: (i, 0)),
        out_shape=jax.ShapeDtypeStruct((n_out, D_MODEL), F32),
        scratch_shapes=[pltpu.VMEM((TOP_K, tc, D_MODEL), F32), pltpu.SemaphoreType.DMA(())],
        compiler_params=pltpu.CompilerParams(dimension_semantics=("arbitrary",), vmem_limit_bytes=VMEM_LIMIT),
        name="combine",
    )(dest_flat, h, gates, ys)


def kernel(x_prompt, x_sample, norm1_g, w_in, q_norm_g, k_norm_g, lambda_q1, lambda_k1, lambda_q2, lambda_k2,
           subln_g, rel_bias, w_pool_grp, pool_scale, w_att_out, w_pool_out, w_o, norm2_g, w_router, b_router,
           w_gate, b_gate, w_up, b_up, w_down, b_down):
    bp, sp, _ = x_prompt.shape
    bs, ss, _ = x_sample.shape
    n_p, n_s = bp * sp, bs * ss
    n_tok = n_p + n_s
    xp = x_prompt.reshape(n_p, D_MODEL)
    xs = x_sample.reshape(n_s, D_MODEL)

    proj = _in_projection(xp, xs, norm1_g[0], w_in[0].astype(BF16), q_norm_g[0], k_norm_g[0])
    band = _bias_band(rel_bias)
    lam_vecs = (lambda_q1[0], lambda_k1[0], lambda_q2[0], lambda_k2[0])
    att_p = _attention(proj, band, lam_vecs, subln_g[0], bp, sp, 0)
    att_s = _attention(proj, band, lam_vecs, subln_g[0], bs, ss, n_p)
    h, hn, logits = _merge_project(att_p, att_s, proj, xp, xs, sp, ss, w_pool_grp[0], pool_scale[0],
                                   w_att_out[0], w_pool_out[0], w_o[0], norm2_g[0], w_router[0], b_router[0])

    idx, gates, rank, counts = _route(logits)

    n_blocks = (n_tok * TOP_K + N_EXPERTS * (TM_MOE - 1)) // TM_MOE
    counts = counts[0].astype(I32)
    padded = (counts + TM_MOE - 1) // TM_MOE * TM_MOE
    pad_end = jnp.cumsum(padded)
    start_pad = pad_end - padded
    dest = (start_pad[idx] + rank).reshape(-1)
    blk_start = jnp.arange(n_blocks, dtype=I32) * TM_MOE
    blk_expert = jnp.minimum(jnp.searchsorted(pad_end, blk_start, side="right"), N_EXPERTS - 1).astype(I32)
    blk_valid = jnp.clip(counts[blk_expert] - (blk_start - start_pad[blk_expert]), 0, TM_MOE).astype(I32)
    n_used = (pad_end[-1:] // TM_MOE).astype(I32)

    xs_sorted = _dispatch(hn, dest, n_blocks * TM_MOE)
    ys = _expert_ffn(xs_sorted, blk_expert, n_used, blk_valid,
                     w_gate[0], b_gate[0], w_up[0], b_up[0], w_down[0], b_down[0])
    y_p = _combine(h, gates, dest, ys, 0, n_p)
    y_s = _combine(h, gates, dest, ys, n_p, n_s)
    return (y_p.reshape(bp, sp, D_MODEL), y_s.reshape(bs, ss, D_MODEL))
```

```python
import functools
import math

import jax
import jax.numpy as jnp
from jax import lax
from jax.experimental import pallas as pl
from jax.experimental.pallas import tpu as pltpu

F32 = jnp.float32
BF16 = jnp.bfloat16
I32 = jnp.int32

D_MODEL = 2048
N_HEADS = 8
HEAD_DIM = 64
V_DIM = 2 * HEAD_DIM
ATT_W = N_HEADS * V_DIM
POOL_WINDOWS = (2, 4, 8, 16)
POOL_W = D_MODEL // 2
POOL_GROUP_W = POOL_W // len(POOL_WINDOWS)
IN_COLS = 3 * ATT_W + POOL_W + 2 * D_MODEL
N_BUCKETS = 32
MAX_DISTANCE = 128
N_EXPERTS = 32
TOP_K = 4
D_FF = D_MODEL
SWIGLU_LIMIT = 7.0
SWIGLU_ALPHA = 1.702
NORM_EPS = 1e-6
LAM_INIT = 0.8 - 0.6 * math.exp(-0.3 * 0)
LOG2E = math.log2(math.e)

LANES = 128
VMEM_LIMIT = 56 * 1024 * 1024

TM_IN = 512
TN_IN = 1024
T_ATT = 256
G_ATT = 4
TM_OUT = 256
POOL_HALO = 16
TR_ROUTE = 512
TM_MOE = 512
TF_MOE = 256
GT_MOE = 4
TD_DISP = 256
TC_COMB = 128


def _t5_thresholds():
    half = N_BUCKETS // 2
    max_exact = half // 2
    steps = half - max_exact
    ratio = MAX_DISTANCE // max_exact
    out = []
    for k in range(1, steps):
        n = max_exact
        while n ** steps < (max_exact ** steps) * (ratio ** k):
            n += 1
        out.append(n)
    return tuple(out)


T5_THRESHOLDS = _t5_thresholds()
T5_FAR = T5_THRESHOLDS[-1]


def _inproj_kernel(n_prompt_tiles, xp_ref, xs_ref, g1_ref, w_ref, gsum_ref, qg_ref, kg_ref, o_ref, xn_ref):
    i = pl.program_id(0)
    j = pl.program_id(1)

    @pl.when(j == 0)
    def _():
        x = jnp.where(i < n_prompt_tiles, xp_ref[...], xs_ref[...])
        ms = jnp.mean(x * x, axis=-1, keepdims=True)
        xn_ref[...] = (x * lax.rsqrt(ms + NORM_EPS) * g1_ref[...]).astype(BF16)

    acc = jnp.dot(xn_ref[...], w_ref[...], preferred_element_type=F32)

    def head_norm(g_ref, scale):
        sq = (acc * acc).astype(BF16)
        gw = gsum_ref.shape[0]
        parts = [jnp.dot(sq[:, c * gw:(c + 1) * gw], gsum_ref[...], preferred_element_type=F32)
                 for c in range(TN_IN // gw)]
        ss = jnp.concatenate(parts, axis=1)
        return acc * lax.rsqrt(ss * (1.0 / HEAD_DIM) + NORM_EPS) * (g_ref[...] * scale)

    @pl.when(j == 0)
    def _():
        o_ref[...] = head_norm(qg_ref, HEAD_DIM ** -0.5 * LOG2E).astype(BF16)

    @pl.when(j == 1)
    def _():
        o_ref[...] = head_norm(kg_ref, 1.0).astype(BF16)

    @pl.when((j == 2) | (j == 3))
    def _():
        o_ref[...] = acc.astype(BF16)

    @pl.when(j >= 4)
    def _():
        o_ref[...] = jax.nn.sigmoid(acc).astype(BF16)


def _in_projection(xp, xs, norm1_g, w_in_bf, q_norm_g, k_norm_g):
    n_p, n_s = xp.shape[0], xs.shape[0]
    n_tok = n_p + n_s
    npt, nst = n_p // TM_IN, n_s // TM_IN
    gw = 2 * LANES
    gid = jnp.arange(gw) // HEAD_DIM
    gsum = (gid[:, None] == gid[None, :]).astype(BF16)
    qg = jnp.tile(q_norm_g.astype(F32), TN_IN // HEAD_DIM)[None, :]
    kg = jnp.tile(k_norm_g.astype(F32), TN_IN // HEAD_DIM)[None, :]
    return pl.pallas_call(
        functools.partial(_inproj_kernel, npt),
        grid=(n_tok // TM_IN, IN_COLS // TN_IN),
        in_specs=[
            pl.BlockSpec((TM_IN, D_MODEL), lambda i, j: (jnp.minimum(i, npt - 1), 0)),
            pl.BlockSpec((TM_IN, D_MODEL), lambda i, j: (jnp.maximum(i - npt, 0), 0)),
            pl.BlockSpec((1, D_MODEL), lambda i, j: (0, 0)),
            pl.BlockSpec((D_MODEL, TN_IN), lambda i, j: (0, j)),
            pl.BlockSpec((gw, gw), lambda i, j: (0, 0)),
            pl.BlockSpec((1, TN_IN), lambda i, j: (0, 0)),
            pl.BlockSpec((1, TN_IN), lambda i, j: (0, 0)),
        ],
        out_specs=pl.BlockSpec((TM_IN, TN_IN), lambda i, j: (i, j)),
        out_shape=jax.ShapeDtypeStruct((n_tok, IN_COLS), BF16),
        scratch_shapes=[pltpu.VMEM((TM_IN, D_MODEL), BF16)],
        compiler_params=pltpu.CompilerParams(
            dimension_semantics=("arbitrary", "arbitrary"), vmem_limit_bytes=VMEM_LIMIT),
        name="in_projection",
    )(xp, xs, norm1_g.astype(F32)[None, :], w_in_bf, gsum, qg, kg)


N_BAND = 5


def _band_kernel(rb_ref, o_ref):
    h = pl.program_id(0)
    t = T_ATT
    kj = lax.broadcasted_iota(I32, (t, t), 0)
    qi = lax.broadcasted_iota(I32, (t, t), 1)
    half = N_BUCKETS // 2
    max_exact = half // 2
    for d in range(N_BAND):
        rel = (d - N_BAND // 2) * t + kj - qi
        n = jnp.abs(rel)
        large = jnp.full((t, t), max_exact, I32)
        for th in T5_THRESHOLDS:
            large = large + jnp.where(n >= th, 1, 0)
        bucket = jnp.where(rel > 0, half, 0) + jnp.where(n < max_exact, n, large)
        val = jnp.zeros((t, t), F32)
        for b in range(N_BUCKETS):
            val = jnp.where(bucket == b, rb_ref[b, h], val)
        o_ref[0, d] = val * LOG2E


def _bias_band(rel_bias):
    assert T_ATT >= T5_FAR, "tiles two or more away from the diagonal must lie in the constant-bias region"
    return pl.pallas_call(
        _band_kernel,
        grid=(N_HEADS,),
        in_specs=[pl.BlockSpec(memory_space=pltpu.SMEM)],
        out_specs=pl.BlockSpec((1, N_BAND, T_ATT, T_ATT), lambda h: (h, 0, 0, 0)),
        out_shape=jax.ShapeDtypeStruct((N_HEADS, N_BAND, T_ATT, T_ATT), F32),
        compiler_params=pltpu.CompilerParams(dimension_semantics=("arbitrary",)),
        name="bias_band",
    )(rel_bias.astype(F32))


def _attn_kernel(seq, q_ref, k_ref, v_ref, band_ref, lq1_ref, lk1_ref, lq2_ref, lk2_ref, sg_ref,
                 o_ref, vt_ref, qz_ref, m_ref, l_ref, acc_ref, s0_ref, s1_ref):
    qi = pl.program_id(2)
    t = T_ATT
    nk = seq // t
    heads = range(G_ATT)

    def cols(g):
        return slice(g * V_DIM, (g + 1) * V_DIM)

    @pl.when(qi == 0)
    def _():
        for g in heads:
            for c in range(nk):
                vt_ref[g, c] = v_ref[c * t:(c + 1) * t, cols(g)].astype(F32).T.astype(BF16)

    for g in heads:
        q = q_ref[:, cols(g)]
        lane = lax.broadcasted_iota(I32, q.shape, 1)
        qz_ref[g, 0:t, :] = jnp.where(lane < HEAD_DIM, q, jnp.zeros_like(q))
        qz_ref[g, t:2 * t, :] = jnp.where(lane >= HEAD_DIM, q, jnp.zeros_like(q))
    m_ref[...] = jnp.full(m_ref.shape, -1e30, F32)
    l_ref[...] = jnp.zeros(l_ref.shape, F32)
    acc_ref[...] = jnp.zeros(acc_ref.shape, F32)

    def score_tiles(ki, s_ref):
        for g in heads:
            k_c = k_ref[pl.ds(pl.multiple_of(ki * t, t), t), cols(g)]
            s_ref[g] = lax.dot_general(k_c, qz_ref[g], (((1,), (1,)), ((), ())),
                                       preferred_element_type=F32)

    def softmax_pv(ki, s_ref):
        band = jnp.clip(ki - qi + N_BAND // 2, 0, N_BAND - 1)
        probs, alphas = [], []
        for g in heads:
            bias = band_ref[g, band]
            s = s_ref[g] + jnp.concatenate([bias, bias], axis=1)
            m_old = m_ref[g]
            m_new = jnp.maximum(m_old, jnp.max(s, axis=0, keepdims=True))
            alpha = jnp.exp2(m_old - m_new)
            p = jnp.exp2(s - m_new)
            l_ref[g] = alpha * l_ref[g] + jnp.sum(p, axis=0, keepdims=True)
            m_ref[g] = m_new
            probs.append(p.astype(BF16))
            alphas.append(alpha)
        for g in heads:
            pv = jnp.dot(vt_ref[g, ki], probs[g], preferred_element_type=F32)
            acc_ref[g] = acc_ref[g] * alphas[g] + pv

    score_tiles(0, s0_ref)

    def pair(j, carry):
        k0 = 2 * j
        score_tiles(k0 + 1, s1_ref)
        softmax_pv(k0, s0_ref)
        score_tiles(jnp.minimum(k0 + 2, nk - 1), s0_ref)
        softmax_pv(k0 + 1, s1_ref)
        return carry

    lax.fori_loop(0, nk // 2, pair, 0)

    lam = (jnp.exp(jnp.sum(lq1_ref[...] * lk1_ref[...], axis=1, keepdims=True))
           - jnp.exp(jnp.sum(lq2_ref[...] * lk2_ref[...], axis=1, keepdims=True)) + LAM_INIT)
    for g in heads:
        acc = acc_ref[g]
        l = l_ref[g]
        o = acc[:, 0:t] / l[:, 0:t] - lam * (acc[:, t:2 * t] / l[:, t:2 * t])
        ms = jnp.mean(o * o, axis=0, keepdims=True)
        on = o * lax.rsqrt(ms + NORM_EPS) * sg_ref[...] * (1.0 - LAM_INIT)
        o_ref[:, cols(g)] = on.T.astype(BF16)


def _attention(proj, band, lam_vecs, subln_g, batch, seq, row_off):
    t = T_ATT
    nq = seq // t
    assert nq % 2 == 0, "key tiles are processed in pairs"
    gw = G_ATT * V_DIM
    sec = ATT_W // gw
    lq1, lk1, lq2, lk2 = [v.astype(F32)[None, :] for v in lam_vecs]
    vec_spec = pl.BlockSpec((1, HEAD_DIM), lambda b, h, qi: (0, 0))
    return pl.pallas_call(
        functools.partial(_attn_kernel, seq),
        grid=(batch, N_HEADS // G_ATT, nq),
        in_specs=[
            pl.BlockSpec((t, gw), lambda b, h, qi: (row_off // t + b * nq + qi, h)),
            pl.BlockSpec((seq, gw), lambda b, h, qi: (row_off // seq + b, sec + h)),
            pl.BlockSpec((seq, gw), lambda b, h, qi: (row_off // seq + b, 2 * sec + h)),
            pl.BlockSpec((G_ATT, N_BAND, t, t), lambda b, h, qi: (h, 0, 0, 0)),
            vec_spec, vec_spec, vec_spec, vec_spec,
            pl.BlockSpec((V_DIM, 1), lambda b, h, qi: (0, 0)),
        ],
        out_specs=pl.BlockSpec((t, gw), lambda b, h, qi: (b * nq + qi, h)),
        out_shape=jax.ShapeDtypeStruct((batch * seq, ATT_W), BF16),
        scratch_shapes=[
            pltpu.VMEM((G_ATT, seq // t, V_DIM, t), BF16),
            pltpu.VMEM((G_ATT, 2 * t, V_DIM), BF16),
            pltpu.VMEM((G_ATT, 1, 2 * t), F32),
            pltpu.VMEM((G_ATT, 1, 2 * t), F32),
            pltpu.VMEM((G_ATT, V_DIM, 2 * t), F32),
            pltpu.VMEM((G_ATT, t, 2 * t), F32),
            pltpu.VMEM((G_ATT, t, 2 * t), F32),
        ],
        compiler_params=pltpu.CompilerParams(
            dimension_semantics=("arbitrary", "arbitrary", "arbitrary"), vmem_limit_bytes=VMEM_LIMIT),
        name=f"diff_attention_s{seq}",
    )(proj, proj, proj, band, lq1, lk1, lq2, lk2, subln_g.astype(F32)[:, None])


def _pack_bf16_pairs(x):
    n = x.shape[1] // 2
    lo = lax.bitcast_convert_type(x[:, :n].astype(F32), jnp.uint32)
    hi = lax.bitcast_convert_type(x[:, n:].astype(F32), jnp.uint32)
    return lax.shift_right_logical(lo, jnp.uint32(16)) | (hi & jnp.uint32(0xFFFF0000))


def _unpack_bf16_pairs(w):
    lo = lax.bitcast_convert_type(lax.shift_left(w, jnp.uint32(16)), F32)
    hi = lax.bitcast_convert_type(w & jnp.uint32(0xFFFF0000), F32)
    return jnp.concatenate([lo.astype(BF16), hi.astype(BF16)], axis=1)


def _merge_kernel(n_prompt_tiles, seq_p, seq_s,
                  attp_ref, atts_ref, u_ref, ul_ref, ur_ref, ga_ref, gb_ref, xp_ref, xs_ref,
                  wpg_ref, psc_ref, wao_ref, wpo_ref, wo_ref, g2_ref, wr_ref, br_ref,
                  h_ref, hn_ref, lg_ref):
    i = pl.program_id(0)
    tm = TM_OUT
    is_p = i < n_prompt_tiles
    seq = jnp.where(is_p, seq_p, seq_s)
    t0 = jnp.where(is_p, i, i - n_prompt_tiles) * tm
    pos0 = t0 % seq
    first = pos0 == 0
    last = pos0 + tm == seq

    u = u_ref[...]
    zero_halo = jnp.zeros(ul_ref.shape, BF16)
    ul = jnp.where(first, zero_halo, ul_ref[...])
    ur = jnp.where(last, zero_halo, ur_ref[...])
    uext = jnp.concatenate([ul, u, ur], axis=0)
    r = lax.broadcasted_iota(I32, (tm, tm + 2 * POOL_HALO), 0)
    c = lax.broadcasted_iota(I32, (tm, tm + 2 * POOL_HALO), 1) - POOL_HALO
    pos = pos0 + lax.broadcasted_iota(I32, (tm, 1), 0)
    mixed = []
    for gi, w in enumerate(POOL_WINDOWS):
        sl = slice(gi * POOL_GROUP_W, (gi + 1) * POOL_GROUP_W)
        band = jnp.where(c >= r - w // 2, jnp.where(c < r + (w - w // 2), 1.0, 0.0), 0.0).astype(BF16)
        wsum = jnp.dot(band, uext[:, sl], preferred_element_type=F32)
        cnt = (jnp.minimum(pos + (w - w // 2), seq) - jnp.maximum(pos - w // 2, 0)).astype(F32)
        pooled = wsum / cnt - u[:, sl].astype(F32)
        mg = jnp.dot(pooled.astype(BF16), wpg_ref[gi], preferred_element_type=F32)
        mixed.append((mg * psc_ref[:, sl]).astype(BF16))
    mixed = jnp.concatenate(mixed, axis=1)

    att = jnp.where(is_p, attp_ref[...], atts_ref[...])
    y_a = jnp.dot(att, wao_ref[...], preferred_element_type=F32)
    y_b = jnp.dot(mixed, wpo_ref[...], preferred_element_type=F32)
    merged = ga_ref[...].astype(F32) * y_a + gb_ref[...].astype(F32) * y_b
    x = jnp.where(is_p, xp_ref[...], xs_ref[...])
    h = x + jnp.dot(merged.astype(BF16), wo_ref[...], preferred_element_type=F32)
    h_ref[...] = h
    ms = jnp.mean(h * h, axis=-1, keepdims=True)
    hn = (h * lax.rsqrt(ms + NORM_EPS) * g2_ref[...]).astype(BF16)
    hn_ref[...] = _pack_bf16_pairs(hn)
    lg_ref[...] = jnp.dot(hn, wr_ref[...], preferred_element_type=F32) + br_ref[...]


def _merge_project(att_p, att_s, proj, xp, xs, seq_p, seq_s, w_pool_grp, pool_scale, w_att_out, w_pool_out,
                   w_o, norm2_g, w_router, b_router):
    tm = TM_OUT
    n_p, n_s = xp.shape[0], xs.shape[0]
    n_tok = n_p + n_s
    npt = n_p // tm
    hb = tm // POOL_HALO
    n_hblk = n_tok // POOL_HALO
    u_col = 3 * ATT_W // POOL_W
    const = dict(pipeline_mode=pl.Buffered(1))
    return pl.pallas_call(
        functools.partial(_merge_kernel, npt, seq_p, seq_s),
        grid=(n_tok // tm,),
        in_specs=[
            pl.BlockSpec((tm, ATT_W), lambda i: (jnp.minimum(i, npt - 1), 0)),
            pl.BlockSpec((tm, ATT_W), lambda i: (jnp.maximum(i - npt, 0), 0)),
            pl.BlockSpec((tm, POOL_W), lambda i: (i, u_col)),
            pl.BlockSpec((POOL_HALO, POOL_W), lambda i: (jnp.maximum(i * hb - 1, 0), u_col)),
            pl.BlockSpec((POOL_HALO, POOL_W), lambda i: (jnp.minimum((i + 1) * hb, n_hblk - 1), u_col)),
            pl.BlockSpec((tm, D_MODEL), lambda i: (i, 2)),
            pl.BlockSpec((tm, D_MODEL), lambda i: (i, 3)),
            pl.BlockSpec((tm, D_MODEL), lambda i: (jnp.minimum(i, npt - 1), 0)),
            pl.BlockSpec((tm, D_MODEL), lambda i: (jnp.maximum(i - npt, 0), 0)),
            pl.BlockSpec((len(POOL_WINDOWS), POOL_GROUP_W, POOL_GROUP_W), lambda i: (0, 0, 0), **const),
            pl.BlockSpec((1, POOL_W), lambda i: (0, 0), **const),
            pl.BlockSpec((ATT_W, D_MODEL), lambda i: (0, 0), **const),
            pl.BlockSpec((POOL_W, D_MODEL), lambda i: (0, 0), **const),
            pl.BlockSpec((D_MODEL, D_MODEL), lambda i: (0, 0), **const),
            pl.BlockSpec((1, D_MODEL), lambda i: (0, 0), **const),
            pl.BlockSpec((D_MODEL, N_EXPERTS), lambda i: (0, 0), **const),
            pl.BlockSpec((1, N_EXPERTS), lambda i: (0, 0), **const),
        ],
        out_specs=[
            pl.BlockSpec((tm, D_MODEL), lambda i: (i, 0)),
            pl.BlockSpec((tm, D_MODEL // 2), lambda i: (i, 0)),
            pl.BlockSpec((tm, N_EXPERTS), lambda i: (i, 0)),
        ],
        out_shape=[
            jax.ShapeDtypeStruct((n_tok, D_MODEL), F32),
            jax.ShapeDtypeStruct((n_tok, D_MODEL // 2), jnp.uint32),
            jax.ShapeDtypeStruct((n_tok, N_EXPERTS), F32),
        ],
        compiler_params=pltpu.CompilerParams(dimension_semantics=("arbitrary",), vmem_limit_bytes=VMEM_LIMIT),
        name="merge_project",
    )(att_p, att_s, proj, proj, proj, proj, proj, xp, xs,
      w_pool_grp.astype(BF16), pool_scale.astype(F32)[None, :], w_att_out.astype(BF16),
      w_pool_out.astype(BF16), w_o.astype(BF16), norm2_g.astype(F32)[None, :],
      w_router.astype(BF16), b_router.astype(F32)[None, :])


def _route_kernel(lg_ref, idx_ref, gate_ref, rank_ref, cnt_ref, run_ref):
    i = pl.program_id(0)
    tr = TR_ROUTE

    @pl.when(i == 0)
    def _():
        run_ref[...] = jnp.zeros(run_ref.shape, F32)

    cur = lg_ref[...]
    e_iota = lax.broadcasted_iota(I32, cur.shape, 1).astype(F32)
    member = jnp.zeros(cur.shape, F32)
    vals, idxs = [], []
    for _ in range(TOP_K):
        mx = jnp.max(cur, axis=1, keepdims=True)
        am = jnp.min(jnp.where(cur == mx, e_iota, float(N_EXPERTS)), axis=1, keepdims=True)
        hit = e_iota == am
        vals.append(mx)
        idxs.append(am)
        member = member + jnp.where(hit, 1.0, 0.0)
        cur = jnp.where(hit, -jnp.inf, cur)
    exps = [jnp.exp(v - vals[0]) for v in vals]
    denom = exps[0]
    for e in exps[1:]:
        denom = denom + e

    rr = lax.broadcasted_iota(I32, (tr, tr), 0)
    cc = lax.broadcasted_iota(I32, (tr, tr), 1)
    tri = jnp.where(cc < rr, 1.0, 0.0).astype(BF16)
    before = jnp.dot(tri, member.astype(BF16), preferred_element_type=F32) + run_ref[...]
    run_ref[...] = run_ref[...] + jnp.sum(member, axis=0, keepdims=True)
    cnt_ref[...] = run_ref[...]

    lane = lax.broadcasted_iota(I32, (tr, TOP_K), 1)
    idx_out = jnp.zeros((tr, TOP_K), F32)
    gate_out = jnp.zeros((tr, TOP_K), F32)
    rank_out = jnp.zeros((tr, TOP_K), F32)
    for k in range(TOP_K):
        rk = jnp.sum(jnp.where(e_iota == idxs[k], before, 0.0), axis=1, keepdims=True)
        idx_out = jnp.where(lane == k, idxs[k], idx_out)
        gate_out = jnp.where(lane == k, exps[k] / denom, gate_out)
        rank_out = jnp.where(lane == k, rk, rank_out)
    idx_ref[...] = idx_out.astype(I32)
    gate_ref[...] = gate_out
    rank_ref[...] = rank_out.astype(I32)


def _route(logits):
    n_tok = logits.shape[0]
    tr = TR_ROUTE
    return pl.pallas_call(
        _route_kernel,
        grid=(n_tok // tr,),
        in_specs=[pl.BlockSpec((tr, N_EXPERTS), lambda i: (i, 0))],
        out_specs=[
            pl.BlockSpec((tr, TOP_K), lambda i: (i, 0)),
            pl.BlockSpec((tr, TOP_K), lambda i: (i, 0)),
            pl.BlockSpec((tr, TOP_K), lambda i: (i, 0)),
            pl.BlockSpec((1, N_EXPERTS), lambda i: (0, 0)),
        ],
        out_shape=[
            jax.ShapeDtypeStruct((n_tok, TOP_K), I32),
            jax.ShapeDtypeStruct((n_tok, TOP_K), F32),
            jax.ShapeDtypeStruct((n_tok, TOP_K), I32),
            jax.ShapeDtypeStruct((1, N_EXPERTS), F32),
        ],
        scratch_shapes=[pltpu.VMEM((1, N_EXPERTS), F32)],
        compiler_params=pltpu.CompilerParams(dimension_semantics=("arbitrary",)),
        name="route",
    )(logits)


def _dispatch_kernel(dest_ref, hn_ref, xs_ref, sem):
    td = TD_DISP

    def row_copy(r, d):
        return pltpu.make_async_copy(hn_ref.at[pl.ds(r, 1)], xs_ref.at[pl.ds(d, 1)], sem)

    def issue(r, carry):
        for k in range(TOP_K):
            row_copy(r, dest_ref[r * TOP_K + k]).start()
        return carry

    def drain(r, carry):
        for k in range(TOP_K):
            row_copy(r, dest_ref[r * TOP_K + k]).wait()
        return carry

    lax.fori_loop(0, td, issue, 0)
    lax.fori_loop(0, td, drain, 0)


def _dispatch(hn, dest_flat, n_rows):
    n_tok = hn.shape[0]
    td = TD_DISP
    return pl.pallas_call(
        _dispatch_kernel,
        grid=(n_tok // td,),
        in_specs=[
            pl.BlockSpec((td * TOP_K,), lambda i: (i,), memory_space=pltpu.SMEM),
            pl.BlockSpec((td, hn.shape[1]), lambda i: (i, 0)),
        ],
        out_specs=pl.BlockSpec(memory_space=pl.ANY),
        out_shape=jax.ShapeDtypeStruct((n_rows, hn.shape[1]), hn.dtype),
        scratch_shapes=[pltpu.SemaphoreType.DMA(())],
        compiler_params=pltpu.CompilerParams(dimension_semantics=("arbitrary",)),
        name="dispatch",
    )(dest_flat, hn)


def _ffn_kernel(ge_ref, gs_ref, gn_ref, ng_ref, bv_ref,
                xs_hbm, wg_ref, bg_ref, wu_ref, bu_ref, wd_ref, bd_ref,
                ys_hbm, slab_ref, acc_ref, sem_x, sem_o):
    grp = pl.program_id(0)
    f = pl.program_id(1)
    nf = pl.num_programs(1)
    tm = TM_MOE

    @pl.when(grp < ng_ref[0])
    def _():
        start = gs_ref[grp]
        nt = gn_ref[grp]

        @pl.when(f == 0)
        def _():
            row0 = pl.multiple_of(start * tm, tm)
            load = pltpu.make_async_copy(xs_hbm.at[pl.ds(row0, GT_MOE * tm)], slab_ref, sem_x)
            load.start()
            acc_ref[...] = jnp.broadcast_to(bd_ref[0], acc_ref.shape)
            load.wait()

        def row_tiles(tiles):
            wg = wg_ref[0].astype(BF16)
            wu = wu_ref[0].astype(BF16)
            wd = wd_ref[0].astype(BF16)
            row = lax.broadcasted_iota(I32, (tm, 1), 0)
            gates, ups = [], []
            for r in tiles:
                words = jnp.where(row < bv_ref[start + r], slab_ref[r * tm:(r + 1) * tm, :], jnp.uint32(0))
                xb = _unpack_bf16_pairs(words)
                gates.append(jnp.dot(xb, wg, preferred_element_type=F32) + bg_ref[0])
                ups.append(jnp.dot(xb, wu, preferred_element_type=F32) + bu_ref[0])
            hidden = []
            for g, up in zip(gates, ups):
                g = jnp.minimum(g, SWIGLU_LIMIT)
                up = jnp.clip(up, -SWIGLU_LIMIT, SWIGLU_LIMIT)
                hidden.append(((up + 1.0) * (g * jax.nn.sigmoid(SWIGLU_ALPHA * g))).astype(BF16))
            for r, hmid in zip(tiles, hidden):
                acc_ref[r * tm:(r + 1) * tm, :] += jnp.dot(hmid, wd, preferred_element_type=F32)

        assert GT_MOE == 4
        pl.when(nt >= 2)(lambda: row_tiles((0, 1)))
        pl.when(nt == 1)(lambda: row_tiles((0,)))
        pl.when(nt == 4)(lambda: row_tiles((2, 3)))
        pl.when(nt == 3)(lambda: row_tiles((2,)))

        @pl.when(f == nf - 1)
        def _():
            def store(r):
                row0 = pl.multiple_of((start + r) * tm, tm)
                return pltpu.make_async_copy(acc_ref.at[pl.ds(r * tm, tm)], ys_hbm.at[pl.ds(row0, tm)], sem_o)

            for r in range(GT_MOE):
                pl.when(r < nt)(lambda r=r: store(r).start())
            for r in range(GT_MOE):
                pl.when(r < nt)(lambda r=r: store(r).wait())


def _expert_ffn(xs, grp_expert, grp_start, grp_ntiles, n_groups, blk_valid,
                w_gate, b_gate, w_up, b_up, w_down, b_down):
    n_rows = xs.shape[0]
    max_groups = grp_expert.shape[0]
    nf = D_FF // TF_MOE

    def expert(g, ge, ng):
        return ge[jnp.minimum(g, ng[0] - 1)]

    def ftile(g, f, ng):
        return jnp.where(g < ng[0], f, nf - 1)

    grid_spec = pltpu.PrefetchScalarGridSpec(
        num_scalar_prefetch=5,
        grid=(max_groups, nf),
        in_specs=[
            pl.BlockSpec(memory_space=pl.ANY),
            pl.BlockSpec((1, D_MODEL, TF_MOE), lambda g, f, ge, gs, gn, ng, bv: (expert(g, ge, ng), 0, ftile(g, f, ng))),
            pl.BlockSpec((1, 1, TF_MOE), lambda g, f, ge, gs, gn, ng, bv: (expert(g, ge, ng), 0, ftile(g, f, ng))),
            pl.BlockSpec((1, D_MODEL, TF_MOE), lambda g, f, ge, gs, gn, ng, bv: (expert(g, ge, ng), 0, ftile(g, f, ng))),
            pl.BlockSpec((1, 1, TF_MOE), lambda g, f, ge, gs, gn, ng, bv: (expert(g, ge, ng), 0, ftile(g, f, ng))),
            pl.BlockSpec((1, TF_MOE, D_MODEL), lambda g, f, ge, gs, gn, ng, bv: (expert(g, ge, ng), ftile(g, f, ng), 0)),
            pl.BlockSpec((1, 1, D_MODEL), lambda g, f, ge, gs, gn, ng, bv: (expert(g, ge, ng), 0, 0)),
        ],
        out_specs=pl.BlockSpec(memory_space=pl.ANY),
        scratch_shapes=[
            pltpu.VMEM((GT_MOE * TM_MOE, D_MODEL // 2), jnp.uint32),
            pltpu.VMEM((GT_MOE * TM_MOE, D_MODEL), F32),
            pltpu.SemaphoreType.DMA(()),
            pltpu.SemaphoreType.DMA(()),
        ],
    )
    return pl.pallas_call(
        _ffn_kernel,
        grid_spec=grid_spec,
        out_shape=jax.ShapeDtypeStruct((n_rows, D_MODEL), F32),
        compiler_params=pltpu.CompilerParams(
            dimension_semantics=("arbitrary", "arbitrary"), vmem_limit_bytes=VMEM_LIMIT),
        name="expert_ffn",
    )(grp_expert, grp_start, grp_ntiles, n_groups, blk_valid, xs,
      w_gate, b_gate[:, None, :], w_up, b_up[:, None, :], w_down, b_down[:, None, :])


def _combine_kernel(dest_ref, h_ref, gate_ref, ys_ref, o_ref, buf_ref, sem):
    tc = TC_COMB

    def row_copy(r, k, d):
        return pltpu.make_async_copy(ys_ref.at[pl.ds(d, 1)], buf_ref.at[k, pl.ds(r, 1)], sem)

    def issue(r, carry):
        for k in range(TOP_K):
            row_copy(r, k, dest_ref[r * TOP_K + k]).start()
        return carry

    def drain(r, carry):
        for k in range(TOP_K):
            row_copy(r, k, dest_ref[r * TOP_K + k]).wait()
        return carry

    lax.fori_loop(0, tc, issue, 0)
    lax.fori_loop(0, tc, drain, 0)
    gates = gate_ref[...]
    y = h_ref[...]
    for k in range(TOP_K):
        y = y + gates[:, k:k + 1] * buf_ref[k]
    o_ref[...] = y


def _combine(h, gates, dest_flat, ys, tok_off, n_out):
    tc = TC_COMB
    off = tok_off // tc
    return pl.pallas_call(
        _combine_kernel,
        grid=(n_out // tc,),
        in_specs=[
            pl.BlockSpec((tc * TOP_K,), lambda i: (i + off,), memory_space=pltpu.SMEM),
            pl.BlockSpec((tc, D_MODEL), lambda i: (i + off, 0)),
            pl.BlockSpec((tc, TOP_K), lambda i: (i + off, 0)),
            pl.BlockSpec(memory_space=pl.ANY),
        ],
        out_specs=pl.BlockSpec((tc, D_MODEL), lambda i: (i, 0)),
        out_shape=jax.ShapeDtypeStruct((n_out, D_MODEL), F32),
        scratch_shapes=[pltpu.VMEM((TOP_K, tc, D_MODEL), F32), pltpu.SemaphoreType.DMA(())],
        compiler_params=pltpu.CompilerParams(dimension_semantics=("arbitrary",), vmem_limit_bytes=VMEM_LIMIT),
        name="combine",
    )(dest_flat, h, gates, ys)


def kernel(x_prompt, x_sample, norm1_g, w_in, q_norm_g, k_norm_g, lambda_q1, lambda_k1, lambda_q2, lambda_k2,
           subln_g, rel_bias, w_pool_grp, pool_scale, w_att_out, w_pool_out, w_o, norm2_g, w_router, b_router,
           w_gate, b_gate, w_up, b_up, w_down, b_down):
    bp, sp, _ = x_prompt.shape
    bs, ss, _ = x_sample.shape
    n_p, n_s = bp * sp, bs * ss
    n_tok = n_p + n_s
    xp = x_prompt.reshape(n_p, D_MODEL)
    xs = x_sample.reshape(n_s, D_MODEL)

    proj = _in_projection(xp, xs, norm1_g[0], w_in[0].astype(BF16), q_norm_g[0], k_norm_g[0])
    band = _bias_band(rel_bias)
    lam_vecs = (lambda_q1[0], lambda_k1[0], lambda_q2[0], lambda_k2[0])
    att_p = _attention(proj, band, lam_vecs, subln_g[0], bp, sp, 0)
    att_s = _attention(proj, band, lam_vecs, subln_g[0], bs, ss, n_p)
    h, hn, logits = _merge_project(att_p, att_s, proj, xp, xs, sp, ss, w_pool_grp[0], pool_scale[0],
                                   w_att_out[0], w_pool_out[0], w_o[0], norm2_g[0], w_router[0], b_router[0])

    idx, gates, rank, counts = _route(logits)

    n_blocks = (n_tok * TOP_K + N_EXPERTS * (TM_MOE - 1)) // TM_MOE
    counts = counts[0].astype(I32)
    padded = (counts + TM_MOE - 1) // TM_MOE * TM_MOE
    pad_end = jnp.cumsum(padded)
    start_pad = pad_end - padded
    dest = (start_pad[idx] + rank).reshape(-1)
    blk_start = jnp.arange(n_blocks, dtype=I32) * TM_MOE
    blk_expert = jnp.minimum(jnp.searchsorted(pad_end, blk_start, side="right"), N_EXPERTS - 1).astype(I32)
    blk_valid = jnp.clip(counts[blk_expert] - (blk_start - start_pad[blk_expert]), 0, TM_MOE).astype(I32)
    e_blocks = padded // TM_MOE
    e_groups = (e_blocks + GT_MOE - 1) // GT_MOE
    grp_end = jnp.cumsum(e_groups)
    max_groups = n_blocks // GT_MOE + N_EXPERTS
    gidx = jnp.arange(max_groups, dtype=I32)
    grp_expert = jnp.minimum(jnp.searchsorted(grp_end, gidx, side="right"), N_EXPERTS - 1).astype(I32)
    in_expert = gidx - (grp_end - e_groups)[grp_expert]
    grp_start = (start_pad[grp_expert] // TM_MOE + GT_MOE * in_expert).astype(I32)
    grp_ntiles = jnp.clip(e_blocks[grp_expert] - GT_MOE * in_expert, 0, GT_MOE).astype(I32)
    n_groups = grp_end[-1:].astype(I32)

    n_rows = (n_blocks + GT_MOE - 1) * TM_MOE
    xs_sorted = _dispatch(hn, dest, n_rows)
    ys = _expert_ffn(xs_sorted, grp_expert, grp_start, grp_ntiles, n_groups, blk_valid,
                     w_gate[0], b_gate[0], w_up[0], b_up[0], w_down[0], b_down[0])
    y_p = _combine(h, gates, dest, ys, 0, n_p)
    y_s = _combine(h, gates, dest, ys, n_p, n_s)
    return (y_p.reshape(bp, sp, D_MODEL), y_s.reshape(bs, ss, D_MODEL))
```

```python
import functools
import math

import jax
import jax.numpy as jnp
from jax import lax
from jax.experimental import pallas as pl
from jax.experimental.pallas import tpu as pltpu

F32 = jnp.float32
BF16 = jnp.bfloat16
I32 = jnp.int32

D_MODEL = 2048
N_HEADS = 8
HEAD_DIM = 64
V_DIM = 2 * HEAD_DIM
ATT_W = N_HEADS * V_DIM
POOL_WINDOWS = (2, 4, 8, 16)
POOL_W = D_MODEL // 2
POOL_GROUP_W = POOL_W // len(POOL_WINDOWS)
IN_COLS = 3 * ATT_W + POOL_W + 2 * D_MODEL
N_BUCKETS = 32
MAX_DISTANCE = 128
N_EXPERTS = 32
TOP_K = 4
D_FF = D_MODEL
SWIGLU_LIMIT = 7.0
SWIGLU_ALPHA = 1.702
NORM_EPS = 1e-6
LAM_INIT = 0.8 - 0.6 * math.exp(-0.3 * 0)
LOG2E = math.log2(math.e)

LANES = 128
VMEM_LIMIT = 56 * 1024 * 1024

TM_IN = 512
TN_IN = 1024
T_ATT = 256
G_ATT = 4
TM_OUT = 256
POOL_HALO = 16
TR_ROUTE = 512
TM_MOE = 512
TF_MOE = 256
GT_MOE = 4
TD_DISP = 256
TC_COMB = 256
ROW_DMA_UNROLL = 8


def _t5_thresholds():
    half = N_BUCKETS // 2
    max_exact = half // 2
    steps = half - max_exact
    ratio = MAX_DISTANCE // max_exact
    out = []
    for k in range(1, steps):
        n = max_exact
        while n ** steps < (max_exact ** steps) * (ratio ** k):
            n += 1
        out.append(n)
    return tuple(out)


T5_THRESHOLDS = _t5_thresholds()
T5_FAR = T5_THRESHOLDS[-1]


def _inproj_kernel(n_prompt_tiles, xp_ref, xs_ref, g1_ref, w_ref, gsum_ref, qg_ref, kg_ref, o_ref, xn_ref):
    i = pl.program_id(0)
    j = pl.program_id(1)

    @pl.when(j == 0)
    def _():
        x = jnp.where(i < n_prompt_tiles, xp_ref[...], xs_ref[...])
        ms = jnp.mean(x * x, axis=-1, keepdims=True)
        xn_ref[...] = (x * lax.rsqrt(ms + NORM_EPS) * g1_ref[...]).astype(BF16)

    acc = jnp.dot(xn_ref[...], w_ref[...], preferred_element_type=F32)

    def head_norm(g_ref, scale):
        sq = (acc * acc).astype(BF16)
        gw = gsum_ref.shape[0]
        parts = [jnp.dot(sq[:, c * gw:(c + 1) * gw], gsum_ref[...], preferred_element_type=F32)
                 for c in range(TN_IN // gw)]
        ss = jnp.concatenate(parts, axis=1)
        return acc * lax.rsqrt(ss * (1.0 / HEAD_DIM) + NORM_EPS) * (g_ref[...] * scale)

    @pl.when(j == 0)
    def _():
        o_ref[...] = head_norm(qg_ref, HEAD_DIM ** -0.5 * LOG2E).astype(BF16)

    @pl.when(j == 1)
    def _():
        o_ref[...] = head_norm(kg_ref, 1.0).astype(BF16)

    @pl.when((j == 2) | (j == 3))
    def _():
        o_ref[...] = acc.astype(BF16)

    @pl.when(j >= 4)
    def _():
        o_ref[...] = jax.nn.sigmoid(acc).astype(BF16)


def _in_projection(xp, xs, norm1_g, w_in_bf, q_norm_g, k_norm_g):
    n_p, n_s = xp.shape[0], xs.shape[0]
    n_tok = n_p + n_s
    npt, nst = n_p // TM_IN, n_s // TM_IN
    gw = 2 * LANES
    gid = jnp.arange(gw) // HEAD_DIM
    gsum = (gid[:, None] == gid[None, :]).astype(BF16)
    qg = jnp.tile(q_norm_g.astype(F32), TN_IN // HEAD_DIM)[None, :]
    kg = jnp.tile(k_norm_g.astype(F32), TN_IN // HEAD_DIM)[None, :]
    return pl.pallas_call(
        functools.partial(_inproj_kernel, npt),
        grid=(n_tok // TM_IN, IN_COLS // TN_IN),
        in_specs=[
            pl.BlockSpec((TM_IN, D_MODEL), lambda i, j: (jnp.minimum(i, npt - 1), 0)),
            pl.BlockSpec((TM_IN, D_MODEL), lambda i, j: (jnp.maximum(i - npt, 0), 0)),
            pl.BlockSpec((1, D_MODEL), lambda i, j: (0, 0)),
            pl.BlockSpec((D_MODEL, TN_IN), lambda i, j: (0, j)),
            pl.BlockSpec((gw, gw), lambda i, j: (0, 0)),
            pl.BlockSpec((1, TN_IN), lambda i, j: (0, 0)),
            pl.BlockSpec((1, TN_IN), lambda i, j: (0, 0)),
        ],
        out_specs=pl.BlockSpec((TM_IN, TN_IN), lambda i, j: (i, j)),
        out_shape=jax.ShapeDtypeStruct((n_tok, IN_COLS), BF16),
        scratch_shapes=[pltpu.VMEM((TM_IN, D_MODEL), BF16)],
        compiler_params=pltpu.CompilerParams(
            dimension_semantics=("arbitrary", "arbitrary"), vmem_limit_bytes=VMEM_LIMIT),
        name="in_projection",
    )(xp, xs, norm1_g.astype(F32)[None, :], w_in_bf, gsum, qg, kg)


N_BAND = 5


def _band_kernel(rb_ref, o_ref):
    h = pl.program_id(0)
    t = T_ATT
    kj = lax.broadcasted_iota(I32, (t, t), 0)
    qi = lax.broadcasted_iota(I32, (t, t), 1)
    half = N_BUCKETS // 2
    max_exact = half // 2
    for d in range(N_BAND):
        rel = (d - N_BAND // 2) * t + kj - qi
        n = jnp.abs(rel)
        large = jnp.full((t, t), max_exact, I32)
        for th in T5_THRESHOLDS:
            large = large + jnp.where(n >= th, 1, 0)
        bucket = jnp.where(rel > 0, half, 0) + jnp.where(n < max_exact, n, large)
        val = jnp.zeros((t, t), F32)
        for b in range(N_BUCKETS):
            val = jnp.where(bucket == b, rb_ref[b, h], val)
        o_ref[0, d] = val * LOG2E


def _bias_band(rel_bias):
    assert T_ATT >= T5_FAR, "tiles two or more away from the diagonal must lie in the constant-bias region"
    return pl.pallas_call(
        _band_kernel,
        grid=(N_HEADS,),
        in_specs=[pl.BlockSpec(memory_space=pltpu.SMEM)],
        out_specs=pl.BlockSpec((1, N_BAND, T_ATT, T_ATT), lambda h: (h, 0, 0, 0)),
        out_shape=jax.ShapeDtypeStruct((N_HEADS, N_BAND, T_ATT, T_ATT), F32),
        compiler_params=pltpu.CompilerParams(dimension_semantics=("arbitrary",)),
        name="bias_band",
    )(rel_bias.astype(F32))


def _attn_kernel(seq, q_ref, k_ref, v_ref, band_ref, lq1_ref, lk1_ref, lq2_ref, lk2_ref, sg_ref,
                 o_ref, vt_ref, qz_ref, m_ref, l_ref, acc_ref, s0_ref, s1_ref):
    qi = pl.program_id(2)
    t = T_ATT
    nk = seq // t
    heads = range(G_ATT)

    def cols(g):
        return slice(g * V_DIM, (g + 1) * V_DIM)

    @pl.when(qi == 0)
    def _():
        for g in heads:
            for c in range(nk):
                vt_ref[g, c] = v_ref[c * t:(c + 1) * t, cols(g)].astype(F32).T.astype(BF16)

    for g in heads:
        q = q_ref[:, cols(g)]
        lane = lax.broadcasted_iota(I32, q.shape, 1)
        qz_ref[g, 0:t, :] = jnp.where(lane < HEAD_DIM, q, jnp.zeros_like(q))
        qz_ref[g, t:2 * t, :] = jnp.where(lane >= HEAD_DIM, q, jnp.zeros_like(q))
    m_ref[...] = jnp.full(m_ref.shape, -1e30, F32)
    l_ref[...] = jnp.zeros(l_ref.shape, F32)
    acc_ref[...] = jnp.zeros(acc_ref.shape, F32)

    def score_tiles(ki, s_ref):
        for g in heads:
            k_c = k_ref[pl.ds(pl.multiple_of(ki * t, t), t), cols(g)]
            s_ref[g] = lax.dot_general(k_c, qz_ref[g], (((1,), (1,)), ((), ())),
                                       preferred_element_type=F32)

    def softmax_pv(ki, s_ref):
        band = jnp.clip(ki - qi + N_BAND // 2, 0, N_BAND - 1)
        probs, alphas = [], []
        for g in heads:
            bias = band_ref[g, band]
            s = s_ref[g] + jnp.concatenate([bias, bias], axis=1)
            m_old = m_ref[g]
            m_new = jnp.maximum(m_old, jnp.max(s, axis=0, keepdims=True))
            alpha = jnp.exp2(m_old - m_new)
            p = jnp.exp2(s - m_new)
            l_ref[g] = alpha * l_ref[g] + jnp.sum(p, axis=0, keepdims=True)
            m_ref[g] = m_new
            probs.append(p.astype(BF16))
            alphas.append(alpha)
        for g in heads:
            pv = jnp.dot(vt_ref[g, ki], probs[g], preferred_element_type=F32)
            acc_ref[g] = acc_ref[g] * alphas[g] + pv

    score_tiles(0, s0_ref)

    def pair(j, carry):
        k0 = 2 * j
        score_tiles(k0 + 1, s1_ref)
        softmax_pv(k0, s0_ref)
        score_tiles(jnp.minimum(k0 + 2, nk - 1), s0_ref)
        softmax_pv(k0 + 1, s1_ref)
        return carry

    lax.fori_loop(0, nk // 2, pair, 0)

    lam = (jnp.exp(jnp.sum(lq1_ref[...] * lk1_ref[...], axis=1, keepdims=True))
           - jnp.exp(jnp.sum(lq2_ref[...] * lk2_ref[...], axis=1, keepdims=True)) + LAM_INIT)
    for g in heads:
        acc = acc_ref[g]
        l = l_ref[g]
        o = acc[:, 0:t] / l[:, 0:t] - lam * (acc[:, t:2 * t] / l[:, t:2 * t])
        ms = jnp.mean(o * o, axis=0, keepdims=True)
        on = o * lax.rsqrt(ms + NORM_EPS) * sg_ref[...] * (1.0 - LAM_INIT)
        o_ref[:, cols(g)] = on.T.astype(BF16)


def _attention(proj, band, lam_vecs, subln_g, batch, seq, row_off):
    t = T_ATT
    nq = seq // t
    assert nq % 2 == 0, "key tiles are processed in pairs"
    gw = G_ATT * V_DIM
    sec = ATT_W // gw
    lq1, lk1, lq2, lk2 = [v.astype(F32)[None, :] for v in lam_vecs]
    vec_spec = pl.BlockSpec((1, HEAD_DIM), lambda b, h, qi: (0, 0))
    return pl.pallas_call(
        functools.partial(_attn_kernel, seq),
        grid=(batch, N_HEADS // G_ATT, nq),
        in_specs=[
            pl.BlockSpec((t, gw), lambda b, h, qi: (row_off // t + b * nq + qi, h)),
            pl.BlockSpec((seq, gw), lambda b, h, qi: (row_off // seq + b, sec + h)),
            pl.BlockSpec((seq, gw), lambda b, h, qi: (row_off // seq + b, 2 * sec + h)),
            pl.BlockSpec((G_ATT, N_BAND, t, t), lambda b, h, qi: (h, 0, 0, 0)),
            vec_spec, vec_spec, vec_spec, vec_spec,
            pl.BlockSpec((V_DIM, 1), lambda b, h, qi: (0, 0)),
        ],
        out_specs=pl.BlockSpec((t, gw), lambda b, h, qi: (b * nq + qi, h)),
        out_shape=jax.ShapeDtypeStruct((batch * seq, ATT_W), BF16),
        scratch_shapes=[
            pltpu.VMEM((G_ATT, seq // t, V_DIM, t), BF16),
            pltpu.VMEM((G_ATT, 2 * t, V_DIM), BF16),
            pltpu.VMEM((G_ATT, 1, 2 * t), F32),
            pltpu.VMEM((G_ATT, 1, 2 * t), F32),
            pltpu.VMEM((G_ATT, V_DIM, 2 * t), F32),
            pltpu.VMEM((G_ATT, t, 2 * t), F32),
            pltpu.VMEM((G_ATT, t, 2 * t), F32),
        ],
        compiler_params=pltpu.CompilerParams(
            dimension_semantics=("arbitrary", "arbitrary", "arbitrary"), vmem_limit_bytes=VMEM_LIMIT),
        name=f"diff_attention_s{seq}",
    )(proj, proj, proj, band, lq1, lk1, lq2, lk2, subln_g.astype(F32)[:, None])


def _pack_bf16_pairs(x):
    n = x.shape[1] // 2
    lo = lax.bitcast_convert_type(x[:, :n].astype(F32), jnp.uint32)
    hi = lax.bitcast_convert_type(x[:, n:].astype(F32), jnp.uint32)
    return lax.shift_right_logical(lo, jnp.uint32(16)) | (hi & jnp.uint32(0xFFFF0000))


def _unpack_bf16_pairs(w):
    lo = lax.bitcast_convert_type(lax.shift_left(w, jnp.uint32(16)), F32)
    hi = lax.bitcast_convert_type(w & jnp.uint32(0xFFFF0000), F32)
    return jnp.concatenate([lo.astype(BF16), hi.astype(BF16)], axis=1)


def _merge_kernel(n_prompt_tiles, seq_p, seq_s,
                  attp_ref, atts_ref, u_ref, ul_ref, ur_ref, ga_ref, gb_ref, xp_ref, xs_ref,
                  wpg_ref, psc_ref, wao_ref, wpo_ref, wo_ref, g2_ref, wr_ref, br_ref,
                  h_ref, hn_ref, lg_ref):
    i = pl.program_id(0)
    tm = TM_OUT
    is_p = i < n_prompt_tiles
    seq = jnp.where(is_p, seq_p, seq_s)
    t0 = jnp.where(is_p, i, i - n_prompt_tiles) * tm
    pos0 = t0 % seq
    first = pos0 == 0
    last = pos0 + tm == seq

    u = u_ref[...]
    zero_halo = jnp.zeros(ul_ref.shape, BF16)
    ul = jnp.where(first, zero_halo, ul_ref[...])
    ur = jnp.where(last, zero_halo, ur_ref[...])
    uext = jnp.concatenate([ul, u, ur], axis=0)
    r = lax.broadcasted_iota(I32, (tm, tm + 2 * POOL_HALO), 0)
    c = lax.broadcasted_iota(I32, (tm, tm + 2 * POOL_HALO), 1) - POOL_HALO
    pos = pos0 + lax.broadcasted_iota(I32, (tm, 1), 0)
    mixed = []
    for gi, w in enumerate(POOL_WINDOWS):
        sl = slice(gi * POOL_GROUP_W, (gi + 1) * POOL_GROUP_W)
        band = jnp.where(c >= r - w // 2, jnp.where(c < r + (w - w // 2), 1.0, 0.0), 0.0).astype(BF16)
        wsum = jnp.dot(band, uext[:, sl], preferred_element_type=F32)
        cnt = (jnp.minimum(pos + (w - w // 2), seq) - jnp.maximum(pos - w // 2, 0)).astype(F32)
        pooled = wsum / cnt - u[:, sl].astype(F32)
        mg = jnp.dot(pooled.astype(BF16), wpg_ref[gi], preferred_element_type=F32)
        mixed.append((mg * psc_ref[:, sl]).astype(BF16))
    mixed = jnp.concatenate(mixed, axis=1)

    att = jnp.where(is_p, attp_ref[...], atts_ref[...])
    y_a = jnp.dot(att, wao_ref[...], preferred_element_type=F32)
    y_b = jnp.dot(mixed, wpo_ref[...], preferred_element_type=F32)
    merged = ga_ref[...].astype(F32) * y_a + gb_ref[...].astype(F32) * y_b
    x = jnp.where(is_p, xp_ref[...], xs_ref[...])
    h = x + jnp.dot(merged.astype(BF16), wo_ref[...], preferred_element_type=F32)
    h_ref[...] = h
    ms = jnp.mean(h * h, axis=-1, keepdims=True)
    hn = (h * lax.rsqrt(ms + NORM_EPS) * g2_ref[...]).astype(BF16)
    hn_ref[...] = _pack_bf16_pairs(hn)
    lg_ref[...] = jnp.dot(hn, wr_ref[...], preferred_element_type=F32) + br_ref[...]


def _merge_project(att_p, att_s, proj, xp, xs, seq_p, seq_s, w_pool_grp, pool_scale, w_att_out, w_pool_out,
                   w_o, norm2_g, w_router, b_router):
    tm = TM_OUT
    n_p, n_s = xp.shape[0], xs.shape[0]
    n_tok = n_p + n_s
    npt = n_p // tm
    hb = tm // POOL_HALO
    n_hblk = n_tok // POOL_HALO
    u_col = 3 * ATT_W // POOL_W
    const = dict(pipeline_mode=pl.Buffered(1))
    return pl.pallas_call(
        functools.partial(_merge_kernel, npt, seq_p, seq_s),
        grid=(n_tok // tm,),
        in_specs=[
            pl.BlockSpec((tm, ATT_W), lambda i: (jnp.minimum(i, npt - 1), 0)),
            pl.BlockSpec((tm, ATT_W), lambda i: (jnp.maximum(i - npt, 0), 0)),
            pl.BlockSpec((tm, POOL_W), lambda i: (i, u_col)),
            pl.BlockSpec((POOL_HALO, POOL_W), lambda i: (jnp.maximum(i * hb - 1, 0), u_col)),
            pl.BlockSpec((POOL_HALO, POOL_W), lambda i: (jnp.minimum((i + 1) * hb, n_hblk - 1), u_col)),
            pl.BlockSpec((tm, D_MODEL), lambda i: (i, 2)),
            pl.BlockSpec((tm, D_MODEL), lambda i: (i, 3)),
            pl.BlockSpec((tm, D_MODEL), lambda i: (jnp.minimum(i, npt - 1), 0)),
            pl.BlockSpec((tm, D_MODEL), lambda i: (jnp.maximum(i - npt, 0), 0)),
            pl.BlockSpec((len(POOL_WINDOWS), POOL_GROUP_W, POOL_GROUP_W), lambda i: (0, 0, 0), **const),
            pl.BlockSpec((1, POOL_W), lambda i: (0, 0), **const),
            pl.BlockSpec((ATT_W, D_MODEL), lambda i: (0, 0), **const),
            pl.BlockSpec((POOL_W, D_MODEL), lambda i: (0, 0), **const),
            pl.BlockSpec((D_MODEL, D_MODEL), lambda i: (0, 0), **const),
            pl.BlockSpec((1, D_MODEL), lambda i: (0, 0), **const),
            pl.BlockSpec((D_MODEL, N_EXPERTS), lambda i: (0, 0), **const),
            pl.BlockSpec((1, N_EXPERTS), lambda i: (0, 0), **const),
        ],
        out_specs=[
            pl.BlockSpec((tm, D_MODEL), lambda i: (i, 0)),
            pl.BlockSpec((tm, D_MODEL // 2), lambda i: (i, 0)),
            pl.BlockSpec((tm, N_EXPERTS), lambda i: (i, 0)),
        ],
        out_shape=[
            jax.ShapeDtypeStruct((n_tok, D_MODEL), F32),
            jax.ShapeDtypeStruct((n_tok, D_MODEL // 2), jnp.uint32),
            jax.ShapeDtypeStruct((n_tok, N_EXPERTS), F32),
        ],
        compiler_params=pltpu.CompilerParams(dimension_semantics=("arbitrary",), vmem_limit_bytes=VMEM_LIMIT),
        name="merge_project",
    )(att_p, att_s, proj, proj, proj, proj, proj, xp, xs,
      w_pool_grp.astype(BF16), pool_scale.astype(F32)[None, :], w_att_out.astype(BF16),
      w_pool_out.astype(BF16), w_o.astype(BF16), norm2_g.astype(F32)[None, :],
      w_router.astype(BF16), b_router.astype(F32)[None, :])


def _route_kernel(lg_ref, idx_ref, gate_ref, rank_ref, cnt_ref, run_ref):
    i = pl.program_id(0)
    tr = TR_ROUTE

    @pl.when(i == 0)
    def _():
        run_ref[...] = jnp.zeros(run_ref.shape, F32)

    cur = lg_ref[...]
    e_iota = lax.broadcasted_iota(I32, cur.shape, 1).astype(F32)
    member = jnp.zeros(cur.shape, F32)
    vals, idxs = [], []
    for _ in range(TOP_K):
        mx = jnp.max(cur, axis=1, keepdims=True)
        am = jnp.min(jnp.where(cur == mx, e_iota, float(N_EXPERTS)), axis=1, keepdims=True)
        hit = e_iota == am
        vals.append(mx)
        idxs.append(am)
        member = member + jnp.where(hit, 1.0, 0.0)
        cur = jnp.where(hit, -jnp.inf, cur)
    exps = [jnp.exp(v - vals[0]) for v in vals]
    denom = exps[0]
    for e in exps[1:]:
        denom = denom + e

    rr = lax.broadcasted_iota(I32, (tr, tr), 0)
    cc = lax.broadcasted_iota(I32, (tr, tr), 1)
    tri = jnp.where(cc < rr, 1.0, 0.0).astype(BF16)
    before = jnp.dot(tri, member.astype(BF16), preferred_element_type=F32) + run_ref[...]
    run_ref[...] = run_ref[...] + jnp.sum(member, axis=0, keepdims=True)
    cnt_ref[...] = run_ref[...]

    lane = lax.broadcasted_iota(I32, (tr, TOP_K), 1)
    idx_out = jnp.zeros((tr, TOP_K), F32)
    gate_out = jnp.zeros((tr, TOP_K), F32)
    rank_out = jnp.zeros((tr, TOP_K), F32)
    for k in range(TOP_K):
        rk = jnp.sum(jnp.where(e_iota == idxs[k], before, 0.0), axis=1, keepdims=True)
        idx_out = jnp.where(lane == k, idxs[k], idx_out)
        gate_out = jnp.where(lane == k, exps[k] / denom, gate_out)
        rank_out = jnp.where(lane == k, rk, rank_out)
    idx_ref[...] = idx_out.astype(I32)
    gate_ref[...] = gate_out
    rank_ref[...] = rank_out.astype(I32)


def _route(logits):
    n_tok = logits.shape[0]
    tr = TR_ROUTE
    return pl.pallas_call(
        _route_kernel,
        grid=(n_tok // tr,),
        in_specs=[pl.BlockSpec((tr, N_EXPERTS), lambda i: (i, 0))],
        out_specs=[
            pl.BlockSpec((tr, TOP_K), lambda i: (i, 0)),
            pl.BlockSpec((tr, TOP_K), lambda i: (i, 0)),
            pl.BlockSpec((tr, TOP_K), lambda i: (i, 0)),
            pl.BlockSpec((1, N_EXPERTS), lambda i: (0, 0)),
        ],
        out_shape=[
            jax.ShapeDtypeStruct((n_tok, TOP_K), I32),
            jax.ShapeDtypeStruct((n_tok, TOP_K), F32),
            jax.ShapeDtypeStruct((n_tok, TOP_K), I32),
            jax.ShapeDtypeStruct((1, N_EXPERTS), F32),
        ],
        scratch_shapes=[pltpu.VMEM((1, N_EXPERTS), F32)],
        compiler_params=pltpu.CompilerParams(dimension_semantics=("arbitrary",)),
        name="route",
    )(logits)


def _dispatch_kernel(dest_ref, hn_ref, xs_ref, sem):
    td = TD_DISP

    def issue(i, carry):
        for u in range(ROW_DMA_UNROLL):
            r = i * ROW_DMA_UNROLL + u
            for k in range(TOP_K):
                d = dest_ref[r * TOP_K + k]
                pltpu.make_async_copy(hn_ref.at[pl.ds(r, 1)], xs_ref.at[pl.ds(d, 1)], sem).start(priority=k % 2)
        return carry

    lax.fori_loop(0, td // ROW_DMA_UNROLL, issue, 0)
    for _ in range(TOP_K):
        pltpu.make_async_copy(hn_ref, xs_ref.at[pl.ds(0, td)], sem).wait()


def _dispatch(hn, dest_flat, n_rows):
    n_tok = hn.shape[0]
    td = TD_DISP
    return pl.pallas_call(
        _dispatch_kernel,
        grid=(n_tok // td,),
        in_specs=[
            pl.BlockSpec((td * TOP_K,), lambda i: (i,), memory_space=pltpu.SMEM),
            pl.BlockSpec((td, hn.shape[1]), lambda i: (i, 0)),
        ],
        out_specs=pl.BlockSpec(memory_space=pl.ANY),
        out_shape=jax.ShapeDtypeStruct((n_rows, hn.shape[1]), hn.dtype),
        scratch_shapes=[pltpu.SemaphoreType.DMA(())],
        compiler_params=pltpu.CompilerParams(dimension_semantics=("arbitrary",)),
        name="dispatch",
    )(dest_flat, hn)


def _ffn_kernel(ge_ref, gs_ref, gn_ref, ng_ref, bv_ref,
                xs_hbm, wg_ref, bg_ref, wu_ref, bu_ref, wd_ref, bd_ref,
                ys_hbm, slab_ref, acc_ref, sem_x, sem_o):
    grp = pl.program_id(0)
    f = pl.program_id(1)
    nf = pl.num_programs(1)
    tm = TM_MOE
    n_grp = ng_ref[0]

    def load(g, slot):
        row0 = pl.multiple_of(gs_ref[g] * tm, tm)
        return pltpu.make_async_copy(xs_hbm.at[pl.ds(row0, GT_MOE * tm)], slab_ref.at[slot], sem_x.at[slot])

    def store(g, r):
        row0 = pl.multiple_of((gs_ref[g] + r) * tm, tm)
        return pltpu.make_async_copy(acc_ref.at[pl.ds(r * tm, tm)], ys_hbm.at[pl.ds(row0, tm)], sem_o)

    @pl.when(grp < n_grp)
    def _():
        start = gs_ref[grp]
        nt = gn_ref[grp]
        slot = grp % 2

        @pl.when(f == 0)
        def _():
            pl.when(grp == 0)(lambda: load(0, 0).start())

            @pl.when(grp > 0)
            def _():
                for r in range(GT_MOE):
                    pl.when(r < gn_ref[grp - 1])(lambda r=r: store(grp - 1, r).wait())

            load(grp, slot).wait()
            pl.when(grp + 1 < n_grp)(lambda: load(grp + 1, 1 - slot).start())

        def row_tiles(tiles):
            wg = wg_ref[0].astype(BF16)
            wu = wu_ref[0].astype(BF16)
            wd = wd_ref[0].astype(BF16)
            row = lax.broadcasted_iota(I32, (tm, 1), 0)
            gates, ups = [], []
            for r in tiles:
                words = jnp.where(row < bv_ref[start + r], slab_ref[slot, r * tm:(r + 1) * tm, :], jnp.uint32(0))
                xb = _unpack_bf16_pairs(words)
                gates.append(jnp.dot(xb, wg, preferred_element_type=F32) + bg_ref[0])
                ups.append(jnp.dot(xb, wu, preferred_element_type=F32) + bu_ref[0])
            hidden = []
            for g, up in zip(gates, ups):
                g = jnp.minimum(g, SWIGLU_LIMIT)
                up = jnp.clip(up, -SWIGLU_LIMIT, SWIGLU_LIMIT)
                hidden.append(((up + 1.0) * (g * jax.nn.sigmoid(SWIGLU_ALPHA * g))).astype(BF16))
            for r, hmid in zip(tiles, hidden):
                rows = slice(r * tm, (r + 1) * tm)
                base = jnp.where(f == 0, jnp.broadcast_to(bd_ref[0], (tm, D_MODEL)), acc_ref[rows, :])
                acc_ref[rows, :] = base + jnp.dot(hmid, wd, preferred_element_type=F32)

            @pl.when(f == nf - 1)
            def _():
                for r in tiles:
                    store(grp, r).start()

        assert GT_MOE == 4
        pl.when(nt >= 2)(lambda: row_tiles((0, 1)))
        pl.when(nt == 1)(lambda: row_tiles((0,)))
        pl.when(nt == 4)(lambda: row_tiles((2, 3)))
        pl.when(nt == 3)(lambda: row_tiles((2,)))

        @pl.when((f == nf - 1) & (grp == n_grp - 1))
        def _():
            for r in range(GT_MOE):
                pl.when(r < nt)(lambda r=r: store(grp, r).wait())


def _expert_ffn(xs, grp_expert, grp_start, grp_ntiles, n_groups, blk_valid,
                w_gate, b_gate, w_up, b_up, w_down, b_down):
    n_rows = xs.shape[0]
    max_groups = grp_expert.shape[0]
    nf = D_FF // TF_MOE

    def expert(g, ge, ng):
        return ge[jnp.minimum(g, ng[0] - 1)]

    def ftile(g, f, ng):
        return jnp.where(g < ng[0], f, nf - 1)

    grid_spec = pltpu.PrefetchScalarGridSpec(
        num_scalar_prefetch=5,
        grid=(max_groups, nf),
        in_specs=[
            pl.BlockSpec(memory_space=pl.ANY),
            pl.BlockSpec((1, D_MODEL, TF_MOE), lambda g, f, ge, gs, gn, ng, bv: (expert(g, ge, ng), 0, ftile(g, f, ng))),
            pl.BlockSpec((1, 1, TF_MOE), lambda g, f, ge, gs, gn, ng, bv: (expert(g, ge, ng), 0, ftile(g, f, ng))),
            pl.BlockSpec((1, D_MODEL, TF_MOE), lambda g, f, ge, gs, gn, ng, bv: (expert(g, ge, ng), 0, ftile(g, f, ng))),
            pl.BlockSpec((1, 1, TF_MOE), lambda g, f, ge, gs, gn, ng, bv: (expert(g, ge, ng), 0, ftile(g, f, ng))),
            pl.BlockSpec((1, TF_MOE, D_MODEL), lambda g, f, ge, gs, gn, ng, bv: (expert(g, ge, ng), ftile(g, f, ng), 0)),
            pl.BlockSpec((1, 1, D_MODEL), lambda g, f, ge, gs, gn, ng, bv: (expert(g, ge, ng), 0, 0)),
        ],
        out_specs=pl.BlockSpec(memory_space=pl.ANY),
        scratch_shapes=[
            pltpu.VMEM((2, GT_MOE * TM_MOE, D_MODEL // 2), jnp.uint32),
            pltpu.VMEM((GT_MOE * TM_MOE, D_MODEL), F32),
            pltpu.SemaphoreType.DMA((2,)),
            pltpu.SemaphoreType.DMA(()),
        ],
    )
    return pl.pallas_call(
        _ffn_kernel,
        grid_spec=grid_spec,
        out_shape=jax.ShapeDtypeStruct((n_rows, D_MODEL), F32),
        compiler_params=pltpu.CompilerParams(
            dimension_semantics=("arbitrary", "arbitrary"), vmem_limit_bytes=VMEM_LIMIT),
        name="expert_ffn",
    )(grp_expert, grp_start, grp_ntiles, n_groups, blk_valid, xs,
      w_gate, b_gate[:, None, :], w_up, b_up[:, None, :], w_down, b_down[:, None, :])


def _combine_kernel(dest_ref, h_ref, gate_ref, ys_ref, o_ref, buf_ref, sem):
    tc = TC_COMB

    def issue(i, carry):
        for u in range(ROW_DMA_UNROLL):
            r = i * ROW_DMA_UNROLL + u
            for k in range(TOP_K):
                d = dest_ref[r * TOP_K + k]
                pltpu.make_async_copy(ys_ref.at[pl.ds(d, 1)], buf_ref.at[k, pl.ds(r, 1)], sem).start(priority=k % 2)
        return carry

    lax.fori_loop(0, tc // ROW_DMA_UNROLL, issue, 0)
    for k in range(TOP_K):
        pltpu.make_async_copy(ys_ref.at[pl.ds(0, tc)], buf_ref.at[k], sem).wait()
    gates = gate_ref[...]
    y = h_ref[...]
    for k in range(TOP_K):
        y = y + gates[:, k:k + 1] * buf_ref[k]
    o_ref[...] = y


def _combine(h, gates, dest_flat, ys, tok_off, n_out):
    tc = TC_COMB
    off = tok_off // tc
    return pl.pallas_call(
        _combine_kernel,
        grid=(n_out // tc,),
        in_specs=[
            pl.BlockSpec((tc * TOP_K,), lambda i: (i + off,), memory_space=pltpu.SMEM),
            pl.BlockSpec((tc, D_MODEL), lambda i: (i + off, 0)),
            pl.BlockSpec((tc, TOP_K), lambda i: (i + off, 0)),
            pl.BlockSpec(memory_space=pl.ANY),
        ],
        out_specs=pl.BlockSpec((tc, D_MODEL), lambda i: (i, 0)),
        out_shape=jax.ShapeDtypeStruct((n_out, D_MODEL), F32),
        scratch_shapes=[pltpu.VMEM((TOP_K, tc, D_MODEL), F32), pltpu.SemaphoreType.DMA(())],
        compiler_params=pltpu.CompilerParams(dimension_semantics=("arbitrary",), vmem_limit_bytes=VMEM_LIMIT),
        name="combine",
    )(dest_flat, h, gates, ys)


def kernel(x_prompt, x_sample, norm1_g, w_in, q_norm_g, k_norm_g, lambda_q1, lambda_k1, lambda_q2, lambda_k2,
           subln_g, rel_bias, w_pool_grp, pool_scale, w_att_out, w_pool_out, w_o, norm2_g, w_router, b_router,
           w_gate, b_gate, w_up, b_up, w_down, b_down):
    bp, sp, _ = x_prompt.shape
    bs, ss, _ = x_sample.shape
    n_p, n_s = bp * sp, bs * ss
    n_tok = n_p + n_s
    xp = x_prompt.reshape(n_p, D_MODEL)
    xs = x_sample.reshape(n_s, D_MODEL)

    proj = _in_projection(xp, xs, norm1_g[0], w_in[0].astype(BF16), q_norm_g[0], k_norm_g[0])
    band = _bias_band(rel_bias)
    lam_vecs = (lambda_q1[0], lambda_k1[0], lambda_q2[0], lambda_k2[0])
    att_p = _attention(proj, band, lam_vecs, subln_g[0], bp, sp, 0)
    att_s = _attention(proj, band, lam_vecs, subln_g[0], bs, ss, n_p)
    h, hn, logits = _merge_project(att_p, att_s, proj, xp, xs, sp, ss, w_pool_grp[0], pool_scale[0],
                                   w_att_out[0], w_pool_out[0], w_o[0], norm2_g[0], w_router[0], b_router[0])

    idx, gates, rank, counts = _route(logits)

    n_blocks = (n_tok * TOP_K + N_EXPERTS * (TM_MOE - 1)) // TM_MOE
    counts = counts[0].astype(I32)
    padded = (counts + TM_MOE - 1) // TM_MOE * TM_MOE
    pad_end = jnp.cumsum(padded)
    start_pad = pad_end - padded
    dest = (start_pad[idx] + rank).reshape(-1)
    blk_start = jnp.arange(n_blocks, dtype=I32) * TM_MOE
    blk_expert = jnp.minimum(jnp.searchsorted(pad_end, blk_start, side="right"), N_EXPERTS - 1).astype(I32)
    blk_valid = jnp.clip(counts[blk_expert] - (blk_start - start_pad[blk_expert]), 0, TM_MOE).astype(I32)
    e_blocks = padded // TM_MOE
    e_groups = (e_blocks + GT_MOE - 1) // GT_MOE
    grp_end = jnp.cumsum(e_groups)
    max_groups = n_blocks // GT_MOE + N_EXPERTS
    gidx = jnp.arange(max_groups, dtype=I32)
    grp_expert = jnp.minimum(jnp.searchsorted(grp_end, gidx, side="right"), N_EXPERTS - 1).astype(I32)
    in_expert = gidx - (grp_end - e_groups)[grp_expert]
    base = (e_blocks // jnp.maximum(e_groups, 1))[grp_expert]
    rem = e_blocks[grp_expert] - base * e_groups[grp_expert]
    grp_start = (start_pad[grp_expert] // TM_MOE + in_expert * base + jnp.minimum(in_expert, rem)).astype(I32)
    grp_ntiles = (base + (in_expert < rem)).astype(I32)
    n_groups = grp_end[-1:].astype(I32)

    n_rows = (n_blocks + GT_MOE - 1) * TM_MOE
    xs_sorted = _dispatch(hn, dest, n_rows)
    ys = _expert_ffn(xs_sorted, grp_expert, grp_start, grp_ntiles, n_groups, blk_valid,
                     w_gate[0], b_gate[0], w_up[0], b_up[0], w_down[0], b_down[0])
    y_p = _combine(h, gates, dest, ys, 0, n_p)
    y_s = _combine(h, gates, dest, ys, n_p, n_s)
    return (y_p.reshape(bp, sp, D_MODEL), y_s.reshape(bs, ss, D_MODEL))
```

```python
import functools
import math

import jax
import jax.numpy as jnp
from jax import lax
from jax.experimental import pallas as pl
from jax.experimental.pallas import tpu as pltpu

F32 = jnp.float32
BF16 = jnp.bfloat16
I32 = jnp.int32

D_MODEL = 2048
N_HEADS = 8
HEAD_DIM = 64
V_DIM = 2 * HEAD_DIM
ATT_W = N_HEADS * V_DIM
POOL_WINDOWS = (2, 4, 8, 16)
POOL_W = D_MODEL // 2
POOL_GROUP_W = POOL_W // len(POOL_WINDOWS)
IN_COLS = 3 * ATT_W + POOL_W + 2 * D_MODEL
N_BUCKETS = 32
MAX_DISTANCE = 128
N_EXPERTS = 32
TOP_K = 4
D_FF = D_MODEL
SWIGLU_LIMIT = 7.0
SWIGLU_ALPHA = 1.702
NORM_EPS = 1e-6
LAM_INIT = 0.8 - 0.6 * math.exp(-0.3 * 0)
LOG2E = math.log2(math.e)

LANES = 128
VMEM_LIMIT = 56 * 1024 * 1024

TM_IN = 512
TN_IN = 1024
T_ATT = 256
G_ATT = 4
V_AUG = V_DIM + 16
TM_OUT = 256
POOL_HALO = 16
TR_ROUTE = 512
TM_MOE = 512
TF_MOE = 256
GT_MOE = 4
TD_DISP = 256
TC_COMB = 256
ROW_DMA_UNROLL = 8


def _t5_thresholds():
    half = N_BUCKETS // 2
    max_exact = half // 2
    steps = half - max_exact
    ratio = MAX_DISTANCE // max_exact
    out = []
    for k in range(1, steps):
        n = max_exact
        while n ** steps < (max_exact ** steps) * (ratio ** k):
            n += 1
        out.append(n)
    return tuple(out)


T5_THRESHOLDS = _t5_thresholds()
T5_FAR = T5_THRESHOLDS[-1]


def _inproj_kernel(n_prompt_tiles, xp_ref, xs_ref, g1_ref, w_ref, gsum_ref, qg_ref, kg_ref, o_ref, xn_ref):
    i = pl.program_id(0)
    j = pl.program_id(1)

    @pl.when(j == 0)
    def _():
        x = jnp.where(i < n_prompt_tiles, xp_ref[...], xs_ref[...])
        ms = jnp.mean(x * x, axis=-1, keepdims=True)
        xn_ref[...] = (x * lax.rsqrt(ms + NORM_EPS) * g1_ref[...]).astype(BF16)

    acc = jnp.dot(xn_ref[...], w_ref[...], preferred_element_type=F32)

    def head_norm(g_ref, scale):
        sq = (acc * acc).astype(BF16)
        gw = gsum_ref.shape[0]
        parts = [jnp.dot(sq[:, c * gw:(c + 1) * gw], gsum_ref[...], preferred_element_type=F32)
                 for c in range(TN_IN // gw)]
        ss = jnp.concatenate(parts, axis=1)
        return acc * lax.rsqrt(ss * (1.0 / HEAD_DIM) + NORM_EPS) * (g_ref[...] * scale)

    @pl.when(j == 0)
    def _():
        o_ref[...] = head_norm(qg_ref, HEAD_DIM ** -0.5 * LOG2E).astype(BF16)

    @pl.when(j == 1)
    def _():
        o_ref[...] = head_norm(kg_ref, 1.0).astype(BF16)

    @pl.when((j == 2) | (j == 3))
    def _():
        o_ref[...] = acc.astype(BF16)

    @pl.when(j >= 4)
    def _():
        o_ref[...] = (0.5 * jnp.tanh(0.5 * acc) + 0.5).astype(BF16)


def _in_projection(xp, xs, norm1_g, w_in_bf, q_norm_g, k_norm_g):
    n_p, n_s = xp.shape[0], xs.shape[0]
    n_tok = n_p + n_s
    npt, nst = n_p // TM_IN, n_s // TM_IN
    gw = 2 * LANES
    gid = jnp.arange(gw) // HEAD_DIM
    gsum = (gid[:, None] == gid[None, :]).astype(BF16)
    qg = jnp.tile(q_norm_g.astype(F32), TN_IN // HEAD_DIM)[None, :]
    kg = jnp.tile(k_norm_g.astype(F32), TN_IN // HEAD_DIM)[None, :]
    return pl.pallas_call(
        functools.partial(_inproj_kernel, npt),
        grid=(n_tok // TM_IN, IN_COLS // TN_IN),
        in_specs=[
            pl.BlockSpec((TM_IN, D_MODEL), lambda i, j: (jnp.minimum(i, npt - 1), 0)),
            pl.BlockSpec((TM_IN, D_MODEL), lambda i, j: (jnp.maximum(i - npt, 0), 0)),
            pl.BlockSpec((1, D_MODEL), lambda i, j: (0, 0)),
            pl.BlockSpec((D_MODEL, TN_IN), lambda i, j: (0, j)),
            pl.BlockSpec((gw, gw), lambda i, j: (0, 0)),
            pl.BlockSpec((1, TN_IN), lambda i, j: (0, 0)),
            pl.BlockSpec((1, TN_IN), lambda i, j: (0, 0)),
        ],
        out_specs=pl.BlockSpec((TM_IN, TN_IN), lambda i, j: (i, j)),
        out_shape=jax.ShapeDtypeStruct((n_tok, IN_COLS), BF16),
        scratch_shapes=[pltpu.VMEM((TM_IN, D_MODEL), BF16)],
        compiler_params=pltpu.CompilerParams(
            dimension_semantics=("arbitrary", "arbitrary"), vmem_limit_bytes=VMEM_LIMIT),
        name="in_projection",
    )(xp, xs, norm1_g.astype(F32)[None, :], w_in_bf, gsum, qg, kg)


N_BAND = 5


def _band_kernel(rb_ref, o_ref):
    h = pl.program_id(0)
    t = T_ATT
    kj = lax.broadcasted_iota(I32, (t, t), 0)
    qi = lax.broadcasted_iota(I32, (t, t), 1)
    half = N_BUCKETS // 2
    max_exact = half // 2
    for d in range(N_BAND):
        rel = (d - N_BAND // 2) * t + kj - qi
        n = jnp.abs(rel)
        large = jnp.full((t, t), max_exact, I32)
        for th in T5_THRESHOLDS:
            large = large + jnp.where(n >= th, 1, 0)
        bucket = jnp.where(rel > 0, half, 0) + jnp.where(n < max_exact, n, large)
        val = jnp.zeros((t, t), F32)
        for b in range(N_BUCKETS):
            val = jnp.where(bucket == b, rb_ref[b, h], val)
        o_ref[0, d] = val * LOG2E


def _bias_band(rel_bias):
    assert T_ATT >= T5_FAR, "tiles two or more away from the diagonal must lie in the constant-bias region"
    return pl.pallas_call(
        _band_kernel,
        grid=(N_HEADS,),
        in_specs=[pl.BlockSpec(memory_space=pltpu.SMEM)],
        out_specs=pl.BlockSpec((1, N_BAND, T_ATT, T_ATT), lambda h: (h, 0, 0, 0)),
        out_shape=jax.ShapeDtypeStruct((N_HEADS, N_BAND, T_ATT, T_ATT), F32),
        compiler_params=pltpu.CompilerParams(dimension_semantics=("arbitrary",)),
        name="bias_band",
    )(rel_bias.astype(F32))


def _attn_kernel(seq, q_ref, k_ref, v_ref, band_ref, lq1_ref, lk1_ref, lq2_ref, lk2_ref, sg_ref,
                 o_ref, vt_ref, qz_ref, m_ref, acc_ref, s0_ref, s1_ref, c0_ref, c1_ref):
    qi = pl.program_id(2)
    t = T_ATT
    nk = seq // t
    heads = range(G_ATT)

    def cols(g):
        return slice(g * V_DIM, (g + 1) * V_DIM)

    @pl.when(qi == 0)
    def _():
        for g in heads:
            for c in range(nk):
                vt_ref[g, c, 0:V_DIM, :] = v_ref[c * t:(c + 1) * t, cols(g)].astype(F32).T.astype(BF16)
                vt_ref[g, c, V_DIM:V_AUG, :] = jnp.ones((V_AUG - V_DIM, t), BF16)

    for g in heads:
        q = q_ref[:, cols(g)]
        lane = lax.broadcasted_iota(I32, q.shape, 1)
        qz_ref[g, 0:t, :] = jnp.where(lane < HEAD_DIM, q, jnp.zeros_like(q))
        qz_ref[g, t:2 * t, :] = jnp.where(lane >= HEAD_DIM, q, jnp.zeros_like(q))
    m_ref[...] = jnp.full(m_ref.shape, -1e30, F32)
    acc_ref[...] = jnp.zeros(acc_ref.shape, F32)

    def score_tiles(ki, s_ref, c_ref):
        band = jnp.clip(ki - qi + N_BAND // 2, 0, N_BAND - 1)
        for g in heads:
            k_c = k_ref[pl.ds(pl.multiple_of(ki * t, t), t), cols(g)]
            bias = band_ref[g, band]
            s = lax.dot_general(k_c, qz_ref[g], (((1,), (1,)), ((), ())), preferred_element_type=F32)
            s = s + jnp.concatenate([bias, bias], axis=1)
            s_ref[g] = s
            c_ref[g] = jnp.max(s, axis=0, keepdims=True)

    def softmax_pv(ki, s_ref, c_ref):
        for g in heads:
            m_old = m_ref[g]
            m_new = jnp.maximum(m_old, c_ref[g])
            m_ref[g] = m_new
            alpha = jnp.exp2(m_old - m_new)
            p = jnp.exp2(s_ref[g] - m_new).astype(BF16)
            pv = jnp.dot(vt_ref[g, ki], p, preferred_element_type=F32)
            acc_ref[g] = acc_ref[g] * alpha + pv

    score_tiles(0, s0_ref, c0_ref)

    def pair(j, carry):
        k0 = 2 * j
        score_tiles(k0 + 1, s1_ref, c1_ref)
        softmax_pv(k0, s0_ref, c0_ref)
        score_tiles(jnp.minimum(k0 + 2, nk - 1), s0_ref, c0_ref)
        softmax_pv(k0 + 1, s1_ref, c1_ref)
        return carry

    lax.fori_loop(0, nk // 2, pair, 0)

    lam = (jnp.exp(jnp.sum(lq1_ref[...] * lk1_ref[...], axis=1, keepdims=True))
           - jnp.exp(jnp.sum(lq2_ref[...] * lk2_ref[...], axis=1, keepdims=True)) + LAM_INIT)
    for g in heads:
        acc = acc_ref[g, 0:V_DIM, :]
        l = acc_ref[g, V_DIM:V_DIM + 1, :]
        o = acc[:, 0:t] / l[:, 0:t] - lam * (acc[:, t:2 * t] / l[:, t:2 * t])
        ms = jnp.mean(o * o, axis=0, keepdims=True)
        on = o * lax.rsqrt(ms + NORM_EPS) * sg_ref[...] * (1.0 - LAM_INIT)
        o_ref[:, cols(g)] = on.T.astype(BF16)


def _attention(proj, band, lam_vecs, subln_g, batch, seq, row_off):
    t = T_ATT
    nq = seq // t
    assert nq % 2 == 0, "key tiles are processed in pairs"
    gw = G_ATT * V_DIM
    sec = ATT_W // gw
    lq1, lk1, lq2, lk2 = [v.astype(F32)[None, :] for v in lam_vecs]
    vec_spec = pl.BlockSpec((1, HEAD_DIM), lambda b, h, qi: (0, 0))
    return pl.pallas_call(
        functools.partial(_attn_kernel, seq),
        grid=(batch, N_HEADS // G_ATT, nq),
        in_specs=[
            pl.BlockSpec((t, gw), lambda b, h, qi: (row_off // t + b * nq + qi, h)),
            pl.BlockSpec((seq, gw), lambda b, h, qi: (row_off // seq + b, sec + h)),
            pl.BlockSpec((seq, gw), lambda b, h, qi: (row_off // seq + b, 2 * sec + h)),
            pl.BlockSpec((G_ATT, N_BAND, t, t), lambda b, h, qi: (h, 0, 0, 0)),
            vec_spec, vec_spec, vec_spec, vec_spec,
            pl.BlockSpec((V_DIM, 1), lambda b, h, qi: (0, 0)),
        ],
        out_specs=pl.BlockSpec((t, gw), lambda b, h, qi: (b * nq + qi, h)),
        out_shape=jax.ShapeDtypeStruct((batch * seq, ATT_W), BF16),
        scratch_shapes=[
            pltpu.VMEM((G_ATT, seq // t, V_AUG, t), BF16),
            pltpu.VMEM((G_ATT, 2 * t, V_DIM), BF16),
            pltpu.VMEM((G_ATT, 1, 2 * t), F32),
            pltpu.VMEM((G_ATT, V_AUG, 2 * t), F32),
            pltpu.VMEM((G_ATT, t, 2 * t), F32),
            pltpu.VMEM((G_ATT, t, 2 * t), F32),
            pltpu.VMEM((G_ATT, 1, 2 * t), F32),
            pltpu.VMEM((G_ATT, 1, 2 * t), F32),
        ],
        compiler_params=pltpu.CompilerParams(
            dimension_semantics=("arbitrary", "arbitrary", "arbitrary"), vmem_limit_bytes=VMEM_LIMIT),
        name=f"diff_attention_s{seq}",
    )(proj, proj, proj, band, lq1, lk1, lq2, lk2, subln_g.astype(F32)[:, None])


def _pack_bf16_pairs(x):
    n = x.shape[1] // 2
    lo = lax.bitcast_convert_type(x[:, :n].astype(F32), jnp.uint32)
    hi = lax.bitcast_convert_type(x[:, n:].astype(F32), jnp.uint32)
    return lax.shift_right_logical(lo, jnp.uint32(16)) | (hi & jnp.uint32(0xFFFF0000))


def _unpack_bf16_pairs(w):
    lo = lax.bitcast_convert_type(lax.shift_left(w, jnp.uint32(16)), F32)
    hi = lax.bitcast_convert_type(w & jnp.uint32(0xFFFF0000), F32)
    return jnp.concatenate([lo.astype(BF16), hi.astype(BF16)], axis=1)


def _merge_kernel(n_prompt_tiles, seq_p, seq_s,
                  attp_ref, atts_ref, u_ref, ul_ref, ur_ref, ga_ref, gb_ref, xp_ref, xs_ref,
                  wpg_ref, psc_ref, wao_ref, wpo_ref, wo_ref, g2_ref, wr_ref, br_ref,
                  h_ref, hn_ref, lg_ref):
    i = pl.program_id(0)
    tm = TM_OUT
    is_p = i < n_prompt_tiles
    seq = jnp.where(is_p, seq_p, seq_s)
    t0 = jnp.where(is_p, i, i - n_prompt_tiles) * tm
    pos0 = t0 % seq
    first = pos0 == 0
    last = pos0 + tm == seq

    u = u_ref[...]
    zero_halo = jnp.zeros(ul_ref.shape, BF16)
    ul = jnp.where(first, zero_halo, ul_ref[...])
    ur = jnp.where(last, zero_halo, ur_ref[...])
    uext = jnp.concatenate([ul, u, ur], axis=0)
    r = lax.broadcasted_iota(I32, (tm, tm + 2 * POOL_HALO), 0)
    c = lax.broadcasted_iota(I32, (tm, tm + 2 * POOL_HALO), 1) - POOL_HALO
    pos = pos0 + lax.broadcasted_iota(I32, (tm, 1), 0)
    mixed = []
    for gi, w in enumerate(POOL_WINDOWS):
        sl = slice(gi * POOL_GROUP_W, (gi + 1) * POOL_GROUP_W)
        band = jnp.where(c >= r - w // 2, jnp.where(c < r + (w - w // 2), 1.0, 0.0), 0.0).astype(BF16)
        wsum = jnp.dot(band, uext[:, sl], preferred_element_type=F32)
        cnt = (jnp.minimum(pos + (w - w // 2), seq) - jnp.maximum(pos - w // 2, 0)).astype(F32)
        pooled = wsum / cnt - u[:, sl].astype(F32)
        mg = jnp.dot(pooled.astype(BF16), wpg_ref[gi], preferred_element_type=F32)
        mixed.append((mg * psc_ref[:, sl]).astype(BF16))
    mixed = jnp.concatenate(mixed, axis=1)

    att = jnp.where(is_p, attp_ref[...], atts_ref[...])
    y_a = jnp.dot(att, wao_ref[...], preferred_element_type=F32)
    y_b = jnp.dot(mixed, wpo_ref[...], preferred_element_type=F32)
    merged = ga_ref[...].astype(F32) * y_a + gb_ref[...].astype(F32) * y_b
    x = jnp.where(is_p, xp_ref[...], xs_ref[...])
    h = x + jnp.dot(merged.astype(BF16), wo_ref[...], preferred_element_type=F32)
    h_ref[...] = h
    ms = jnp.mean(h * h, axis=-1, keepdims=True)
    hn = (h * lax.rsqrt(ms + NORM_EPS) * g2_ref[...]).astype(BF16)
    hn_ref[...] = _pack_bf16_pairs(hn)
    lg_ref[...] = jnp.dot(hn, wr_ref[...], preferred_element_type=F32) + br_ref[...]


def _merge_project(att_p, att_s, proj, xp, xs, seq_p, seq_s, w_pool_grp, pool_scale, w_att_out, w_pool_out,
                   w_o, norm2_g, w_router, b_router):
    tm = TM_OUT
    n_p, n_s = xp.shape[0], xs.shape[0]
    n_tok = n_p + n_s
    npt = n_p // tm
    hb = tm // POOL_HALO
    n_hblk = n_tok // POOL_HALO
    u_col = 3 * ATT_W // POOL_W
    const = dict(pipeline_mode=pl.Buffered(1))
    return pl.pallas_call(
        functools.partial(_merge_kernel, npt, seq_p, seq_s),
        grid=(n_tok // tm,),
        in_specs=[
            pl.BlockSpec((tm, ATT_W), lambda i: (jnp.minimum(i, npt - 1), 0)),
            pl.BlockSpec((tm, ATT_W), lambda i: (jnp.maximum(i - npt, 0), 0)),
            pl.BlockSpec((tm, POOL_W), lambda i: (i, u_col)),
            pl.BlockSpec((POOL_HALO, POOL_W), lambda i: (jnp.maximum(i * hb - 1, 0), u_col)),
            pl.BlockSpec((POOL_HALO, POOL_W), lambda i: (jnp.minimum((i + 1) * hb, n_hblk - 1), u_col)),
            pl.BlockSpec((tm, D_MODEL), lambda i: (i, 2)),
            pl.BlockSpec((tm, D_MODEL), lambda i: (i, 3)),
            pl.BlockSpec((tm, D_MODEL), lambda i: (jnp.minimum(i, npt - 1), 0)),
            pl.BlockSpec((tm, D_MODEL), lambda i: (jnp.maximum(i - npt, 0), 0)),
            pl.BlockSpec((len(POOL_WINDOWS), POOL_GROUP_W, POOL_GROUP_W), lambda i: (0, 0, 0), **const),
            pl.BlockSpec((1, POOL_W), lambda i: (0, 0), **const),
            pl.BlockSpec((ATT_W, D_MODEL), lambda i: (0, 0), **const),
            pl.BlockSpec((POOL_W, D_MODEL), lambda i: (0, 0), **const),
            pl.BlockSpec((D_MODEL, D_MODEL), lambda i: (0, 0), **const),
            pl.BlockSpec((1, D_MODEL), lambda i: (0, 0), **const),
            pl.BlockSpec((D_MODEL, N_EXPERTS), lambda i: (0, 0), **const),
            pl.BlockSpec((1, N_EXPERTS), lambda i: (0, 0), **const),
        ],
        out_specs=[
            pl.BlockSpec((tm, D_MODEL), lambda i: (i, 0)),
            pl.BlockSpec((tm, D_MODEL // 2), lambda i: (i, 0)),
            pl.BlockSpec((tm, N_EXPERTS), lambda i: (i, 0)),
        ],
        out_shape=[
            jax.ShapeDtypeStruct((n_tok, D_MODEL), F32),
            jax.ShapeDtypeStruct((n_tok, D_MODEL // 2), jnp.uint32),
            jax.ShapeDtypeStruct((n_tok, N_EXPERTS), F32),
        ],
        compiler_params=pltpu.CompilerParams(dimension_semantics=("arbitrary",), vmem_limit_bytes=VMEM_LIMIT),
        name="merge_project",
    )(att_p, att_s, proj, proj, proj, proj, proj, xp, xs,
      w_pool_grp.astype(BF16), pool_scale.astype(F32)[None, :], w_att_out.astype(BF16),
      w_pool_out.astype(BF16), w_o.astype(BF16), norm2_g.astype(F32)[None, :],
      w_router.astype(BF16), b_router.astype(F32)[None, :])


def _route_kernel(lg_ref, idx_ref, gate_ref, rank_ref, cnt_ref, run_ref):
    i = pl.program_id(0)
    tr = TR_ROUTE

    @pl.when(i == 0)
    def _():
        run_ref[...] = jnp.zeros(run_ref.shape, F32)

    cur = lg_ref[...]
    e_iota = lax.broadcasted_iota(I32, cur.shape, 1).astype(F32)
    member = jnp.zeros(cur.shape, F32)
    vals, idxs = [], []
    for _ in range(TOP_K):
        mx = jnp.max(cur, axis=1, keepdims=True)
        am = jnp.min(jnp.where(cur == mx, e_iota, float(N_EXPERTS)), axis=1, keepdims=True)
        hit = e_iota == am
        vals.append(mx)
        idxs.append(am)
        member = member + jnp.where(hit, 1.0, 0.0)
        cur = jnp.where(hit, -jnp.inf, cur)
    exps = [jnp.exp(v - vals[0]) for v in vals]
    denom = exps[0]
    for e in exps[1:]:
        denom = denom + e

    rr = lax.broadcasted_iota(I32, (tr, tr), 0)
    cc = lax.broadcasted_iota(I32, (tr, tr), 1)
    tri = jnp.where(cc < rr, 1.0, 0.0).astype(BF16)
    before = jnp.dot(tri, member.astype(BF16), preferred_element_type=F32) + run_ref[...]
    run_ref[...] = run_ref[...] + jnp.sum(member, axis=0, keepdims=True)
    cnt_ref[...] = run_ref[...]

    lane = lax.broadcasted_iota(I32, (tr, TOP_K), 1)
    idx_out = jnp.zeros((tr, TOP_K), F32)
    gate_out = jnp.zeros((tr, TOP_K), F32)
    rank_out = jnp.zeros((tr, TOP_K), F32)
    for k in range(TOP_K):
        rk = jnp.sum(jnp.where(e_iota == idxs[k], before, 0.0), axis=1, keepdims=True)
        idx_out = jnp.where(lane == k, idxs[k], idx_out)
        gate_out = jnp.where(lane == k, exps[k] / denom, gate_out)
        rank_out = jnp.where(lane == k, rk, rank_out)
    idx_ref[...] = idx_out.astype(I32)
    gate_ref[...] = gate_out
    rank_ref[...] = rank_out.astype(I32)


def _route(logits):
    n_tok = logits.shape[0]
    tr = TR_ROUTE
    return pl.pallas_call(
        _route_kernel,
        grid=(n_tok // tr,),
        in_specs=[pl.BlockSpec((tr, N_EXPERTS), lambda i: (i, 0))],
        out_specs=[
            pl.BlockSpec((tr, TOP_K), lambda i: (i, 0)),
            pl.BlockSpec((tr, TOP_K), lambda i: (i, 0)),
            pl.BlockSpec((tr, TOP_K), lambda i: (i, 0)),
            pl.BlockSpec((1, N_EXPERTS), lambda i: (0, 0)),
        ],
        out_shape=[
            jax.ShapeDtypeStruct((n_tok, TOP_K), I32),
            jax.ShapeDtypeStruct((n_tok, TOP_K), F32),
            jax.ShapeDtypeStruct((n_tok, TOP_K), I32),
            jax.ShapeDtypeStruct((1, N_EXPERTS), F32),
        ],
        scratch_shapes=[pltpu.VMEM((1, N_EXPERTS), F32)],
        compiler_params=pltpu.CompilerParams(dimension_semantics=("arbitrary",)),
        name="route",
    )(logits)


def _dispatch_kernel(dest_ref, hn_ref, xs_ref, sem):
    td = TD_DISP

    def issue(i, carry):
        for u in range(ROW_DMA_UNROLL):
            r = i * ROW_DMA_UNROLL + u
            for k in range(TOP_K):
                d = dest_ref[r * TOP_K + k]
                pltpu.make_async_copy(hn_ref.at[pl.ds(r, 1)], xs_ref.at[pl.ds(d, 1)], sem).start(priority=k % 2)
        return carry

    lax.fori_loop(0, td // ROW_DMA_UNROLL, issue, 0)
    for _ in range(TOP_K):
        pltpu.make_async_copy(hn_ref, xs_ref.at[pl.ds(0, td)], sem).wait()


def _dispatch(hn, dest_flat, n_rows):
    n_tok = hn.shape[0]
    td = TD_DISP
    return pl.pallas_call(
        _dispatch_kernel,
        grid=(n_tok // td,),
        in_specs=[
            pl.BlockSpec((td * TOP_K,), lambda i: (i,), memory_space=pltpu.SMEM),
            pl.BlockSpec((td, hn.shape[1]), lambda i: (i, 0)),
        ],
        out_specs=pl.BlockSpec(memory_space=pl.ANY),
        out_shape=jax.ShapeDtypeStruct((n_rows, hn.shape[1]), hn.dtype),
        scratch_shapes=[pltpu.SemaphoreType.DMA(())],
        compiler_params=pltpu.CompilerParams(dimension_semantics=("arbitrary",)),
        name="dispatch",
    )(dest_flat, hn)


def _ffn_kernel(ge_ref, gs_ref, gn_ref, ng_ref, bv_ref,
                xs_hbm, wg_ref, bg_ref, wu_ref, bu_ref, wd_ref, bd_ref,
                ys_hbm, slab_ref, acc_ref, sem_x, sem_o):
    grp = pl.program_id(0)
    f = pl.program_id(1)
    nf = pl.num_programs(1)
    tm = TM_MOE
    n_grp = ng_ref[0]

    def load(g, slot):
        row0 = pl.multiple_of(gs_ref[g] * tm, tm)
        return pltpu.make_async_copy(xs_hbm.at[pl.ds(row0, GT_MOE * tm)], slab_ref.at[slot], sem_x.at[slot])

    def store(g, r):
        row0 = pl.multiple_of((gs_ref[g] + r) * tm, tm)
        return pltpu.make_async_copy(acc_ref.at[pl.ds(r * tm, tm)], ys_hbm.at[pl.ds(row0, tm)], sem_o)

    @pl.when(grp < n_grp)
    def _():
        start = gs_ref[grp]
        nt = gn_ref[grp]
        slot = grp % 2

        @pl.when(f == 0)
        def _():
            pl.when(grp == 0)(lambda: load(0, 0).start())

            @pl.when(grp > 0)
            def _():
                for r in range(GT_MOE):
                    pl.when(r < gn_ref[grp - 1])(lambda r=r: store(grp - 1, r).wait())

            load(grp, slot).wait()
            pl.when(grp + 1 < n_grp)(lambda: load(grp + 1, 1 - slot).start())

        def row_tiles(tiles):
            wg = wg_ref[0].astype(BF16)
            wu = wu_ref[0].astype(BF16)
            wd = wd_ref[0].astype(BF16)
            row = lax.broadcasted_iota(I32, (tm, 1), 0)
            gates, ups = [], []
            for r in tiles:
                words = jnp.where(row < bv_ref[start + r], slab_ref[slot, r * tm:(r + 1) * tm, :], jnp.uint32(0))
                xb = _unpack_bf16_pairs(words)
                gates.append(jnp.dot(xb, wg, preferred_element_type=F32) + bg_ref[0])
                ups.append(jnp.dot(xb, wu, preferred_element_type=F32) + bu_ref[0])
            hidden = []
            for g, up in zip(gates, ups):
                g = jnp.minimum(g, SWIGLU_LIMIT)
                up = jnp.clip(up, -SWIGLU_LIMIT, SWIGLU_LIMIT)
                hidden.append(((up + 1.0) * (g * jax.nn.sigmoid(SWIGLU_ALPHA * g))).astype(BF16))
            for r, hmid in zip(tiles, hidden):
                rows = slice(r * tm, (r + 1) * tm)
                base = jnp.where(f == 0, jnp.broadcast_to(bd_ref[0], (tm, D_MODEL)), acc_ref[rows, :])
                acc_ref[rows, :] = base + jnp.dot(hmid, wd, preferred_element_type=F32)

            @pl.when(f == nf - 1)
            def _():
                for r in tiles:
                    store(grp, r).start()

        assert GT_MOE == 4
        pl.when(nt >= 2)(lambda: row_tiles((0, 1)))
        pl.when(nt == 1)(lambda: row_tiles((0,)))
        pl.when(nt == 4)(lambda: row_tiles((2, 3)))
        pl.when(nt == 3)(lambda: row_tiles((2,)))

        @pl.when((f == nf - 1) & (grp == n_grp - 1))
        def _():
            for r in range(GT_MOE):
                pl.when(r < nt)(lambda r=r: store(grp, r).wait())


def _expert_ffn(xs, grp_expert, grp_start, grp_ntiles, n_groups, blk_valid,
                w_gate, b_gate, w_up, b_up, w_down, b_down):
    n_rows = xs.shape[0]
    max_groups = grp_expert.shape[0]
    nf = D_FF // TF_MOE

    def expert(g, ge, ng):
        return ge[jnp.minimum(g, ng[0] - 1)]

    def ftile(g, f, ng):
        return jnp.where(g < ng[0], f, nf - 1)

    grid_spec = pltpu.PrefetchScalarGridSpec(
        num_scalar_prefetch=5,
        grid=(max_groups, nf),
        in_specs=[
            pl.BlockSpec(memory_space=pl.ANY),
            pl.BlockSpec((1, D_MODEL, TF_MOE), lambda g, f, ge, gs, gn, ng, bv: (expert(g, ge, ng), 0, ftile(g, f, ng))),
            pl.BlockSpec((1, 1, TF_MOE), lambda g, f, ge, gs, gn, ng, bv: (expert(g, ge, ng), 0, ftile(g, f, ng))),
            pl.BlockSpec((1, D_MODEL, TF_MOE), lambda g, f, ge, gs, gn, ng, bv: (expert(g, ge, ng), 0, ftile(g, f, ng))),
            pl.BlockSpec((1, 1, TF_MOE), lambda g, f, ge, gs, gn, ng, bv: (expert(g, ge, ng), 0, ftile(g, f, ng))),
            pl.BlockSpec((1, TF_MOE, D_MODEL), lambda g, f, ge, gs, gn, ng, bv: (expert(g, ge, ng), ftile(g, f, ng), 0)),
            pl.BlockSpec((1, 1, D_MODEL), lambda g, f, ge, gs, gn, ng, bv: (expert(g, ge, ng), 0, 0)),
        ],
        out_specs=pl.BlockSpec(memory_space=pl.ANY),
        scratch_shapes=[
            pltpu.VMEM((2, GT_MOE * TM_MOE, D_MODEL // 2), jnp.uint32),
            pltpu.VMEM((GT_MOE * TM_MOE, D_MODEL), F32),
            pltpu.SemaphoreType.DMA((2,)),
            pltpu.SemaphoreType.DMA(()),
        ],
    )
    return pl.pallas_call(
        _ffn_kernel,
        grid_spec=grid_spec,
        out_shape=jax.ShapeDtypeStruct((n_rows, D_MODEL), F32),
        compiler_params=pltpu.CompilerParams(
            dimension_semantics=("arbitrary", "arbitrary"), vmem_limit_bytes=VMEM_LIMIT),
        name="expert_ffn",
    )(grp_expert, grp_start, grp_ntiles, n_groups, blk_valid, xs,
      w_gate, b_gate[:, None, :], w_up, b_up[:, None, :], w_down, b_down[:, None, :])


def _combine_kernel(dest_ref, h_ref, gate_ref, ys_ref, o_ref, buf_ref, sem):
    tc = TC_COMB

    def issue(i, carry):
        for u in range(ROW_DMA_UNROLL):
            r = i * ROW_DMA_UNROLL + u
            for k in range(TOP_K):
                d = dest_ref[r * TOP_K + k]
                pltpu.make_async_copy(ys_ref.at[pl.ds(d, 1)], buf_ref.at[k, pl.ds(r, 1)], sem).start(priority=k % 2)
        return carry

    lax.fori_loop(0, tc // ROW_DMA_UNROLL, issue, 0)
    for k in range(TOP_K):
        pltpu.make_async_copy(ys_ref.at[pl.ds(0, tc)], buf_ref.at[k], sem).wait()
    gates = gate_ref[...]
    y = h_ref[...]
    for k in range(TOP_K):
        y = y + gates[:, k:k + 1] * buf_ref[k]
    o_ref[...] = y


def _combine(h, gates, dest_flat, ys, tok_off, n_out):
    tc = TC_COMB
    off = tok_off // tc
    return pl.pallas_call(
        _combine_kernel,
        grid=(n_out // tc,),
        in_specs=[
            pl.BlockSpec((tc * TOP_K,), lambda i: (i + off,), memory_space=pltpu.SMEM),
            pl.BlockSpec((tc, D_MODEL), lambda i: (i + off, 0)),
            pl.BlockSpec((tc, TOP_K), lambda i: (i + off, 0)),
            pl.BlockSpec(memory_space=pl.ANY),
        ],
        out_specs=pl.BlockSpec((tc, D_MODEL), lambda i: (i, 0)),
        out_shape=jax.ShapeDtypeStruct((n_out, D_MODEL), F32),
        scratch_shapes=[pltpu.VMEM((TOP_K, tc, D_MODEL), F32), pltpu.SemaphoreType.DMA(())],
        compiler_params=pltpu.CompilerParams(dimension_semantics=("arbitrary",), vmem_limit_bytes=VMEM_LIMIT),
        name="combine",
    )(dest_flat, h, gates, ys)


def kernel(x_prompt, x_sample, norm1_g, w_in, q_norm_g, k_norm_g, lambda_q1, lambda_k1, lambda_q2, lambda_k2,
           subln_g, rel_bias, w_pool_grp, pool_scale, w_att_out, w_pool_out, w_o, norm2_g, w_router, b_router,
           w_gate, b_gate, w_up, b_up, w_down, b_down):
    bp, sp, _ = x_prompt.shape
    bs, ss, _ = x_sample.shape
    n_p, n_s = bp * sp, bs * ss
    n_tok = n_p + n_s
    xp = x_prompt.reshape(n_p, D_MODEL)
    xs = x_sample.reshape(n_s, D_MODEL)

    proj = _in_projection(xp, xs, norm1_g[0], w_in[0].astype(BF16), q_norm_g[0], k_norm_g[0])
    band = _bias_band(rel_bias)
    lam_vecs = (lambda_q1[0], lambda_k1[0], lambda_q2[0], lambda_k2[0])
    att_p = _attention(proj, band, lam_vecs, subln_g[0], bp, sp, 0)
    att_s = _attention(proj, band, lam_vecs, subln_g[0], bs, ss, n_p)
    h, hn, logits = _merge_project(att_p, att_s, proj, xp, xs, sp, ss, w_pool_grp[0], pool_scale[0],
                                   w_att_out[0], w_pool_out[0], w_o[0], norm2_g[0], w_router[0], b_router[0])

    idx, gates, rank, counts = _route(logits)

    n_blocks = (n_tok * TOP_K + N_EXPERTS * (TM_MOE - 1)) // TM_MOE
    counts = counts[0].astype(I32)
    padded = (counts + TM_MOE - 1) // TM_MOE * TM_MOE
    pad_end = jnp.cumsum(padded)
    start_pad = pad_end - padded
    dest = (start_pad[idx] + rank).reshape(-1)
    blk_start = jnp.arange(n_blocks, dtype=I32) * TM_MOE
    blk_expert = jnp.minimum(jnp.searchsorted(pad_end, blk_start, side="right"), N_EXPERTS - 1).astype(I32)
    blk_valid = jnp.clip(counts[blk_expert] - (blk_start - start_pad[blk_expert]), 0, TM_MOE).astype(I32)
    e_blocks = padded // TM_MOE
    e_groups = (e_blocks + GT_MOE - 1) // GT_MOE
    grp_end = jnp.cumsum(e_groups)
    max_groups = n_blocks // GT_MOE + N_EXPERTS
    gidx = jnp.arange(max_groups, dtype=I32)
    grp_expert = jnp.minimum(jnp.searchsorted(grp_end, gidx, side="right"), N_EXPERTS - 1).astype(I32)
    in_expert = gidx - (grp_end - e_groups)[grp_expert]
    base = (e_blocks // jnp.maximum(e_groups, 1))[grp_expert]
    rem = e_blocks[grp_expert] - base * e_groups[grp_expert]
    grp_start = (start_pad[grp_expert] // TM_MOE + in_expert * base + jnp.minimum(in_expert, rem)).astype(I32)
    grp_ntiles = (base + (in_expert < rem)).astype(I32)
    n_groups = grp_end[-1:].astype(I32)

    n_rows = (n_blocks + GT_MOE - 1) * TM_MOE
    xs_sorted = _dispatch(hn, dest, n_rows)
    ys = _expert_ffn(xs_sorted, grp_expert, grp_start, grp_ntiles, n_groups, blk_valid,
                     w_gate[0], b_gate[0], w_up[0], b_up[0], w_down[0], b_down[0])
    y_p = _combine(h, gates, dest, ys, 0, n_p)
    y_s = _combine(h, gates, dest, ys, n_p, n_s)
    return (y_p.reshape(bp, sp, D_MODEL), y_s.reshape(bs, ss, D_MODEL))
```

```python
import functools
import math

import jax
import jax.numpy as jnp
from jax import lax
from jax.experimental import pallas as pl
from jax.experimental.pallas import tpu as pltpu

F32 = jnp.float32
BF16 = jnp.bfloat16
I32 = jnp.int32

D_MODEL = 2048
N_HEADS = 8
HEAD_DIM = 64
V_DIM = 2 * HEAD_DIM
ATT_W = N_HEADS * V_DIM
POOL_WINDOWS = (2, 4, 8, 16)
POOL_W = D_MODEL // 2
POOL_GROUP_W = POOL_W // len(POOL_WINDOWS)
IN_COLS = 3 * ATT_W + POOL_W + 2 * D_MODEL
N_BUCKETS = 32
MAX_DISTANCE = 128
N_EXPERTS = 32
TOP_K = 4
D_FF = D_MODEL
SWIGLU_LIMIT = 7.0
SWIGLU_ALPHA = 1.702
NORM_EPS = 1e-6
LAM_INIT = 0.8 - 0.6 * math.exp(-0.3 * 0)
LOG2E = math.log2(math.e)

LANES = 128
VMEM_LIMIT = 56 * 1024 * 1024

TM_IN = 512
TN_IN = 1024
T_ATT = 256
G_ATT = 4
V_AUG = V_DIM + 16
TM_OUT = 256
POOL_HALO = 16
TR_ROUTE = 512
TM_MOE = 512
TF_MOE = 256
GT_MOE = 4
TD_DISP = 256
TC_COMB = 256
ROW_DMA_UNROLL = 8


def _t5_thresholds():
    half = N_BUCKETS // 2
    max_exact = half // 2
    steps = half - max_exact
    ratio = MAX_DISTANCE // max_exact
    out = []
    for k in range(1, steps):
        n = max_exact
        while n ** steps < (max_exact ** steps) * (ratio ** k):
            n += 1
        out.append(n)
    return tuple(out)


T5_THRESHOLDS = _t5_thresholds()
T5_FAR = T5_THRESHOLDS[-1]


def _inproj_kernel(n_prompt_tiles, xp_ref, xs_ref, g1_ref, w_ref, gsum_ref, qg_ref, kg_ref, o_ref, xn_ref):
    i = pl.program_id(0)
    j = pl.program_id(1)

    @pl.when(j == 0)
    def _():
        x = jnp.where(i < n_prompt_tiles, xp_ref[...], xs_ref[...])
        ms = jnp.mean(x * x, axis=-1, keepdims=True)
        xn_ref[...] = (x * lax.rsqrt(ms + NORM_EPS) * g1_ref[...]).astype(BF16)

    acc = jnp.dot(xn_ref[...], w_ref[...], preferred_element_type=F32)

    def head_norm(g_ref, scale):
        sq = (acc * acc).astype(BF16)
        gw = gsum_ref.shape[0]
        parts = [jnp.dot(sq[:, c * gw:(c + 1) * gw], gsum_ref[...], preferred_element_type=F32)
                 for c in range(TN_IN // gw)]
        ss = jnp.concatenate(parts, axis=1)
        return acc * lax.rsqrt(ss * (1.0 / HEAD_DIM) + NORM_EPS) * (g_ref[...] * scale)

    @pl.when(j == 0)
    def _():
        o_ref[...] = head_norm(qg_ref, HEAD_DIM ** -0.5 * LOG2E).astype(BF16)

    @pl.when(j == 1)
    def _():
        o_ref[...] = head_norm(kg_ref, 1.0).astype(BF16)

    @pl.when((j == 2) | (j == 3))
    def _():
        o_ref[...] = acc.astype(BF16)

    @pl.when(j >= 4)
    def _():
        o_ref[...] = (0.5 * jnp.tanh(0.5 * acc) + 0.5).astype(BF16)


def _in_projection(xp, xs, norm1_g, w_in_bf, q_norm_g, k_norm_g):
    n_p, n_s = xp.shape[0], xs.shape[0]
    n_tok = n_p + n_s
    npt, nst = n_p // TM_IN, n_s // TM_IN
    gw = 2 * LANES
    gid = jnp.arange(gw) // HEAD_DIM
    gsum = (gid[:, None] == gid[None, :]).astype(BF16)
    qg = jnp.tile(q_norm_g.astype(F32), TN_IN // HEAD_DIM)[None, :]
    kg = jnp.tile(k_norm_g.astype(F32), TN_IN // HEAD_DIM)[None, :]
    return pl.pallas_call(
        functools.partial(_inproj_kernel, npt),
        grid=(n_tok // TM_IN, IN_COLS // TN_IN),
        in_specs=[
            pl.BlockSpec((TM_IN, D_MODEL), lambda i, j: (jnp.minimum(i, npt - 1), 0)),
            pl.BlockSpec((TM_IN, D_MODEL), lambda i, j: (jnp.maximum(i - npt, 0), 0)),
            pl.BlockSpec((1, D_MODEL), lambda i, j: (0, 0)),
            pl.BlockSpec((D_MODEL, TN_IN), lambda i, j: (0, j)),
            pl.BlockSpec((gw, gw), lambda i, j: (0, 0)),
            pl.BlockSpec((1, TN_IN), lambda i, j: (0, 0)),
            pl.BlockSpec((1, TN_IN), lambda i, j: (0, 0)),
        ],
        out_specs=pl.BlockSpec((TM_IN, TN_IN), lambda i, j: (i, j)),
        out_shape=jax.ShapeDtypeStruct((n_tok, IN_COLS), BF16),
        scratch_shapes=[pltpu.VMEM((TM_IN, D_MODEL), BF16)],
        compiler_params=pltpu.CompilerParams(
            dimension_semantics=("arbitrary", "arbitrary"), vmem_limit_bytes=VMEM_LIMIT),
        name="in_projection",
    )(xp, xs, norm1_g.astype(F32)[None, :], w_in_bf, gsum, qg, kg)


N_BAND = 5


def _band_kernel(rb_ref, o_ref):
    h = pl.program_id(0)
    t = T_ATT
    kj = lax.broadcasted_iota(I32, (t, t), 0)
    qi = lax.broadcasted_iota(I32, (t, t), 1)
    half = N_BUCKETS // 2
    max_exact = half // 2
    for d in range(N_BAND):
        rel = (d - N_BAND // 2) * t + kj - qi
        n = jnp.abs(rel)
        large = jnp.full((t, t), max_exact, I32)
        for th in T5_THRESHOLDS:
            large = large + jnp.where(n >= th, 1, 0)
        bucket = jnp.where(rel > 0, half, 0) + jnp.where(n < max_exact, n, large)
        val = jnp.zeros((t, t), F32)
        for b in range(N_BUCKETS):
            val = jnp.where(bucket == b, rb_ref[b, h], val)
        o_ref[0, d] = val * LOG2E


def _bias_band(rel_bias):
    assert T_ATT >= T5_FAR, "tiles two or more away from the diagonal must lie in the constant-bias region"
    return pl.pallas_call(
        _band_kernel,
        grid=(N_HEADS,),
        in_specs=[pl.BlockSpec(memory_space=pltpu.SMEM)],
        out_specs=pl.BlockSpec((1, N_BAND, T_ATT, T_ATT), lambda h: (h, 0, 0, 0)),
        out_shape=jax.ShapeDtypeStruct((N_HEADS, N_BAND, T_ATT, T_ATT), F32),
        compiler_params=pltpu.CompilerParams(dimension_semantics=("arbitrary",)),
        name="bias_band",
    )(rel_bias.astype(F32))


def _attn_kernel(seq, q_ref, k_ref, v_ref, band_ref, lq1_ref, lk1_ref, lq2_ref, lk2_ref, sg_ref,
                 o_ref, vt_ref, qz_ref, m_ref, acc_ref, s0_ref, s1_ref, c0_ref, c1_ref):
    qi = pl.program_id(2)
    t = T_ATT
    nk = seq // t
    heads = range(G_ATT)

    def cols(g):
        return slice(g * V_DIM, (g + 1) * V_DIM)

    @pl.when(qi == 0)
    def _():
        for g in heads:
            for c in range(nk):
                vt_ref[g, c, 0:V_DIM, :] = v_ref[c * t:(c + 1) * t, cols(g)].astype(F32).T.astype(BF16)
                vt_ref[g, c, V_DIM:V_AUG, :] = jnp.ones((V_AUG - V_DIM, t), BF16)

    for g in heads:
        q = q_ref[:, cols(g)]
        lane = lax.broadcasted_iota(I32, q.shape, 1)
        qz_ref[g, 0:t, :] = jnp.where(lane < HEAD_DIM, q, jnp.zeros_like(q))
        qz_ref[g, t:2 * t, :] = jnp.where(lane >= HEAD_DIM, q, jnp.zeros_like(q))
    m_ref[...] = jnp.full(m_ref.shape, -1e30, F32)
    acc_ref[...] = jnp.zeros(acc_ref.shape, F32)

    def score_tiles(ki, s_ref, c_ref):
        band = jnp.clip(ki - qi + N_BAND // 2, 0, N_BAND - 1)
        for g in heads:
            k_c = k_ref[pl.ds(pl.multiple_of(ki * t, t), t), cols(g)]
            bias = band_ref[g, band]
            s = lax.dot_general(k_c, qz_ref[g], (((1,), (1,)), ((), ())), preferred_element_type=F32)
            s = s + jnp.concatenate([bias, bias], axis=1)
            s_ref[g] = s
            c_ref[g] = jnp.max(s, axis=0, keepdims=True)

    def softmax_pv(ki, s_ref, c_ref):
        for g in heads:
            m_old = m_ref[g]
            m_new = jnp.maximum(m_old, c_ref[g])
            m_ref[g] = m_new
            alpha = jnp.exp2(m_old - m_new)
            p = jnp.exp2(s_ref[g] - m_new).astype(BF16)
            pv = jnp.dot(vt_ref[g, ki], p, preferred_element_type=F32)
            acc_ref[g] = acc_ref[g] * alpha + pv

    score_tiles(0, s0_ref, c0_ref)

    def pair(j, carry):
        k0 = 2 * j
        score_tiles(k0 + 1, s1_ref, c1_ref)
        softmax_pv(k0, s0_ref, c0_ref)
        score_tiles(jnp.minimum(k0 + 2, nk - 1), s0_ref, c0_ref)
        softmax_pv(k0 + 1, s1_ref, c1_ref)
        return carry

    lax.fori_loop(0, nk // 2, pair, 0)

    lam = (jnp.exp(jnp.sum(lq1_ref[...] * lk1_ref[...], axis=1, keepdims=True))
           - jnp.exp(jnp.sum(lq2_ref[...] * lk2_ref[...], axis=1, keepdims=True)) + LAM_INIT)
    for g in heads:
        acc = acc_ref[g, 0:V_DIM, :]
        l = acc_ref[g, V_DIM:V_DIM + 1, :]
        o = acc[:, 0:t] / l[:, 0:t] - lam * (acc[:, t:2 * t] / l[:, t:2 * t])
        ms = jnp.mean(o * o, axis=0, keepdims=True)
        on = o * lax.rsqrt(ms + NORM_EPS) * sg_ref[...] * (1.0 - LAM_INIT)
        o_ref[:, cols(g)] = on.T.astype(BF16)


def _attention(proj, band, lam_vecs, subln_g, batch, seq, row_off):
    t = T_ATT
    nq = seq // t
    assert nq % 2 == 0, "key tiles are processed in pairs"
    gw = G_ATT * V_DIM
    sec = ATT_W // gw
    lq1, lk1, lq2, lk2 = [v.astype(F32)[None, :] for v in lam_vecs]
    vec_spec = pl.BlockSpec((1, HEAD_DIM), lambda b, h, qi: (0, 0))
    return pl.pallas_call(
        functools.partial(_attn_kernel, seq),
        grid=(batch, N_HEADS // G_ATT, nq),
        in_specs=[
            pl.BlockSpec((t, gw), lambda b, h, qi: (row_off // t + b * nq + qi, h)),
            pl.BlockSpec((seq, gw), lambda b, h, qi: (row_off // seq + b, sec + h)),
            pl.BlockSpec((seq, gw), lambda b, h, qi: (row_off // seq + b, 2 * sec + h)),
            pl.BlockSpec((G_ATT, N_BAND, t, t), lambda b, h, qi: (h, 0, 0, 0)),
            vec_spec, vec_spec, vec_spec, vec_spec,
            pl.BlockSpec((V_DIM, 1), lambda b, h, qi: (0, 0)),
        ],
        out_specs=pl.BlockSpec((t, gw), lambda b, h, qi: (b * nq + qi, h)),
        out_shape=jax.ShapeDtypeStruct((batch * seq, ATT_W), BF16),
        scratch_shapes=[
            pltpu.VMEM((G_ATT, seq // t, V_AUG, t), BF16),
            pltpu.VMEM((G_ATT, 2 * t, V_DIM), BF16),
            pltpu.VMEM((G_ATT, 1, 2 * t), F32),
            pltpu.VMEM((G_ATT, V_AUG, 2 * t), F32),
            pltpu.VMEM((G_ATT, t, 2 * t), F32),
            pltpu.VMEM((G_ATT, t, 2 * t), F32),
            pltpu.VMEM((G_ATT, 1, 2 * t), F32),
            pltpu.VMEM((G_ATT, 1, 2 * t), F32),
        ],
        compiler_params=pltpu.CompilerParams(
            dimension_semantics=("arbitrary", "arbitrary", "arbitrary"), vmem_limit_bytes=VMEM_LIMIT),
        name=f"diff_attention_s{seq}",
    )(proj, proj, proj, band, lq1, lk1, lq2, lk2, subln_g.astype(F32)[:, None])


def _pack_bf16_pairs(x):
    n = x.shape[1] // 2
    lo = lax.bitcast_convert_type(x[:, :n].astype(F32), jnp.uint32)
    hi = lax.bitcast_convert_type(x[:, n:].astype(F32), jnp.uint32)
    return lax.shift_right_logical(lo, jnp.uint32(16)) | (hi & jnp.uint32(0xFFFF0000))


def _unpack_bf16_pairs(w):
    lo = lax.bitcast_convert_type(lax.shift_left(w, jnp.uint32(16)), F32)
    hi = lax.bitcast_convert_type(w & jnp.uint32(0xFFFF0000), F32)
    return jnp.concatenate([lo.astype(BF16), hi.astype(BF16)], axis=1)


def _merge_kernel(n_prompt_tiles, seq_p, seq_s,
                  attp_ref, atts_ref, u_ref, ul_ref, ur_ref, ga_ref, gb_ref, xp_ref, xs_ref,
                  wpg_ref, psc_ref, wao_ref, wpo_ref, wo_ref, g2_ref, wr_ref, br_ref,
                  h_ref, hn_ref, lg_ref):
    i = pl.program_id(0)
    tm = TM_OUT
    is_p = i < n_prompt_tiles
    seq = jnp.where(is_p, seq_p, seq_s)
    t0 = jnp.where(is_p, i, i - n_prompt_tiles) * tm
    pos0 = t0 % seq
    first = pos0 == 0
    last = pos0 + tm == seq

    u = u_ref[...]
    zero_halo = jnp.zeros(ul_ref.shape, BF16)
    ul = jnp.where(first, zero_halo, ul_ref[...])
    ur = jnp.where(last, zero_halo, ur_ref[...])
    uext = jnp.concatenate([ul, u, ur], axis=0)
    r = lax.broadcasted_iota(I32, (tm, tm + 2 * POOL_HALO), 0)
    c = lax.broadcasted_iota(I32, (tm, tm + 2 * POOL_HALO), 1) - POOL_HALO
    pos = pos0 + lax.broadcasted_iota(I32, (tm, 1), 0)
    mixed = []
    for gi, w in enumerate(POOL_WINDOWS):
        sl = slice(gi * POOL_GROUP_W, (gi + 1) * POOL_GROUP_W)
        band = jnp.where(c >= r - w // 2, jnp.where(c < r + (w - w // 2), 1.0, 0.0), 0.0).astype(BF16)
        wsum = jnp.dot(band, uext[:, sl], preferred_element_type=F32)
        cnt = (jnp.minimum(pos + (w - w // 2), seq) - jnp.maximum(pos - w // 2, 0)).astype(F32)
        pooled = wsum / cnt - u[:, sl].astype(F32)
        mg = jnp.dot(pooled.astype(BF16), wpg_ref[gi], preferred_element_type=F32)
        mixed.append((mg * psc_ref[:, sl]).astype(BF16))
    mixed = jnp.concatenate(mixed, axis=1)

    att = jnp.where(is_p, attp_ref[...], atts_ref[...])
    y_a = jnp.dot(att, wao_ref[...], preferred_element_type=F32)
    y_b = jnp.dot(mixed, wpo_ref[...], preferred_element_type=F32)
    merged = ga_ref[...].astype(F32) * y_a + gb_ref[...].astype(F32) * y_b
    x = jnp.where(is_p, xp_ref[...], xs_ref[...])
    h = x + jnp.dot(merged.astype(BF16), wo_ref[...], preferred_element_type=F32)
    h_ref[...] = h
    ms = jnp.mean(h * h, axis=-1, keepdims=True)
    hn = (h * lax.rsqrt(ms + NORM_EPS) * g2_ref[...]).astype(BF16)
    hn_ref[...] = _pack_bf16_pairs(hn)
    lg_ref[...] = jnp.dot(hn, wr_ref[...], preferred_element_type=F32) + br_ref[...]


def _merge_project(att_p, att_s, proj, xp, xs, seq_p, seq_s, w_pool_grp, pool_scale, w_att_out, w_pool_out,
                   w_o, norm2_g, w_router, b_router):
    tm = TM_OUT
    n_p, n_s = xp.shape[0], xs.shape[0]
    n_tok = n_p + n_s
    npt = n_p // tm
    hb = tm // POOL_HALO
    n_hblk = n_tok // POOL_HALO
    u_col = 3 * ATT_W // POOL_W
    const = dict(pipeline_mode=pl.Buffered(1))
    return pl.pallas_call(
        functools.partial(_merge_kernel, npt, seq_p, seq_s),
        grid=(n_tok // tm,),
        in_specs=[
            pl.BlockSpec((tm, ATT_W), lambda i: (jnp.minimum(i, npt - 1), 0)),
            pl.BlockSpec((tm, ATT_W), lambda i: (jnp.maximum(i - npt, 0), 0)),
            pl.BlockSpec((tm, POOL_W), lambda i: (i, u_col)),
            pl.BlockSpec((POOL_HALO, POOL_W), lambda i: (jnp.maximum(i * hb - 1, 0), u_col)),
            pl.BlockSpec((POOL_HALO, POOL_W), lambda i: (jnp.minimum((i + 1) * hb, n_hblk - 1), u_col)),
            pl.BlockSpec((tm, D_MODEL), lambda i: (i, 2)),
            pl.BlockSpec((tm, D_MODEL), lambda i: (i, 3)),
            pl.BlockSpec((tm, D_MODEL), lambda i: (jnp.minimum(i, npt - 1), 0)),
            pl.BlockSpec((tm, D_MODEL), lambda i: (jnp.maximum(i - npt, 0), 0)),
            pl.BlockSpec((len(POOL_WINDOWS), POOL_GROUP_W, POOL_GROUP_W), lambda i: (0, 0, 0), **const),
            pl.BlockSpec((1, POOL_W), lambda i: (0, 0), **const),
            pl.BlockSpec((ATT_W, D_MODEL), lambda i: (0, 0), **const),
            pl.BlockSpec((POOL_W, D_MODEL), lambda i: (0, 0), **const),
            pl.BlockSpec((D_MODEL, D_MODEL), lambda i: (0, 0), **const),
            pl.BlockSpec((1, D_MODEL), lambda i: (0, 0), **const),
            pl.BlockSpec((D_MODEL, N_EXPERTS), lambda i: (0, 0), **const),
            pl.BlockSpec((1, N_EXPERTS), lambda i: (0, 0), **const),
        ],
        out_specs=[
            pl.BlockSpec((tm, D_MODEL), lambda i: (i, 0)),
            pl.BlockSpec((tm, D_MODEL // 2), lambda i: (i, 0)),
            pl.BlockSpec((tm, N_EXPERTS), lambda i: (i, 0)),
        ],
        out_shape=[
            jax.ShapeDtypeStruct((n_tok, D_MODEL), F32),
            jax.ShapeDtypeStruct((n_tok, D_MODEL // 2), jnp.uint32),
            jax.ShapeDtypeStruct((n_tok, N_EXPERTS), F32),
        ],
        compiler_params=pltpu.CompilerParams(dimension_semantics=("arbitrary",), vmem_limit_bytes=VMEM_LIMIT),
        name="merge_project",
    )(att_p, att_s, proj, proj, proj, proj, proj, xp, xs,
      w_pool_grp.astype(BF16), pool_scale.astype(F32)[None, :], w_att_out.astype(BF16),
      w_pool_out.astype(BF16), w_o.astype(BF16), norm2_g.astype(F32)[None, :],
      w_router.astype(BF16), b_router.astype(F32)[None, :])


def _route_kernel(lg_ref, ir_ref, gate_ref, cnt_ref, run_ref):
    i = pl.program_id(0)
    tr = TR_ROUTE

    @pl.when(i == 0)
    def _():
        run_ref[...] = jnp.zeros(run_ref.shape, F32)

    cur = lg_ref[...]
    e_iota = lax.broadcasted_iota(I32, cur.shape, 1).astype(F32)
    member = jnp.zeros(cur.shape, F32)
    vals, idxs = [], []
    for _ in range(TOP_K):
        mx = jnp.max(cur, axis=1, keepdims=True)
        am = jnp.min(jnp.where(cur == mx, e_iota, float(N_EXPERTS)), axis=1, keepdims=True)
        hit = e_iota == am
        vals.append(mx)
        idxs.append(am)
        member = member + jnp.where(hit, 1.0, 0.0)
        cur = jnp.where(hit, -jnp.inf, cur)
    exps = [jnp.exp(v - vals[0]) for v in vals]
    denom = exps[0]
    for e in exps[1:]:
        denom = denom + e

    rr = lax.broadcasted_iota(I32, (tr, tr), 0)
    cc = lax.broadcasted_iota(I32, (tr, tr), 1)
    tri = jnp.where(cc < rr, 1.0, 0.0).astype(BF16)
    before = jnp.dot(tri, member.astype(BF16), preferred_element_type=F32) + run_ref[...]
    run_ref[...] = run_ref[...] + jnp.sum(member, axis=0, keepdims=True)
    cnt_ref[...] = run_ref[...]

    lane = lax.broadcasted_iota(I32, (tr, TOP_K), 1)
    wide = lax.broadcasted_iota(I32, (tr, LANES), 1)
    gate_out = jnp.zeros((tr, TOP_K), F32)
    ir = jnp.zeros((tr, LANES), F32)
    for k in range(TOP_K):
        rk = jnp.sum(jnp.where(e_iota == idxs[k], before, 0.0), axis=1, keepdims=True)
        gate_out = jnp.where(lane == k, exps[k] / denom, gate_out)
        ir = jnp.where(wide == k, idxs[k], jnp.where(wide == TOP_K + k, rk, ir))
    gate_ref[...] = gate_out
    ir_ref[...] = ir.T[0:2 * TOP_K, :].astype(I32)


def _route(logits):
    n_tok = logits.shape[0]
    tr = TR_ROUTE
    return pl.pallas_call(
        _route_kernel,
        grid=(n_tok // tr,),
        in_specs=[pl.BlockSpec((tr, N_EXPERTS), lambda i: (i, 0))],
        out_specs=[
            pl.BlockSpec((2 * TOP_K, tr), lambda i: (0, i)),
            pl.BlockSpec((tr, TOP_K), lambda i: (i, 0)),
            pl.BlockSpec((1, N_EXPERTS), lambda i: (0, 0)),
        ],
        out_shape=[
            jax.ShapeDtypeStruct((2 * TOP_K, n_tok), I32),
            jax.ShapeDtypeStruct((n_tok, TOP_K), F32),
            jax.ShapeDtypeStruct((1, N_EXPERTS), F32),
        ],
        scratch_shapes=[pltpu.VMEM((1, N_EXPERTS), F32)],
        compiler_params=pltpu.CompilerParams(dimension_semantics=("arbitrary",)),
        name="route",
    )(logits)


def _dispatch_kernel(dest_ref, hn_ref, xs_ref, sem):
    td = TD_DISP

    def issue(i, carry):
        for u in range(ROW_DMA_UNROLL):
            r = i * ROW_DMA_UNROLL + u
            for k in range(TOP_K):
                d = dest_ref[k, r]
                pltpu.make_async_copy(hn_ref.at[pl.ds(r, 1)], xs_ref.at[pl.ds(d, 1)], sem).start(priority=k % 2)
        return carry

    lax.fori_loop(0, td // ROW_DMA_UNROLL, issue, 0)
    for _ in range(TOP_K):
        pltpu.make_async_copy(hn_ref, xs_ref.at[pl.ds(0, td)], sem).wait()


def _dispatch(hn, dest_flat, n_rows):
    n_tok = hn.shape[0]
    td = TD_DISP
    return pl.pallas_call(
        _dispatch_kernel,
        grid=(n_tok // td,),
        in_specs=[
            pl.BlockSpec((TOP_K, td), lambda i: (0, i), memory_space=pltpu.SMEM),
            pl.BlockSpec((td, hn.shape[1]), lambda i: (i, 0)),
        ],
        out_specs=pl.BlockSpec(memory_space=pl.ANY),
        out_shape=jax.ShapeDtypeStruct((n_rows, hn.shape[1]), hn.dtype),
        scratch_shapes=[pltpu.SemaphoreType.DMA(())],
        compiler_params=pltpu.CompilerParams(dimension_semantics=("arbitrary",)),
        name="dispatch",
    )(dest_flat, hn)


def _ffn_kernel(ge_ref, gs_ref, gn_ref, ng_ref, bv_ref,
                xs_hbm, wg_ref, bg_ref, wu_ref, bu_ref, wd_ref, bd_ref,
                ys_hbm, slab_ref, acc_ref, sem_x, sem_o):
    grp = pl.program_id(0)
    f = pl.program_id(1)
    nf = pl.num_programs(1)
    tm = TM_MOE
    n_grp = ng_ref[0]

    def load(g, slot):
        row0 = pl.multiple_of(gs_ref[g] * tm, tm)
        return pltpu.make_async_copy(xs_hbm.at[pl.ds(row0, GT_MOE * tm)], slab_ref.at[slot], sem_x.at[slot])

    def store(g, r):
        row0 = pl.multiple_of((gs_ref[g] + r) * tm, tm)
        return pltpu.make_async_copy(acc_ref.at[pl.ds(r * tm, tm)], ys_hbm.at[pl.ds(row0, tm)], sem_o.at[r])

    @pl.when(grp < n_grp)
    def _():
        start = gs_ref[grp]
        nt = gn_ref[grp]
        slot = grp % 2

        @pl.when(f == 0)
        def _():
            pl.when(grp == 0)(lambda: load(0, 0).start())

            @pl.when(grp > 0)
            def _():
                for r in range(GT_MOE):
                    pl.when((r < gn_ref[grp - 1]) & (r >= nt))(lambda r=r: store(grp - 1, r).wait())

            load(grp, slot).wait()
            pl.when(grp + 1 < n_grp)(lambda: load(grp + 1, 1 - slot).start())

        def row_tiles(tiles):
            @pl.when((f == 0) & (grp > 0))
            def _():
                for r in tiles:
                    pl.when(r < gn_ref[grp - 1])(lambda r=r: store(grp - 1, r).wait())

            wg = wg_ref[0].astype(BF16)
            wu = wu_ref[0].astype(BF16)
            wd = wd_ref[0].astype(BF16)
            row = lax.broadcasted_iota(I32, (tm, 1), 0)
            gates, ups = [], []
            for r in tiles:
                words = jnp.where(row < bv_ref[start + r], slab_ref[slot, r * tm:(r + 1) * tm, :], jnp.uint32(0))
                xb = _unpack_bf16_pairs(words)
                gates.append(jnp.dot(xb, wg, preferred_element_type=F32) + bg_ref[0])
                ups.append(jnp.dot(xb, wu, preferred_element_type=F32) + bu_ref[0])
            hidden = []
            for g, up in zip(gates, ups):
                g = jnp.minimum(g, SWIGLU_LIMIT)
                up = jnp.clip(up, -SWIGLU_LIMIT, SWIGLU_LIMIT)
                hidden.append(((up + 1.0) * (g * jax.nn.sigmoid(SWIGLU_ALPHA * g))).astype(BF16))
            for r, hmid in zip(tiles, hidden):
                rows = slice(r * tm, (r + 1) * tm)
                base = jnp.where(f == 0, jnp.broadcast_to(bd_ref[0], (tm, D_MODEL)), acc_ref[rows, :])
                acc_ref[rows, :] = base + jnp.dot(hmid, wd, preferred_element_type=F32)

            @pl.when(f == nf - 1)
            def _():
                for r in tiles:
                    store(grp, r).start()

        assert GT_MOE == 4
        pl.when(nt >= 2)(lambda: row_tiles((0, 1)))
        pl.when(nt == 1)(lambda: row_tiles((0,)))
        pl.when(nt == 4)(lambda: row_tiles((2, 3)))
        pl.when(nt == 3)(lambda: row_tiles((2,)))

        @pl.when((f == nf - 1) & (grp == n_grp - 1))
        def _():
            for r in range(GT_MOE):
                pl.when(r < nt)(lambda r=r: store(grp, r).wait())


def _expert_ffn(xs, grp_expert, grp_start, grp_ntiles, n_groups, blk_valid,
                w_gate, b_gate, w_up, b_up, w_down, b_down):
    n_rows = xs.shape[0]
    max_groups = grp_expert.shape[0]
    nf = D_FF // TF_MOE

    def expert(g, ge, ng):
        return ge[jnp.minimum(g, ng[0] - 1)]

    def ftile(g, f, ng):
        return jnp.where(g < ng[0], f, nf - 1)

    grid_spec = pltpu.PrefetchScalarGridSpec(
        num_scalar_prefetch=5,
        grid=(max_groups, nf),
        in_specs=[
            pl.BlockSpec(memory_space=pl.ANY),
            pl.BlockSpec((1, D_MODEL, TF_MOE), lambda g, f, ge, gs, gn, ng, bv: (expert(g, ge, ng), 0, ftile(g, f, ng))),
            pl.BlockSpec((1, 1, TF_MOE), lambda g, f, ge, gs, gn, ng, bv: (expert(g, ge, ng), 0, ftile(g, f, ng))),
            pl.BlockSpec((1, D_MODEL, TF_MOE), lambda g, f, ge, gs, gn, ng, bv: (expert(g, ge, ng), 0, ftile(g, f, ng))),
            pl.BlockSpec((1, 1, TF_MOE), lambda g, f, ge, gs, gn, ng, bv: (expert(g, ge, ng), 0, ftile(g, f, ng))),
            pl.BlockSpec((1, TF_MOE, D_MODEL), lambda g, f, ge, gs, gn, ng, bv: (expert(g, ge, ng), ftile(g, f, ng), 0)),
            pl.BlockSpec((1, 1, D_MODEL), lambda g, f, ge, gs, gn, ng, bv: (expert(g, ge, ng), 0, 0)),
        ],
        out_specs=pl.BlockSpec(memory_space=pl.ANY),
        scratch_shapes=[
            pltpu.VMEM((2, GT_MOE * TM_MOE, D_MODEL // 2), jnp.uint32),
            pltpu.VMEM((GT_MOE * TM_MOE, D_MODEL), F32),
            pltpu.SemaphoreType.DMA((2,)),
            pltpu.SemaphoreType.DMA((GT_MOE,)),
        ],
    )
    return pl.pallas_call(
        _ffn_kernel,
        grid_spec=grid_spec,
        out_shape=jax.ShapeDtypeStruct((n_rows, D_MODEL), F32),
        compiler_params=pltpu.CompilerParams(
            dimension_semantics=("arbitrary", "arbitrary"), vmem_limit_bytes=VMEM_LIMIT),
        name="expert_ffn",
    )(grp_expert, grp_start, grp_ntiles, n_groups, blk_valid, xs,
      w_gate, b_gate[:, None, :], w_up, b_up[:, None, :], w_down, b_down[:, None, :])


def _combine_kernel(dest_ref, h_ref, gate_ref, ys_ref, o_ref, buf_ref, sem):
    tc = TC_COMB

    def issue(i, carry):
        for u in range(ROW_DMA_UNROLL):
            r = i * ROW_DMA_UNROLL + u
            for k in range(TOP_K):
                d = dest_ref[k, r]
                pltpu.make_async_copy(ys_ref.at[pl.ds(d, 1)], buf_ref.at[k, pl.ds(r, 1)], sem).start(priority=k % 2)
        return carry

    lax.fori_loop(0, tc // ROW_DMA_UNROLL, issue, 0)
    for k in range(TOP_K):
        pltpu.make_async_copy(ys_ref.at[pl.ds(0, tc)], buf_ref.at[k], sem).wait()
    gates = gate_ref[...]
    y = h_ref[...]
    for k in range(TOP_K):
        y = y + gates[:, k:k + 1] * buf_ref[k]
    o_ref[...] = y


def _combine(h, gates, dest_flat, ys, tok_off, n_out):
    tc = TC_COMB
    off = tok_off // tc
    return pl.pallas_call(
        _combine_kernel,
        grid=(n_out // tc,),
        in_specs=[
            pl.BlockSpec((TOP_K, tc), lambda i: (0, i + off), memory_space=pltpu.SMEM),
            pl.BlockSpec((tc, D_MODEL), lambda i: (i + off, 0)),
            pl.BlockSpec((tc, TOP_K), lambda i: (i + off, 0)),
            pl.BlockSpec(memory_space=pl.ANY),
        ],
        out_specs=pl.BlockSpec((tc, D_MODEL), lambda i: (i, 0)),
        out_shape=jax.ShapeDtypeStruct((n_out, D_MODEL), F32),
        scratch_shapes=[pltpu.VMEM((TOP_K, tc, D_MODEL), F32), pltpu.SemaphoreType.DMA(())],
        compiler_params=pltpu.CompilerParams(dimension_semantics=("arbitrary",), vmem_limit_bytes=VMEM_LIMIT),
        name="combine",
    )(dest_flat, h, gates, ys)


def kernel(x_prompt, x_sample, norm1_g, w_in, q_norm_g, k_norm_g, lambda_q1, lambda_k1, lambda_q2, lambda_k2,
           subln_g, rel_bias, w_pool_grp, pool_scale, w_att_out, w_pool_out, w_o, norm2_g, w_router, b_router,
           w_gate, b_gate, w_up, b_up, w_down, b_down):
    bp, sp, _ = x_prompt.shape
    bs, ss, _ = x_sample.shape
    n_p, n_s = bp * sp, bs * ss
    n_tok = n_p + n_s
    xp = x_prompt.reshape(n_p, D_MODEL)
    xs = x_sample.reshape(n_s, D_MODEL)

    proj = _in_projection(xp, xs, norm1_g[0], w_in[0].astype(BF16), q_norm_g[0], k_norm_g[0])
    band = _bias_band(rel_bias)
    lam_vecs = (lambda_q1[0], lambda_k1[0], lambda_q2[0], lambda_k2[0])
    att_p = _attention(proj, band, lam_vecs, subln_g[0], bp, sp, 0)
    att_s = _attention(proj, band, lam_vecs, subln_g[0], bs, ss, n_p)
    h, hn, logits = _merge_project(att_p, att_s, proj, xp, xs, sp, ss, w_pool_grp[0], pool_scale[0],
                                   w_att_out[0], w_pool_out[0], w_o[0], norm2_g[0], w_router[0], b_router[0])

    idx_rank, gates, counts = _route(logits)

    n_blocks = (n_tok * TOP_K + N_EXPERTS * (TM_MOE - 1)) // TM_MOE
    counts = counts[0].astype(I32)
    padded = (counts + TM_MOE - 1) // TM_MOE * TM_MOE
    pad_end = jnp.cumsum(padded)
    start_pad = pad_end - padded
    dest = start_pad[idx_rank[:TOP_K]] + idx_rank[TOP_K:]
    blk_start = jnp.arange(n_blocks, dtype=I32) * TM_MOE
    blk_expert = jnp.minimum(jnp.searchsorted(pad_end, blk_start, side="right"), N_EXPERTS - 1).astype(I32)
    blk_valid = jnp.clip(counts[blk_expert] - (blk_start - start_pad[blk_expert]), 0, TM_MOE).astype(I32)
    e_blocks = padded // TM_MOE
    e_groups = (e_blocks + GT_MOE - 1) // GT_MOE
    grp_end = jnp.cumsum(e_groups)
    max_groups = n_blocks // GT_MOE + N_EXPERTS
    gidx = jnp.arange(max_groups, dtype=I32)
    grp_expert = jnp.minimum(jnp.searchsorted(grp_end, gidx, side="right"), N_EXPERTS - 1).astype(I32)
    in_expert = gidx - (grp_end - e_groups)[grp_expert]
    base = (e_blocks // jnp.maximum(e_groups, 1))[grp_expert]
    rem = e_blocks[grp_expert] - base * e_groups[grp_expert]
    grp_start = (start_pad[grp_expert] // TM_MOE + in_expert * base + jnp.minimum(in_expert, rem)).astype(I32)
    grp_ntiles = (base + (in_expert < rem)).astype(I32)
    n_groups = grp_end[-1:].astype(I32)

    n_rows = (n_blocks + GT_MOE - 1) * TM_MOE
    xs_sorted = _dispatch(hn, dest, n_rows)
    ys = _expert_ffn(xs_sorted, grp_expert, grp_start, grp_ntiles, n_groups, blk_valid,
                     w_gate[0], b_gate[0], w_up[0], b_up[0], w_down[0], b_down[0])
    y_p = _combine(h, gates, dest, ys, 0, n_p)
    y_s = _combine(h, gates, dest, ys, n_p, n_s)
    return (y_p.reshape(bp, sp, D_MODEL), y_s.reshape(bs, ss, D_MODEL))
```

```python
import functools
import math

import jax
import jax.numpy as jnp
from jax import lax
from jax.experimental import pallas as pl
from jax.experimental.pallas import tpu as pltpu

F32 = jnp.float32
BF16 = jnp.bfloat16
I32 = jnp.int32

D_MODEL = 2048
N_HEADS = 8
HEAD_DIM = 64
V_DIM = 2 * HEAD_DIM
ATT_W = N_HEADS * V_DIM
POOL_WINDOWS = (2, 4, 8, 16)
POOL_W = D_MODEL // 2
POOL_GROUP_W = POOL_W // len(POOL_WINDOWS)
IN_COLS = 3 * ATT_W + POOL_W + 2 * D_MODEL
N_BUCKETS = 32
MAX_DISTANCE = 128
N_EXPERTS = 32
TOP_K = 4
D_FF = D_MODEL
SWIGLU_LIMIT = 7.0
SWIGLU_ALPHA = 1.702
NORM_EPS = 1e-6
LAM_INIT = 0.8 - 0.6 * math.exp(-0.3 * 0)
LOG2E = math.log2(math.e)

LANES = 128
VMEM_LIMIT = 56 * 1024 * 1024

TM_IN = 512
TN_IN = 1024
T_ATT = 256
G_ATT = 4
V_AUG = V_DIM + 16
TM_OUT = 256
POOL_HALO = 16
TR_ROUTE = 512
TM_MOE = 512
TF_MOE = 256
GT_MOE = 4
TD_DISP = 256
TC_COMB = 256
ROW_DMA_UNROLL = 8


def _t5_thresholds():
    half = N_BUCKETS // 2
    max_exact = half // 2
    steps = half - max_exact
    ratio = MAX_DISTANCE // max_exact
    out = []
    for k in range(1, steps):
        n = max_exact
        while n ** steps < (max_exact ** steps) * (ratio ** k):
            n += 1
        out.append(n)
    return tuple(out)


T5_THRESHOLDS = _t5_thresholds()
T5_FAR = T5_THRESHOLDS[-1]


def _inproj_kernel(n_prompt_tiles, xp_ref, xs_ref, g1_ref, w_ref, gsum_ref, qg_ref, kg_ref, o_ref, xn_ref):
    i = pl.program_id(0)
    j = pl.program_id(1)

    @pl.when(j == 0)
    def _():
        x = jnp.where(i < n_prompt_tiles, xp_ref[...], xs_ref[...])
        ms = jnp.mean(x * x, axis=-1, keepdims=True)
        xn_ref[...] = (x * lax.rsqrt(ms + NORM_EPS) * g1_ref[...]).astype(BF16)

    acc = jnp.dot(xn_ref[...], w_ref[...], preferred_element_type=F32)

    def head_norm(g_ref, scale):
        sq = (acc * acc).astype(BF16)
        gw = gsum_ref.shape[0]
        parts = [jnp.dot(sq[:, c * gw:(c + 1) * gw], gsum_ref[...], preferred_element_type=F32)
                 for c in range(TN_IN // gw)]
        ss = jnp.concatenate(parts, axis=1)
        return acc * lax.rsqrt(ss * (1.0 / HEAD_DIM) + NORM_EPS) * (g_ref[...] * scale)

    @pl.when(j == 0)
    def _():
        o_ref[...] = head_norm(qg_ref, HEAD_DIM ** -0.5 * LOG2E).astype(BF16)

    @pl.when(j == 1)
    def _():
        o_ref[...] = head_norm(kg_ref, 1.0).astype(BF16)

    @pl.when((j == 2) | (j == 3))
    def _():
        o_ref[...] = acc.astype(BF16)

    @pl.when(j >= 4)
    def _():
        o_ref[...] = (0.5 * jnp.tanh(0.5 * acc) + 0.5).astype(BF16)


def _in_projection(xp, xs, norm1_g, w_in_bf, q_norm_g, k_norm_g):
    n_p, n_s = xp.shape[0], xs.shape[0]
    n_tok = n_p + n_s
    npt, nst = n_p // TM_IN, n_s // TM_IN
    gw = 2 * LANES
    gid = jnp.arange(gw) // HEAD_DIM
    gsum = (gid[:, None] == gid[None, :]).astype(BF16)
    qg = jnp.tile(q_norm_g.astype(F32), TN_IN // HEAD_DIM)[None, :]
    kg = jnp.tile(k_norm_g.astype(F32), TN_IN // HEAD_DIM)[None, :]
    return pl.pallas_call(
        functools.partial(_inproj_kernel, npt),
        grid=(n_tok // TM_IN, IN_COLS // TN_IN),
        in_specs=[
            pl.BlockSpec((TM_IN, D_MODEL), lambda i, j: (jnp.minimum(i, npt - 1), 0)),
            pl.BlockSpec((TM_IN, D_MODEL), lambda i, j: (jnp.maximum(i - npt, 0), 0)),
            pl.BlockSpec((1, D_MODEL), lambda i, j: (0, 0)),
            pl.BlockSpec((D_MODEL, TN_IN), lambda i, j: (0, j)),
            pl.BlockSpec((gw, gw), lambda i, j: (0, 0)),
            pl.BlockSpec((1, TN_IN), lambda i, j: (0, 0)),
            pl.BlockSpec((1, TN_IN), lambda i, j: (0, 0)),
        ],
        out_specs=pl.BlockSpec((TM_IN, TN_IN), lambda i, j: (i, j)),
        out_shape=jax.ShapeDtypeStruct((n_tok, IN_COLS), BF16),
        scratch_shapes=[pltpu.VMEM((TM_IN, D_MODEL), BF16)],
        compiler_params=pltpu.CompilerParams(
            dimension_semantics=("arbitrary", "arbitrary"), vmem_limit_bytes=VMEM_LIMIT),
        name="in_projection",
    )(xp, xs, norm1_g.astype(F32)[None, :], w_in_bf, gsum, qg, kg)


N_BAND = 5


def _band_kernel(rb_ref, o_ref):
    h = pl.program_id(0)
    t = T_ATT
    kj = lax.broadcasted_iota(I32, (t, t), 0)
    qi = lax.broadcasted_iota(I32, (t, t), 1)
    half = N_BUCKETS // 2
    max_exact = half // 2
    for d in range(N_BAND):
        rel = (d - N_BAND // 2) * t + kj - qi
        n = jnp.abs(rel)
        large = jnp.full((t, t), max_exact, I32)
        for th in T5_THRESHOLDS:
            large = large + jnp.where(n >= th, 1, 0)
        bucket = jnp.where(rel > 0, half, 0) + jnp.where(n < max_exact, n, large)
        val = jnp.zeros((t, t), F32)
        for b in range(N_BUCKETS):
            val = jnp.where(bucket == b, rb_ref[b, h], val)
        o_ref[0, d] = val * LOG2E


def _bias_band(rel_bias):
    assert T_ATT >= T5_FAR, "tiles two or more away from the diagonal must lie in the constant-bias region"
    return pl.pallas_call(
        _band_kernel,
        grid=(N_HEADS,),
        in_specs=[pl.BlockSpec(memory_space=pltpu.SMEM)],
        out_specs=pl.BlockSpec((1, N_BAND, T_ATT, T_ATT), lambda h: (h, 0, 0, 0)),
        out_shape=jax.ShapeDtypeStruct((N_HEADS, N_BAND, T_ATT, T_ATT), F32),
        compiler_params=pltpu.CompilerParams(dimension_semantics=("arbitrary",)),
        name="bias_band",
    )(rel_bias.astype(F32))


def _attn_kernel(seq, q_ref, k_ref, v_ref, band_ref, lq1_ref, lk1_ref, lq2_ref, lk2_ref, sg_ref,
                 o_ref, vt_ref, qz_ref, m_ref, acc_ref, s0_ref, s1_ref, c0_ref, c1_ref):
    qi = pl.program_id(2)
    t = T_ATT
    nk = seq // t
    heads = range(G_ATT)

    def cols(g):
        return slice(g * V_DIM, (g + 1) * V_DIM)

    @pl.when(qi == 0)
    def _():
        for g in heads:
            for c in range(nk):
                vt_ref[g, c, 0:V_DIM, :] = v_ref[c * t:(c + 1) * t, cols(g)].astype(F32).T.astype(BF16)
                vt_ref[g, c, V_DIM:V_AUG, :] = jnp.ones((V_AUG - V_DIM, t), BF16)

    for g in heads:
        q = q_ref[:, cols(g)]
        lane = lax.broadcasted_iota(I32, q.shape, 1)
        qz_ref[g, 0:t, :] = jnp.where(lane < HEAD_DIM, q, jnp.zeros_like(q))
        qz_ref[g, t:2 * t, :] = jnp.where(lane >= HEAD_DIM, q, jnp.zeros_like(q))
    m_ref[...] = jnp.full(m_ref.shape, -1e30, F32)
    acc_ref[...] = jnp.zeros(acc_ref.shape, F32)

    def score_tiles(ki, s_ref, c_ref):
        band = jnp.clip(ki - qi + N_BAND // 2, 0, N_BAND - 1)
        for g in heads:
            k_c = k_ref[pl.ds(pl.multiple_of(ki * t, t), t), cols(g)]
            bias = band_ref[g, band]
            s = lax.dot_general(k_c, qz_ref[g], (((1,), (1,)), ((), ())), preferred_element_type=F32)
            s = s + jnp.concatenate([bias, bias], axis=1)
            s_ref[g] = s
            c_ref[g] = jnp.max(s, axis=0, keepdims=True)

    def softmax_pv(ki, s_ref, c_ref):
        for g in heads:
            m_old = m_ref[g]
            m_new = jnp.maximum(m_old, c_ref[g])
            m_ref[g] = m_new
            alpha = jnp.exp2(m_old - m_new)
            p = jnp.exp2(s_ref[g] - m_new).astype(BF16)
            pv = jnp.dot(vt_ref[g, ki], p, preferred_element_type=F32)
            acc_ref[g] = acc_ref[g] * alpha + pv

    score_tiles(0, s0_ref, c0_ref)

    def pair(j, carry):
        k0 = 2 * j
        score_tiles(k0 + 1, s1_ref, c1_ref)
        softmax_pv(k0, s0_ref, c0_ref)
        score_tiles(jnp.minimum(k0 + 2, nk - 1), s0_ref, c0_ref)
        softmax_pv(k0 + 1, s1_ref, c1_ref)
        return carry

    lax.fori_loop(0, nk // 2, pair, 0)

    lam = (jnp.exp(jnp.sum(lq1_ref[...] * lk1_ref[...], axis=1, keepdims=True))
           - jnp.exp(jnp.sum(lq2_ref[...] * lk2_ref[...], axis=1, keepdims=True)) + LAM_INIT)
    for g in heads:
        acc = acc_ref[g, 0:V_DIM, :]
        l = acc_ref[g, V_DIM:V_DIM + 1, :]
        o = acc[:, 0:t] / l[:, 0:t] - lam * (acc[:, t:2 * t] / l[:, t:2 * t])
        ms = jnp.mean(o * o, axis=0, keepdims=True)
        on = o * lax.rsqrt(ms + NORM_EPS) * sg_ref[...] * (1.0 - LAM_INIT)
        o_ref[:, cols(g)] = on.T.astype(BF16)


def _attention(proj, band, lam_vecs, subln_g, batch, seq, row_off):
    t = T_ATT
    nq = seq // t
    assert nq % 2 == 0, "key tiles are processed in pairs"
    gw = G_ATT * V_DIM
    sec = ATT_W // gw
    lq1, lk1, lq2, lk2 = [v.astype(F32)[None, :] for v in lam_vecs]
    vec_spec = pl.BlockSpec((1, HEAD_DIM), lambda b, h, qi: (0, 0))
    return pl.pallas_call(
        functools.partial(_attn_kernel, seq),
        grid=(batch, N_HEADS // G_ATT, nq),
        in_specs=[
            pl.BlockSpec((t, gw), lambda b, h, qi: (row_off // t + b * nq + qi, h)),
            pl.BlockSpec((seq, gw), lambda b, h, qi: (row_off // seq + b, sec + h)),
            pl.BlockSpec((seq, gw), lambda b, h, qi: (row_off // seq + b, 2 * sec + h)),
            pl.BlockSpec((G_ATT, N_BAND, t, t), lambda b, h, qi: (h, 0, 0, 0)),
            vec_spec, vec_spec, vec_spec, vec_spec,
            pl.BlockSpec((V_DIM, 1), lambda b, h, qi: (0, 0)),
        ],
        out_specs=pl.BlockSpec((t, gw), lambda b, h, qi: (b * nq + qi, h)),
        out_shape=jax.ShapeDtypeStruct((batch * seq, ATT_W), BF16),
        scratch_shapes=[
            pltpu.VMEM((G_ATT, seq // t, V_AUG, t), BF16),
            pltpu.VMEM((G_ATT, 2 * t, V_DIM), BF16),
            pltpu.VMEM((G_ATT, 1, 2 * t), F32),
            pltpu.VMEM((G_ATT, V_AUG, 2 * t), F32),
            pltpu.VMEM((G_ATT, t, 2 * t), F32),
            pltpu.VMEM((G_ATT, t, 2 * t), F32),
            pltpu.VMEM((G_ATT, 1, 2 * t), F32),
            pltpu.VMEM((G_ATT, 1, 2 * t), F32),
        ],
        compiler_params=pltpu.CompilerParams(
            dimension_semantics=("arbitrary", "arbitrary", "arbitrary"), vmem_limit_bytes=VMEM_LIMIT),
        name=f"diff_attention_s{seq}",
    )(proj, proj, proj, band, lq1, lk1, lq2, lk2, subln_g.astype(F32)[:, None])


def _pack_bf16_pairs(x):
    n = x.shape[1] // 2
    lo = lax.bitcast_convert_type(x[:, :n].astype(F32), jnp.uint32)
    hi = lax.bitcast_convert_type(x[:, n:].astype(F32), jnp.uint32)
    return lax.shift_right_logical(lo, jnp.uint32(16)) | (hi & jnp.uint32(0xFFFF0000))


def _unpack_bf16_pairs(w):
    lo = lax.bitcast_convert_type(lax.shift_left(w, jnp.uint32(16)), F32)
    hi = lax.bitcast_convert_type(w & jnp.uint32(0xFFFF0000), F32)
    return jnp.concatenate([lo.astype(BF16), hi.astype(BF16)], axis=1)


def _merge_kernel(n_prompt_tiles, seq_p, seq_s,
                  attp_ref, atts_ref, u_ref, ul_ref, ur_ref, ga_ref, gb_ref, xp_ref, xs_ref,
                  wpg_ref, psc_ref, wao_ref, wpo_ref, wo_ref, g2_ref, wr_ref, br_ref,
                  h_ref, hn_ref, lg_ref):
    i = pl.program_id(0)
    tm = TM_OUT
    is_p = i < n_prompt_tiles
    seq = jnp.where(is_p, seq_p, seq_s)
    t0 = jnp.where(is_p, i, i - n_prompt_tiles) * tm
    pos0 = t0 % seq
    first = pos0 == 0
    last = pos0 + tm == seq

    u = u_ref[...]
    zero_halo = jnp.zeros(ul_ref.shape, BF16)
    ul = jnp.where(first, zero_halo, ul_ref[...])
    ur = jnp.where(last, zero_halo, ur_ref[...])
    uext = jnp.concatenate([ul, u, ur], axis=0)
    r = lax.broadcasted_iota(I32, (tm, tm + 2 * POOL_HALO), 0)
    c = lax.broadcasted_iota(I32, (tm, tm + 2 * POOL_HALO), 1) - POOL_HALO
    pos = pos0 + lax.broadcasted_iota(I32, (tm, 1), 0)
    mixed = []
    for gi, w in enumerate(POOL_WINDOWS):
        sl = slice(gi * POOL_GROUP_W, (gi + 1) * POOL_GROUP_W)
        band = jnp.where(c >= r - w // 2, jnp.where(c < r + (w - w // 2), 1.0, 0.0), 0.0).astype(BF16)
        wsum = jnp.dot(band, uext[:, sl], preferred_element_type=F32)
        cnt = (jnp.minimum(pos + (w - w // 2), seq) - jnp.maximum(pos - w // 2, 0)).astype(F32)
        pooled = wsum / cnt - u[:, sl].astype(F32)
        mg = jnp.dot(pooled.astype(BF16), wpg_ref[gi], preferred_element_type=F32)
        mixed.append((mg * psc_ref[:, sl]).astype(BF16))
    mixed = jnp.concatenate(mixed, axis=1)

    att = jnp.where(is_p, attp_ref[...], atts_ref[...])
    y_a = jnp.dot(att, wao_ref[...], preferred_element_type=F32)
    y_b = jnp.dot(mixed, wpo_ref[...], preferred_element_type=F32)
    merged = ga_ref[...].astype(F32) * y_a + gb_ref[...].astype(F32) * y_b
    x = jnp.where(is_p, xp_ref[...], xs_ref[...])
    h = x + jnp.dot(merged.astype(BF16), wo_ref[...], preferred_element_type=F32)
    h_ref[...] = h
    ms = jnp.mean(h * h, axis=-1, keepdims=True)
    hn = (h * lax.rsqrt(ms + NORM_EPS) * g2_ref[...]).astype(BF16)
    hn_ref[...] = _pack_bf16_pairs(hn)
    lg_ref[...] = jnp.dot(hn, wr_ref[...], preferred_element_type=F32) + br_ref[...]


def _merge_project(att_p, att_s, proj, xp, xs, seq_p, seq_s, w_pool_grp, pool_scale, w_att_out, w_pool_out,
                   w_o, norm2_g, w_router, b_router):
    tm = TM_OUT
    n_p, n_s = xp.shape[0], xs.shape[0]
    n_tok = n_p + n_s
    npt = n_p // tm
    hb = tm // POOL_HALO
    n_hblk = n_tok // POOL_HALO
    u_col = 3 * ATT_W // POOL_W
    const = dict(pipeline_mode=pl.Buffered(1))
    return pl.pallas_call(
        functools.partial(_merge_kernel, npt, seq_p, seq_s),
        grid=(n_tok // tm,),
        in_specs=[
            pl.BlockSpec((tm, ATT_W), lambda i: (jnp.minimum(i, npt - 1), 0)),
            pl.BlockSpec((tm, ATT_W), lambda i: (jnp.maximum(i - npt, 0), 0)),
            pl.BlockSpec((tm, POOL_W), lambda i: (i, u_col)),
            pl.BlockSpec((POOL_HALO, POOL_W), lambda i: (jnp.maximum(i * hb - 1, 0), u_col)),
            pl.BlockSpec((POOL_HALO, POOL_W), lambda i: (jnp.minimum((i + 1) * hb, n_hblk - 1), u_col)),
            pl.BlockSpec((tm, D_MODEL), lambda i: (i, 2)),
            pl.BlockSpec((tm, D_MODEL), lambda i: (i, 3)),
            pl.BlockSpec((tm, D_MODEL), lambda i: (jnp.minimum(i, npt - 1), 0)),
            pl.BlockSpec((tm, D_MODEL), lambda i: (jnp.maximum(i - npt, 0), 0)),
            pl.BlockSpec((len(POOL_WINDOWS), POOL_GROUP_W, POOL_GROUP_W), lambda i: (0, 0, 0), **const),
            pl.BlockSpec((1, POOL_W), lambda i: (0, 0), **const),
            pl.BlockSpec((ATT_W, D_MODEL), lambda i: (0, 0), **const),
            pl.BlockSpec((POOL_W, D_MODEL), lambda i: (0, 0), **const),
            pl.BlockSpec((D_MODEL, D_MODEL), lambda i: (0, 0), **const),
            pl.BlockSpec((1, D_MODEL), lambda i: (0, 0), **const),
            pl.BlockSpec((D_MODEL, N_EXPERTS), lambda i: (0, 0), **const),
            pl.BlockSpec((1, N_EXPERTS), lambda i: (0, 0), **const),
        ],
        out_specs=[
            pl.BlockSpec((tm, D_MODEL), lambda i: (i, 0)),
            pl.BlockSpec((tm, D_MODEL // 2), lambda i: (i, 0)),
            pl.BlockSpec((tm, N_EXPERTS), lambda i: (i, 0)),
        ],
        out_shape=[
            jax.ShapeDtypeStruct((n_tok, D_MODEL), F32),
            jax.ShapeDtypeStruct((n_tok, D_MODEL // 2), jnp.uint32),
            jax.ShapeDtypeStruct((n_tok, N_EXPERTS), F32),
        ],
        compiler_params=pltpu.CompilerParams(dimension_semantics=("arbitrary",), vmem_limit_bytes=VMEM_LIMIT),
        name="merge_project",
    )(att_p, att_s, proj, proj, proj, proj, proj, xp, xs,
      w_pool_grp.astype(BF16), pool_scale.astype(F32)[None, :], w_att_out.astype(BF16),
      w_pool_out.astype(BF16), w_o.astype(BF16), norm2_g.astype(F32)[None, :],
      w_router.astype(BF16), b_router.astype(F32)[None, :])


def _route_kernel(lg_ref, ir_ref, gate_ref, cnt_ref, run_ref):
    i = pl.program_id(0)
    tr = TR_ROUTE

    @pl.when(i == 0)
    def _():
        run_ref[...] = jnp.zeros(run_ref.shape, F32)

    cur = lg_ref[...]
    e_iota = lax.broadcasted_iota(I32, cur.shape, 1).astype(F32)
    member = jnp.zeros(cur.shape, F32)
    vals, idxs = [], []
    for _ in range(TOP_K):
        mx = jnp.max(cur, axis=1, keepdims=True)
        am = jnp.min(jnp.where(cur == mx, e_iota, float(N_EXPERTS)), axis=1, keepdims=True)
        hit = e_iota == am
        vals.append(mx)
        idxs.append(am)
        member = member + jnp.where(hit, 1.0, 0.0)
        cur = jnp.where(hit, -jnp.inf, cur)
    exps = [jnp.exp(v - vals[0]) for v in vals]
    denom = exps[0]
    for e in exps[1:]:
        denom = denom + e

    rr = lax.broadcasted_iota(I32, (tr, tr), 0)
    cc = lax.broadcasted_iota(I32, (tr, tr), 1)
    tri = jnp.where(cc < rr, 1.0, 0.0).astype(BF16)
    before = jnp.dot(tri, member.astype(BF16), preferred_element_type=F32) + run_ref[...]
    run_ref[...] = run_ref[...] + jnp.sum(member, axis=0, keepdims=True)
    cnt_ref[...] = run_ref[...]

    lane = lax.broadcasted_iota(I32, (tr, TOP_K), 1)
    wide = lax.broadcasted_iota(I32, (tr, LANES), 1)
    gate_out = jnp.zeros((tr, TOP_K), F32)
    ir = jnp.zeros((tr, LANES), F32)
    for k in range(TOP_K):
        rk = jnp.sum(jnp.where(e_iota == idxs[k], before, 0.0), axis=1, keepdims=True)
        gate_out = jnp.where(lane == k, exps[k] / denom, gate_out)
        ir = jnp.where(wide == k, idxs[k], jnp.where(wide == TOP_K + k, rk, ir))
    gate_ref[...] = gate_out
    ir_ref[...] = ir.T[0:2 * TOP_K, :].astype(I32)


def _route(logits):
    n_tok = logits.shape[0]
    tr = TR_ROUTE
    return pl.pallas_call(
        _route_kernel,
        grid=(n_tok // tr,),
        in_specs=[pl.BlockSpec((tr, N_EXPERTS), lambda i: (i, 0))],
        out_specs=[
            pl.BlockSpec((2 * TOP_K, tr), lambda i: (0, i)),
            pl.BlockSpec((tr, TOP_K), lambda i: (i, 0)),
            pl.BlockSpec((1, N_EXPERTS), lambda i: (0, 0)),
        ],
        out_shape=[
            jax.ShapeDtypeStruct((2 * TOP_K, n_tok), I32),
            jax.ShapeDtypeStruct((n_tok, TOP_K), F32),
            jax.ShapeDtypeStruct((1, N_EXPERTS), F32),
        ],
        scratch_shapes=[pltpu.VMEM((1, N_EXPERTS), F32)],
        compiler_params=pltpu.CompilerParams(dimension_semantics=("arbitrary",)),
        name="route",
    )(logits)


def _dispatch_kernel(*refs):
    dest_refs, (hn_ref, xs_ref, sem) = refs[:TOP_K], refs[TOP_K:]
    td = TD_DISP

    def issue(i, carry):
        for u in range(ROW_DMA_UNROLL):
            r = i * ROW_DMA_UNROLL + u
            for k in range(TOP_K):
                d = dest_refs[k][r]
                pltpu.make_async_copy(hn_ref.at[pl.ds(r, 1)], xs_ref.at[pl.ds(d, 1)], sem).start(priority=k % 2)
        return carry

    lax.fori_loop(0, td // ROW_DMA_UNROLL, issue, 0)
    for _ in range(TOP_K):
        pltpu.make_async_copy(hn_ref, xs_ref.at[pl.ds(0, td)], sem).wait()


def _dispatch(hn, dest_flat, n_rows):
    n_tok = hn.shape[0]
    td = TD_DISP
    return pl.pallas_call(
        _dispatch_kernel,
        grid=(n_tok // td,),
        in_specs=[pl.BlockSpec((td,), lambda i, k=k: (k * (n_tok // td) + i,), memory_space=pltpu.SMEM)
                  for k in range(TOP_K)] + [
            pl.BlockSpec((td, hn.shape[1]), lambda i: (i, 0)),
        ],
        out_specs=pl.BlockSpec(memory_space=pl.ANY),
        out_shape=jax.ShapeDtypeStruct((n_rows, hn.shape[1]), hn.dtype),
        scratch_shapes=[pltpu.SemaphoreType.DMA(())],
        compiler_params=pltpu.CompilerParams(dimension_semantics=("arbitrary",)),
        name="dispatch",
    )(*([dest_flat] * TOP_K), hn)


def _ffn_kernel(ge_ref, gs_ref, gn_ref, ng_ref, bv_ref,
                xs_hbm, wg_ref, bg_ref, wu_ref, bu_ref, wd_ref, bd_ref,
                ys_hbm, slab_ref, acc_ref, sem_x, sem_o):
    grp = pl.program_id(0)
    f = pl.program_id(1)
    nf = pl.num_programs(1)
    tm = TM_MOE
    n_grp = ng_ref[0]

    def load(g, slot):
        row0 = pl.multiple_of(gs_ref[g] * tm, tm)
        return pltpu.make_async_copy(xs_hbm.at[pl.ds(row0, GT_MOE * tm)], slab_ref.at[slot], sem_x.at[slot])

    def store(g, r):
        row0 = pl.multiple_of((gs_ref[g] + r) * tm, tm)
        return pltpu.make_async_copy(acc_ref.at[pl.ds(r * tm, tm)], ys_hbm.at[pl.ds(row0, tm)], sem_o.at[r])

    @pl.when(grp < n_grp)
    def _():
        start = gs_ref[grp]
        nt = gn_ref[grp]
        slot = grp % 2

        @pl.when(f == 0)
        def _():
            pl.when(grp == 0)(lambda: load(0, 0).start())

            @pl.when(grp > 0)
            def _():
                for r in range(GT_MOE):
                    pl.when((r < gn_ref[grp - 1]) & (r >= nt))(lambda r=r: store(grp - 1, r).wait())

            load(grp, slot).wait()
            pl.when(grp + 1 < n_grp)(lambda: load(grp + 1, 1 - slot).start())

        def row_tiles(tiles):
            @pl.when((f == 0) & (grp > 0))
            def _():
                for r in tiles:
                    pl.when(r < gn_ref[grp - 1])(lambda r=r: store(grp - 1, r).wait())

            wg = wg_ref[0].astype(BF16)
            wu = wu_ref[0].astype(BF16)
            wd = wd_ref[0].astype(BF16)
            row = lax.broadcasted_iota(I32, (tm, 1), 0)
            gates, ups = [], []
            for r in tiles:
                words = jnp.where(row < bv_ref[start + r], slab_ref[slot, r * tm:(r + 1) * tm, :], jnp.uint32(0))
                xb = _unpack_bf16_pairs(words)
                gates.append(jnp.dot(xb, wg, preferred_element_type=F32) + bg_ref[0])
                ups.append(jnp.dot(xb, wu, preferred_element_type=F32) + bu_ref[0])
            hidden = []
            for g, up in zip(gates, ups):
                g = jnp.minimum(g, SWIGLU_LIMIT)
                up = jnp.clip(up, -SWIGLU_LIMIT, SWIGLU_LIMIT)
                hidden.append(((up + 1.0) * (g * jax.nn.sigmoid(SWIGLU_ALPHA * g))).astype(BF16))
            for r, hmid in zip(tiles, hidden):
                rows = slice(r * tm, (r + 1) * tm)
                base = jnp.where(f == 0, jnp.broadcast_to(bd_ref[0], (tm, D_MODEL)), acc_ref[rows, :])
                acc_ref[rows, :] = base + jnp.dot(hmid, wd, preferred_element_type=F32)

            @pl.when(f == nf - 1)
            def _():
                for r in tiles:
                    store(grp, r).start()

        assert GT_MOE == 4
        pl.when(nt >= 2)(lambda: row_tiles((0, 1)))
        pl.when(nt == 1)(lambda: row_tiles((0,)))
        pl.when(nt == 4)(lambda: row_tiles((2, 3)))
        pl.when(nt == 3)(lambda: row_tiles((2,)))

        @pl.when((f == nf - 1) & (grp == n_grp - 1))
        def _():
            for r in range(GT_MOE):
                pl.when(r < nt)(lambda r=r: store(grp, r).wait())


def _expert_ffn(xs, grp_expert, grp_start, grp_ntiles, n_groups, blk_valid,
                w_gate, b_gate, w_up, b_up, w_down, b_down):
    n_rows = xs.shape[0]
    max_groups = grp_expert.shape[0]
    nf = D_FF // TF_MOE

    def expert(g, ge, ng):
        return ge[jnp.minimum(g, ng[0] - 1)]

    def ftile(g, f, ng):
        return jnp.where(g < ng[0], f, nf - 1)

    grid_spec = pltpu.PrefetchScalarGridSpec(
        num_scalar_prefetch=5,
        grid=(max_groups, nf),
        in_specs=[
            pl.BlockSpec(memory_space=pl.ANY),
            pl.BlockSpec((1, D_MODEL, TF_MOE), lambda g, f, ge, gs, gn, ng, bv: (expert(g, ge, ng), 0, ftile(g, f, ng))),
            pl.BlockSpec((1, 1, TF_MOE), lambda g, f, ge, gs, gn, ng, bv: (expert(g, ge, ng), 0, ftile(g, f, ng))),
            pl.BlockSpec((1, D_MODEL, TF_MOE), lambda g, f, ge, gs, gn, ng, bv: (expert(g, ge, ng), 0, ftile(g, f, ng))),
            pl.BlockSpec((1, 1, TF_MOE), lambda g, f, ge, gs, gn, ng, bv: (expert(g, ge, ng), 0, ftile(g, f, ng))),
            pl.BlockSpec((1, TF_MOE, D_MODEL), lambda g, f, ge, gs, gn, ng, bv: (expert(g, ge, ng), ftile(g, f, ng), 0)),
            pl.BlockSpec((1, 1, D_MODEL), lambda g, f, ge, gs, gn, ng, bv: (expert(g, ge, ng), 0, 0)),
        ],
        out_specs=pl.BlockSpec(memory_space=pl.ANY),
        scratch_shapes=[
            pltpu.VMEM((2, GT_MOE * TM_MOE, D_MODEL // 2), jnp.uint32),
            pltpu.VMEM((GT_MOE * TM_MOE, D_MODEL), F32),
            pltpu.SemaphoreType.DMA((2,)),
            pltpu.SemaphoreType.DMA((GT_MOE,)),
        ],
    )
    return pl.pallas_call(
        _ffn_kernel,
        grid_spec=grid_spec,
        out_shape=jax.ShapeDtypeStruct((n_rows, D_MODEL), F32),
        compiler_params=pltpu.CompilerParams(
            dimension_semantics=("arbitrary", "arbitrary"), vmem_limit_bytes=VMEM_LIMIT),
        name="expert_ffn",
    )(grp_expert, grp_start, grp_ntiles, n_groups, blk_valid, xs,
      w_gate, b_gate[:, None, :], w_up, b_up[:, None, :], w_down, b_down[:, None, :])


def _combine_kernel(*refs):
    dest_refs, (h_ref, gate_ref, ys_ref, o_ref, buf_ref, sem) = refs[:TOP_K], refs[TOP_K:]
    tc = TC_COMB

    def issue(i, carry):
        for u in range(ROW_DMA_UNROLL):
            r = i * ROW_DMA_UNROLL + u
            for k in range(TOP_K):
                d = dest_refs[k][r]
                pltpu.make_async_copy(ys_ref.at[pl.ds(d, 1)], buf_ref.at[k, pl.ds(r, 1)], sem).start(priority=k % 2)
        return carry

    lax.fori_loop(0, tc // ROW_DMA_UNROLL, issue, 0)
    for k in range(TOP_K):
        pltpu.make_async_copy(ys_ref.at[pl.ds(0, tc)], buf_ref.at[k], sem).wait()
    gates = gate_ref[...]
    y = h_ref[...]
    for k in range(TOP_K):
        y = y + gates[:, k:k + 1] * buf_ref[k]
    o_ref[...] = y


def _combine(h, gates, dest_flat, ys, tok_off, n_out):
    tc = TC_COMB
    off = tok_off // tc
    n_all = h.shape[0]
    return pl.pallas_call(
        _combine_kernel,
        grid=(n_out // tc,),
        in_specs=[pl.BlockSpec((tc,), lambda i, k=k: (k * (n_all // tc) + i + off,), memory_space=pltpu.SMEM)
                  for k in range(TOP_K)] + [
            pl.BlockSpec((tc, D_MODEL), lambda i: (i + off, 0)),
            pl.BlockSpec((tc, TOP_K), lambda i: (i + off, 0)),
            pl.BlockSpec(memory_space=pl.ANY),
        ],
        out_specs=pl.BlockSpec((tc, D_MODEL), lambda i: (i, 0)),
        out_shape=jax.ShapeDtypeStruct((n_out, D_MODEL), F32),
        scratch_shapes=[pltpu.VMEM((TOP_K, tc, D_MODEL), F32), pltpu.SemaphoreType.DMA(())],
        compiler_params=pltpu.CompilerParams(dimension_semantics=("arbitrary",), vmem_limit_bytes=VMEM_LIMIT),
        name="combine",
    )(*([dest_flat] * TOP_K), h, gates, ys)


def kernel(x_prompt, x_sample, norm1_g, w_in, q_norm_g, k_norm_g, lambda_q1, lambda_k1, lambda_q2, lambda_k2,
           subln_g, rel_bias, w_pool_grp, pool_scale, w_att_out, w_pool_out, w_o, norm2_g, w_router, b_router,
           w_gate, b_gate, w_up, b_up, w_down, b_down):
    bp, sp, _ = x_prompt.shape
    bs, ss, _ = x_sample.shape
    n_p, n_s = bp * sp, bs * ss
    n_tok = n_p + n_s
    xp = x_prompt.reshape(n_p, D_MODEL)
    xs = x_sample.reshape(n_s, D_MODEL)

    proj = _in_projection(xp, xs, norm1_g[0], w_in[0].astype(BF16), q_norm_g[0], k_norm_g[0])
    band = _bias_band(rel_bias)
    lam_vecs = (lambda_q1[0], lambda_k1[0], lambda_q2[0], lambda_k2[0])
    att_p = _attention(proj, band, lam_vecs, subln_g[0], bp, sp, 0)
    att_s = _attention(proj, band, lam_vecs, subln_g[0], bs, ss, n_p)
    h, hn, logits = _merge_project(att_p, att_s, proj, xp, xs, sp, ss, w_pool_grp[0], pool_scale[0],
                                   w_att_out[0], w_pool_out[0], w_o[0], norm2_g[0], w_router[0], b_router[0])

    idx_rank, gates, counts = _route(logits)

    n_blocks = (n_tok * TOP_K + N_EXPERTS * (TM_MOE - 1)) // TM_MOE
    counts = counts[0].astype(I32)
    padded = (counts + TM_MOE - 1) // TM_MOE * TM_MOE
    pad_end = jnp.cumsum(padded)
    start_pad = pad_end - padded
    e_ids = jnp.arange(N_EXPERTS, dtype=I32)[:, None, None]
    dest = (jnp.sum(jnp.where(idx_rank[None, :TOP_K] == e_ids, start_pad[:, None, None], 0), axis=0)
            + idx_rank[TOP_K:]).reshape(-1)
    blk_start = jnp.arange(n_blocks, dtype=I32) * TM_MOE
    blk_expert = jnp.minimum(jnp.searchsorted(pad_end, blk_start, side="right"), N_EXPERTS - 1).astype(I32)
    blk_valid = jnp.clip(counts[blk_expert] - (blk_start - start_pad[blk_expert]), 0, TM_MOE).astype(I32)
    e_blocks = padded // TM_MOE
    e_groups = (e_blocks + GT_MOE - 1) // GT_MOE
    grp_end = jnp.cumsum(e_groups)
    max_groups = n_blocks // GT_MOE + N_EXPERTS
    gidx = jnp.arange(max_groups, dtype=I32)
    grp_expert = jnp.minimum(jnp.searchsorted(grp_end, gidx, side="right"), N_EXPERTS - 1).astype(I32)
    in_expert = gidx - (grp_end - e_groups)[grp_expert]
    base = (e_blocks // jnp.maximum(e_groups, 1))[grp_expert]
    rem = e_blocks[grp_expert] - base * e_groups[grp_expert]
    grp_start = (start_pad[grp_expert] // TM_MOE + in_expert * base + jnp.minimum(in_expert, rem)).astype(I32)
    grp_ntiles = (base + (in_expert < rem)).astype(I32)
    n_groups = grp_end[-1:].astype(I32)

    n_rows = (n_blocks + GT_MOE - 1) * TM_MOE
    xs_sorted = _dispatch(hn, dest, n_rows)
    ys = _expert_ffn(xs_sorted, grp_expert, grp_start, grp_ntiles, n_groups, blk_valid,
                     w_gate[0], b_gate[0], w_up[0], b_up[0], w_down[0], b_down[0])
    y_p = _combine(h, gates, dest, ys, 0, n_p)
    y_s = _combine(h, gates, dest, ys, n_p, n_s)
    return (y_p.reshape(bp, sp, D_MODEL), y_s.reshape(bs, ss, D_MODEL))
```

```python
import functools
import math

import jax
import jax.numpy as jnp
from jax import lax
from jax.experimental import pallas as pl
from jax.experimental.pallas import tpu as pltpu

F32 = jnp.float32
BF16 = jnp.bfloat16
I32 = jnp.int32

D_MODEL = 2048
N_HEADS = 8
HEAD_DIM = 64
V_DIM = 2 * HEAD_DIM
ATT_W = N_HEADS * V_DIM
POOL_WINDOWS = (2, 4, 8, 16)
POOL_W = D_MODEL // 2
POOL_GROUP_W = POOL_W // len(POOL_WINDOWS)
IN_COLS = 3 * ATT_W + POOL_W + 2 * D_MODEL
N_BUCKETS = 32
MAX_DISTANCE = 128
N_EXPERTS = 32
TOP_K = 4
D_FF = D_MODEL
SWIGLU_LIMIT = 7.0
SWIGLU_ALPHA = 1.702
NORM_EPS = 1e-6
LAM_INIT = 0.8 - 0.6 * math.exp(-0.3 * 0)
LOG2E = math.log2(math.e)

LANES = 128
VMEM_LIMIT = 56 * 1024 * 1024

TM_IN = 512
TN_IN = 1024
T_ATT = 256
G_ATT = 4
V_AUG = V_DIM + 16
TM_OUT = 256
POOL_HALO = 16
TR_ROUTE = 512
TM_MOE = 528
TF_MOE = 256
GT_MOE = 4
TD_DISP = 256
TC_COMB = 256
ROW_DMA_UNROLL = 8


def _t5_thresholds():
    half = N_BUCKETS // 2
    max_exact = half // 2
    steps = half - max_exact
    ratio = MAX_DISTANCE // max_exact
    out = []
    for k in range(1, steps):
        n = max_exact
        while n ** steps < (max_exact ** steps) * (ratio ** k):
            n += 1
        out.append(n)
    return tuple(out)


T5_THRESHOLDS = _t5_thresholds()
T5_FAR = T5_THRESHOLDS[-1]


def _inproj_kernel(n_prompt_tiles, xp_ref, xs_ref, g1_ref, w_ref, gsum_ref, qg_ref, kg_ref, o_ref, xn_ref):
    i = pl.program_id(0)
    j = pl.program_id(1)

    @pl.when(j == 0)
    def _():
        x = jnp.where(i < n_prompt_tiles, xp_ref[...], xs_ref[...])
        ms = jnp.mean(x * x, axis=-1, keepdims=True)
        xn_ref[...] = (x * lax.rsqrt(ms + NORM_EPS) * g1_ref[...]).astype(BF16)

    acc = jnp.dot(xn_ref[...], w_ref[...], preferred_element_type=F32)

    def head_norm(g_ref, scale):
        sq = (acc * acc).astype(BF16)
        gw = gsum_ref.shape[0]
        parts = [jnp.dot(sq[:, c * gw:(c + 1) * gw], gsum_ref[...], preferred_element_type=F32)
                 for c in range(TN_IN // gw)]
        ss = jnp.concatenate(parts, axis=1)
        return acc * lax.rsqrt(ss * (1.0 / HEAD_DIM) + NORM_EPS) * (g_ref[...] * scale)

    @pl.when(j == 0)
    def _():
        o_ref[...] = head_norm(qg_ref, HEAD_DIM ** -0.5 * LOG2E).astype(BF16)

    @pl.when(j == 1)
    def _():
        o_ref[...] = head_norm(kg_ref, 1.0).astype(BF16)

    @pl.when((j == 2) | (j == 3))
    def _():
        o_ref[...] = acc.astype(BF16)

    @pl.when(j >= 4)
    def _():
        o_ref[...] = (0.5 * jnp.tanh(0.5 * acc) + 0.5).astype(BF16)


def _in_projection(xp, xs, norm1_g, w_in_bf, q_norm_g, k_norm_g):
    n_p, n_s = xp.shape[0], xs.shape[0]
    n_tok = n_p + n_s
    npt, nst = n_p // TM_IN, n_s // TM_IN
    gw = 2 * LANES
    gid = jnp.arange(gw) // HEAD_DIM
    gsum = (gid[:, None] == gid[None, :]).astype(BF16)
    qg = jnp.tile(q_norm_g.astype(F32), TN_IN // HEAD_DIM)[None, :]
    kg = jnp.tile(k_norm_g.astype(F32), TN_IN // HEAD_DIM)[None, :]
    return pl.pallas_call(
        functools.partial(_inproj_kernel, npt),
        grid=(n_tok // TM_IN, IN_COLS // TN_IN),
        in_specs=[
            pl.BlockSpec((TM_IN, D_MODEL), lambda i, j: (jnp.minimum(i, npt - 1), 0)),
            pl.BlockSpec((TM_IN, D_MODEL), lambda i, j: (jnp.maximum(i - npt, 0), 0)),
            pl.BlockSpec((1, D_MODEL), lambda i, j: (0, 0)),
            pl.BlockSpec((D_MODEL, TN_IN), lambda i, j: (0, j)),
            pl.BlockSpec((gw, gw), lambda i, j: (0, 0)),
            pl.BlockSpec((1, TN_IN), lambda i, j: (0, 0)),
            pl.BlockSpec((1, TN_IN), lambda i, j: (0, 0)),
        ],
        out_specs=pl.BlockSpec((TM_IN, TN_IN), lambda i, j: (i, j)),
        out_shape=jax.ShapeDtypeStruct((n_tok, IN_COLS), BF16),
        scratch_shapes=[pltpu.VMEM((TM_IN, D_MODEL), BF16)],
        compiler_params=pltpu.CompilerParams(
            dimension_semantics=("arbitrary", "arbitrary"), vmem_limit_bytes=VMEM_LIMIT),
        name="in_projection",
    )(xp, xs, norm1_g.astype(F32)[None, :], w_in_bf, gsum, qg, kg)


N_BAND = 5


def _band_kernel(rb_ref, o_ref):
    h = pl.program_id(0)
    t = T_ATT
    kj = lax.broadcasted_iota(I32, (t, t), 0)
    qi = lax.broadcasted_iota(I32, (t, t), 1)
    half = N_BUCKETS // 2
    max_exact = half // 2
    for d in range(N_BAND):
        rel = (d - N_BAND // 2) * t + kj - qi
        n = jnp.abs(rel)
        large = jnp.full((t, t), max_exact, I32)
        for th in T5_THRESHOLDS:
            large = large + jnp.where(n >= th, 1, 0)
        bucket = jnp.where(rel > 0, half, 0) + jnp.where(n < max_exact, n, large)
        val = jnp.zeros((t, t), F32)
        for b in range(N_BUCKETS):
            val = jnp.where(bucket == b, rb_ref[b, h], val)
        o_ref[0, d] = val * LOG2E


def _bias_band(rel_bias):
    assert T_ATT >= T5_FAR, "tiles two or more away from the diagonal must lie in the constant-bias region"
    return pl.pallas_call(
        _band_kernel,
        grid=(N_HEADS,),
        in_specs=[pl.BlockSpec(memory_space=pltpu.SMEM)],
        out_specs=pl.BlockSpec((1, N_BAND, T_ATT, T_ATT), lambda h: (h, 0, 0, 0)),
        out_shape=jax.ShapeDtypeStruct((N_HEADS, N_BAND, T_ATT, T_ATT), F32),
        compiler_params=pltpu.CompilerParams(dimension_semantics=("arbitrary",)),
        name="bias_band",
    )(rel_bias.astype(F32))


def _attn_kernel(seq, q_ref, k_ref, v_ref, band_ref, lq1_ref, lk1_ref, lq2_ref, lk2_ref, sg_ref,
                 o_ref, vt_ref, qz_ref, m_ref, acc_ref, s0_ref, s1_ref, c0_ref, c1_ref):
    qi = pl.program_id(2)
    t = T_ATT
    nk = seq // t
    heads = range(G_ATT)

    def cols(g):
        return slice(g * V_DIM, (g + 1) * V_DIM)

    @pl.when(qi == 0)
    def _():
        for g in heads:
            for c in range(nk):
                vt_ref[g, c, 0:V_DIM, :] = v_ref[c * t:(c + 1) * t, cols(g)].astype(F32).T.astype(BF16)
                vt_ref[g, c, V_DIM:V_AUG, :] = jnp.ones((V_AUG - V_DIM, t), BF16)

    for g in heads:
        q = q_ref[:, cols(g)]
        lane = lax.broadcasted_iota(I32, q.shape, 1)
        qz_ref[g, 0:t, :] = jnp.where(lane < HEAD_DIM, q, jnp.zeros_like(q))
        qz_ref[g, t:2 * t, :] = jnp.where(lane >= HEAD_DIM, q, jnp.zeros_like(q))
    m_ref[...] = jnp.full(m_ref.shape, -1e30, F32)
    acc_ref[...] = jnp.zeros(acc_ref.shape, F32)

    def score_tiles(ki, s_ref, c_ref):
        band = jnp.clip(ki - qi + N_BAND // 2, 0, N_BAND - 1)
        for g in heads:
            k_c = k_ref[pl.ds(pl.multiple_of(ki * t, t), t), cols(g)]
            bias = band_ref[g, band]
            s = lax.dot_general(k_c, qz_ref[g], (((1,), (1,)), ((), ())), preferred_element_type=F32)
            s = s + jnp.concatenate([bias, bias], axis=1)
            s_ref[g] = s
            c_ref[g] = jnp.max(s, axis=0, keepdims=True)

    def softmax_pv(ki, s_ref, c_ref):
        for g in heads:
            m_old = m_ref[g]
            m_new = jnp.maximum(m_old, c_ref[g])
            m_ref[g] = m_new
            alpha = jnp.exp2(m_old - m_new)
            p = jnp.exp2(s_ref[g] - m_new).astype(BF16)
            pv = jnp.dot(vt_ref[g, ki], p, preferred_element_type=F32)
            acc_ref[g] = acc_ref[g] * alpha + pv

    score_tiles(0, s0_ref, c0_ref)

    def pair(j, carry):
        k0 = 2 * j
        score_tiles(k0 + 1, s1_ref, c1_ref)
        softmax_pv(k0, s0_ref, c0_ref)
        score_tiles(jnp.minimum(k0 + 2, nk - 1), s0_ref, c0_ref)
        softmax_pv(k0 + 1, s1_ref, c1_ref)
        return carry

    lax.fori_loop(0, nk // 2, pair, 0)

    lam = (jnp.exp(jnp.sum(lq1_ref[...] * lk1_ref[...], axis=1, keepdims=True))
           - jnp.exp(jnp.sum(lq2_ref[...] * lk2_ref[...], axis=1, keepdims=True)) + LAM_INIT)
    for g in heads:
        acc = acc_ref[g, 0:V_DIM, :]
        l = acc_ref[g, V_DIM:V_DIM + 1, :]
        o = acc[:, 0:t] / l[:, 0:t] - lam * (acc[:, t:2 * t] / l[:, t:2 * t])
        ms = jnp.mean(o * o, axis=0, keepdims=True)
        on = o * lax.rsqrt(ms + NORM_EPS) * sg_ref[...] * (1.0 - LAM_INIT)
        o_ref[:, cols(g)] = on.T.astype(BF16)


def _attention(proj, band, lam_vecs, subln_g, batch, seq, row_off):
    t = T_ATT
    nq = seq // t
    assert nq % 2 == 0, "key tiles are processed in pairs"
    gw = G_ATT * V_DIM
    sec = ATT_W // gw
    lq1, lk1, lq2, lk2 = [v.astype(F32)[None, :] for v in lam_vecs]
    vec_spec = pl.BlockSpec((1, HEAD_DIM), lambda b, h, qi: (0, 0))
    return pl.pallas_call(
        functools.partial(_attn_kernel, seq),
        grid=(batch, N_HEADS // G_ATT, nq),
        in_specs=[
            pl.BlockSpec((t, gw), lambda b, h, qi: (row_off // t + b * nq + qi, h)),
            pl.BlockSpec((seq, gw), lambda b, h, qi: (row_off // seq + b, sec + h)),
            pl.BlockSpec((seq, gw), lambda b, h, qi: (row_off // seq + b, 2 * sec + h)),
            pl.BlockSpec((G_ATT, N_BAND, t, t), lambda b, h, qi: (h, 0, 0, 0)),
            vec_spec, vec_spec, vec_spec, vec_spec,
            pl.BlockSpec((V_DIM, 1), lambda b, h, qi: (0, 0)),
        ],
        out_specs=pl.BlockSpec((t, gw), lambda b, h, qi: (b * nq + qi, h)),
        out_shape=jax.ShapeDtypeStruct((batch * seq, ATT_W), BF16),
        scratch_shapes=[
            pltpu.VMEM((G_ATT, seq // t, V_AUG, t), BF16),
            pltpu.VMEM((G_ATT, 2 * t, V_DIM), BF16),
            pltpu.VMEM((G_ATT, 1, 2 * t), F32),
            pltpu.VMEM((G_ATT, V_AUG, 2 * t), F32),
            pltpu.VMEM((G_ATT, t, 2 * t), F32),
            pltpu.VMEM((G_ATT, t, 2 * t), F32),
            pltpu.VMEM((G_ATT, 1, 2 * t), F32),
            pltpu.VMEM((G_ATT, 1, 2 * t), F32),
        ],
        compiler_params=pltpu.CompilerParams(
            dimension_semantics=("arbitrary", "arbitrary", "arbitrary"), vmem_limit_bytes=VMEM_LIMIT),
        name=f"diff_attention_s{seq}",
    )(proj, proj, proj, band, lq1, lk1, lq2, lk2, subln_g.astype(F32)[:, None])


def _pack_bf16_pairs(x):
    n = x.shape[1] // 2
    lo = lax.bitcast_convert_type(x[:, :n].astype(F32), jnp.uint32)
    hi = lax.bitcast_convert_type(x[:, n:].astype(F32), jnp.uint32)
    return lax.shift_right_logical(lo, jnp.uint32(16)) | (hi & jnp.uint32(0xFFFF0000))


def _unpack_bf16_pairs(w):
    lo = lax.bitcast_convert_type(lax.shift_left(w, jnp.uint32(16)), F32)
    hi = lax.bitcast_convert_type(w & jnp.uint32(0xFFFF0000), F32)
    return jnp.concatenate([lo.astype(BF16), hi.astype(BF16)], axis=1)


def _merge_kernel(n_prompt_tiles, seq_p, seq_s,
                  attp_ref, atts_ref, u_ref, ul_ref, ur_ref, ga_ref, gb_ref, xp_ref, xs_ref,
                  wpg_ref, psc_ref, wao_ref, wpo_ref, wo_ref, g2_ref, wr_ref, br_ref,
                  h_ref, hn_ref, lg_ref):
    i = pl.program_id(0)
    tm = TM_OUT
    is_p = i < n_prompt_tiles
    seq = jnp.where(is_p, seq_p, seq_s)
    t0 = jnp.where(is_p, i, i - n_prompt_tiles) * tm
    pos0 = t0 % seq
    first = pos0 == 0
    last = pos0 + tm == seq

    u = u_ref[...]
    zero_halo = jnp.zeros(ul_ref.shape, BF16)
    ul = jnp.where(first, zero_halo, ul_ref[...])
    ur = jnp.where(last, zero_halo, ur_ref[...])
    uext = jnp.concatenate([ul, u, ur], axis=0)
    r = lax.broadcasted_iota(I32, (tm, tm + 2 * POOL_HALO), 0)
    c = lax.broadcasted_iota(I32, (tm, tm + 2 * POOL_HALO), 1) - POOL_HALO
    pos = pos0 + lax.broadcasted_iota(I32, (tm, 1), 0)
    mixed = []
    for gi, w in enumerate(POOL_WINDOWS):
        sl = slice(gi * POOL_GROUP_W, (gi + 1) * POOL_GROUP_W)
        band = jnp.where(c >= r - w // 2, jnp.where(c < r + (w - w // 2), 1.0, 0.0), 0.0).astype(BF16)
        wsum = jnp.dot(band, uext[:, sl], preferred_element_type=F32)
        cnt = (jnp.minimum(pos + (w - w // 2), seq) - jnp.maximum(pos - w // 2, 0)).astype(F32)
        pooled = wsum / cnt - u[:, sl].astype(F32)
        mg = jnp.dot(pooled.astype(BF16), wpg_ref[gi], preferred_element_type=F32)
        mixed.append((mg * psc_ref[:, sl]).astype(BF16))
    mixed = jnp.concatenate(mixed, axis=1)

    att = jnp.where(is_p, attp_ref[...], atts_ref[...])
    y_a = jnp.dot(att, wao_ref[...], preferred_element_type=F32)
    y_b = jnp.dot(mixed, wpo_ref[...], preferred_element_type=F32)
    merged = ga_ref[...].astype(F32) * y_a + gb_ref[...].astype(F32) * y_b
    x = jnp.where(is_p, xp_ref[...], xs_ref[...])
    h = x + jnp.dot(merged.astype(BF16), wo_ref[...], preferred_element_type=F32)
    h_ref[...] = h
    ms = jnp.mean(h * h, axis=-1, keepdims=True)
    hn = (h * lax.rsqrt(ms + NORM_EPS) * g2_ref[...]).astype(BF16)
    hn_ref[...] = _pack_bf16_pairs(hn)
    lg_ref[...] = jnp.dot(hn, wr_ref[...], preferred_element_type=F32) + br_ref[...]


def _merge_project(att_p, att_s, proj, xp, xs, seq_p, seq_s, w_pool_grp, pool_scale, w_att_out, w_pool_out,
                   w_o, norm2_g, w_router, b_router):
    tm = TM_OUT
    n_p, n_s = xp.shape[0], xs.shape[0]
    n_tok = n_p + n_s
    npt = n_p // tm
    hb = tm // POOL_HALO
    n_hblk = n_tok // POOL_HALO
    u_col = 3 * ATT_W // POOL_W
    const = dict(pipeline_mode=pl.Buffered(1))
    return pl.pallas_call(
        functools.partial(_merge_kernel, npt, seq_p, seq_s),
        grid=(n_tok // tm,),
        in_specs=[
            pl.BlockSpec((tm, ATT_W), lambda i: (jnp.minimum(i, npt - 1), 0)),
            pl.BlockSpec((tm, ATT_W), lambda i: (jnp.maximum(i - npt, 0), 0)),
            pl.BlockSpec((tm, POOL_W), lambda i: (i, u_col)),
            pl.BlockSpec((POOL_HALO, POOL_W), lambda i: (jnp.maximum(i * hb - 1, 0), u_col)),
            pl.BlockSpec((POOL_HALO, POOL_W), lambda i: (jnp.minimum((i + 1) * hb, n_hblk - 1), u_col)),
            pl.BlockSpec((tm, D_MODEL), lambda i: (i, 2)),
            pl.BlockSpec((tm, D_MODEL), lambda i: (i, 3)),
            pl.BlockSpec((tm, D_MODEL), lambda i: (jnp.minimum(i, npt - 1), 0)),
            pl.BlockSpec((tm, D_MODEL), lambda i: (jnp.maximum(i - npt, 0), 0)),
            pl.BlockSpec((len(POOL_WINDOWS), POOL_GROUP_W, POOL_GROUP_W), lambda i: (0, 0, 0), **const),
            pl.BlockSpec((1, POOL_W), lambda i: (0, 0), **const),
            pl.BlockSpec((ATT_W, D_MODEL), lambda i: (0, 0), **const),
            pl.BlockSpec((POOL_W, D_MODEL), lambda i: (0, 0), **const),
            pl.BlockSpec((D_MODEL, D_MODEL), lambda i: (0, 0), **const),
            pl.BlockSpec((1, D_MODEL), lambda i: (0, 0), **const),
            pl.BlockSpec((D_MODEL, N_EXPERTS), lambda i: (0, 0), **const),
            pl.BlockSpec((1, N_EXPERTS), lambda i: (0, 0), **const),
        ],
        out_specs=[
            pl.BlockSpec((tm, D_MODEL), lambda i: (i, 0)),
            pl.BlockSpec((tm, D_MODEL // 2), lambda i: (i, 0)),
            pl.BlockSpec((tm, N_EXPERTS), lambda i: (i, 0)),
        ],
        out_shape=[
            jax.ShapeDtypeStruct((n_tok, D_MODEL), F32),
            jax.ShapeDtypeStruct((n_tok, D_MODEL // 2), jnp.uint32),
            jax.ShapeDtypeStruct((n_tok, N_EXPERTS), F32),
        ],
        compiler_params=pltpu.CompilerParams(dimension_semantics=("arbitrary",), vmem_limit_bytes=VMEM_LIMIT),
        name="merge_project",
    )(att_p, att_s, proj, proj, proj, proj, proj, xp, xs,
      w_pool_grp.astype(BF16), pool_scale.astype(F32)[None, :], w_att_out.astype(BF16),
      w_pool_out.astype(BF16), w_o.astype(BF16), norm2_g.astype(F32)[None, :],
      w_router.astype(BF16), b_router.astype(F32)[None, :])


def _route_kernel(lg_ref, ir_ref, gate_ref, cnt_ref, run_ref):
    i = pl.program_id(0)
    tr = TR_ROUTE

    @pl.when(i == 0)
    def _():
        run_ref[...] = jnp.zeros(run_ref.shape, F32)

    cur = lg_ref[...]
    e_iota = lax.broadcasted_iota(I32, cur.shape, 1).astype(F32)
    member = jnp.zeros(cur.shape, F32)
    vals, idxs = [], []
    for _ in range(TOP_K):
        mx = jnp.max(cur, axis=1, keepdims=True)
        am = jnp.min(jnp.where(cur == mx, e_iota, float(N_EXPERTS)), axis=1, keepdims=True)
        hit = e_iota == am
        vals.append(mx)
        idxs.append(am)
        member = member + jnp.where(hit, 1.0, 0.0)
        cur = jnp.where(hit, -jnp.inf, cur)
    exps = [jnp.exp(v - vals[0]) for v in vals]
    denom = exps[0]
    for e in exps[1:]:
        denom = denom + e

    rr = lax.broadcasted_iota(I32, (tr, tr), 0)
    cc = lax.broadcasted_iota(I32, (tr, tr), 1)
    tri = jnp.where(cc < rr, 1.0, 0.0).astype(BF16)
    before = jnp.dot(tri, member.astype(BF16), preferred_element_type=F32) + run_ref[...]
    run_ref[...] = run_ref[...] + jnp.sum(member, axis=0, keepdims=True)
    cnt_ref[...] = run_ref[...]

    lane = lax.broadcasted_iota(I32, (tr, TOP_K), 1)
    wide = lax.broadcasted_iota(I32, (tr, LANES), 1)
    gate_out = jnp.zeros((tr, TOP_K), F32)
    ir = jnp.zeros((tr, LANES), F32)
    for k in range(TOP_K):
        rk = jnp.sum(jnp.where(e_iota == idxs[k], before, 0.0), axis=1, keepdims=True)
        gate_out = jnp.where(lane == k, exps[k] / denom, gate_out)
        ir = jnp.where(wide == k, idxs[k], jnp.where(wide == TOP_K + k, rk, ir))
    gate_ref[...] = gate_out
    ir_ref[...] = ir.T[0:2 * TOP_K, :].astype(I32)


def _route(logits):
    n_tok = logits.shape[0]
    tr = TR_ROUTE
    return pl.pallas_call(
        _route_kernel,
        grid=(n_tok // tr,),
        in_specs=[pl.BlockSpec((tr, N_EXPERTS), lambda i: (i, 0))],
        out_specs=[
            pl.BlockSpec((2 * TOP_K, tr), lambda i: (0, i)),
            pl.BlockSpec((tr, TOP_K), lambda i: (i, 0)),
            pl.BlockSpec((1, N_EXPERTS), lambda i: (0, 0)),
        ],
        out_shape=[
            jax.ShapeDtypeStruct((2 * TOP_K, n_tok), I32),
            jax.ShapeDtypeStruct((n_tok, TOP_K), F32),
            jax.ShapeDtypeStruct((1, N_EXPERTS), F32),
        ],
        scratch_shapes=[pltpu.VMEM((1, N_EXPERTS), F32)],
        compiler_params=pltpu.CompilerParams(dimension_semantics=("arbitrary",)),
        name="route",
    )(logits)


def _dispatch_kernel(*refs):
    dest_refs, (hn_ref, xs_ref, sem) = refs[:TOP_K], refs[TOP_K:]
    td = TD_DISP

    def issue(i, carry):
        for u in range(ROW_DMA_UNROLL):
            r = i * ROW_DMA_UNROLL + u
            for k in range(TOP_K):
                d = dest_refs[k][r]
                pltpu.make_async_copy(hn_ref.at[pl.ds(r, 1)], xs_ref.at[pl.ds(d, 1)], sem).start(priority=k % 2)
        return carry

    lax.fori_loop(0, td // ROW_DMA_UNROLL, issue, 0)
    for _ in range(TOP_K):
        pltpu.make_async_copy(hn_ref, xs_ref.at[pl.ds(0, td)], sem).wait()


def _dispatch(hn, dest_flat, n_rows):
    n_tok = hn.shape[0]
    td = TD_DISP
    return pl.pallas_call(
        _dispatch_kernel,
        grid=(n_tok // td,),
        in_specs=[pl.BlockSpec((td,), lambda i, k=k: (k * (n_tok // td) + i,), memory_space=pltpu.SMEM)
                  for k in range(TOP_K)] + [
            pl.BlockSpec((td, hn.shape[1]), lambda i: (i, 0)),
        ],
        out_specs=pl.BlockSpec(memory_space=pl.ANY),
        out_shape=jax.ShapeDtypeStruct((n_rows, hn.shape[1]), hn.dtype),
        scratch_shapes=[pltpu.SemaphoreType.DMA(())],
        compiler_params=pltpu.CompilerParams(dimension_semantics=("arbitrary",)),
        name="dispatch",
    )(*([dest_flat] * TOP_K), hn)


def _ffn_kernel(ge_ref, gs_ref, gn_ref, ng_ref, bv_ref,
                xs_hbm, wg_ref, bg_ref, wu_ref, bu_ref, wd_ref, bd_ref,
                ys_hbm, slab_ref, acc_ref, sem_x, sem_o):
    grp = pl.program_id(0)
    f = pl.program_id(1)
    nf = pl.num_programs(1)
    tm = TM_MOE
    n_grp = ng_ref[0]

    def load(g, slot):
        row0 = pl.multiple_of(gs_ref[g] * tm, tm)
        return pltpu.make_async_copy(xs_hbm.at[pl.ds(row0, GT_MOE * tm)], slab_ref.at[slot], sem_x.at[slot])

    def store(g, r):
        row0 = pl.multiple_of((gs_ref[g] + r) * tm, tm)
        return pltpu.make_async_copy(acc_ref.at[pl.ds(r * tm, tm)], ys_hbm.at[pl.ds(row0, tm)], sem_o.at[r])

    @pl.when(grp < n_grp)
    def _():
        start = gs_ref[grp]
        nt = gn_ref[grp]
        slot = grp % 2

        @pl.when(f == 0)
        def _():
            pl.when(grp == 0)(lambda: load(0, 0).start())

            @pl.when(grp > 0)
            def _():
                for r in range(GT_MOE):
                    pl.when((r < gn_ref[grp - 1]) & (r >= nt))(lambda r=r: store(grp - 1, r).wait())

            load(grp, slot).wait()
            pl.when(grp + 1 < n_grp)(lambda: load(grp + 1, 1 - slot).start())

        def row_tiles(tiles):
            @pl.when((f == 0) & (grp > 0))
            def _():
                for r in tiles:
                    pl.when(r < gn_ref[grp - 1])(lambda r=r: store(grp - 1, r).wait())

            wg = wg_ref[0].astype(BF16)
            wu = wu_ref[0].astype(BF16)
            wd = wd_ref[0].astype(BF16)
            row = lax.broadcasted_iota(I32, (tm, 1), 0)
            gates, ups = [], []
            for r in tiles:
                words = jnp.where(row < bv_ref[start + r], slab_ref[slot, r * tm:(r + 1) * tm, :], jnp.uint32(0))
                xb = _unpack_bf16_pairs(words)
                gates.append(jnp.dot(xb, wg, preferred_element_type=F32) + bg_ref[0])
                ups.append(jnp.dot(xb, wu, preferred_element_type=F32) + bu_ref[0])
            hidden = []
            for g, up in zip(gates, ups):
                g = jnp.minimum(g, SWIGLU_LIMIT)
                up = jnp.clip(up, -SWIGLU_LIMIT, SWIGLU_LIMIT)
                hidden.append(((up + 1.0) * (g * jax.nn.sigmoid(SWIGLU_ALPHA * g))).astype(BF16))
            for r, hmid in zip(tiles, hidden):
                rows = slice(r * tm, (r + 1) * tm)
                base = jnp.where(f == 0, jnp.broadcast_to(bd_ref[0], (tm, D_MODEL)), acc_ref[rows, :])
                acc_ref[rows, :] = base + jnp.dot(hmid, wd, preferred_element_type=F32)

            @pl.when(f == nf - 1)
            def _():
                for r in tiles:
                    store(grp, r).start()

        assert GT_MOE == 4
        pl.when(nt >= 2)(lambda: row_tiles((0, 1)))
        pl.when(nt == 1)(lambda: row_tiles((0,)))
        pl.when(nt == 4)(lambda: row_tiles((2, 3)))
        pl.when(nt == 3)(lambda: row_tiles((2,)))

        @pl.when((f == nf - 1) & (grp == n_grp - 1))
        def _():
            for r in range(GT_MOE):
                pl.when(r < nt)(lambda r=r: store(grp, r).wait())


def _expert_ffn(xs, grp_expert, grp_start, grp_ntiles, n_groups, blk_valid,
                w_gate, b_gate, w_up, b_up, w_down, b_down):
    n_rows = xs.shape[0]
    max_groups = grp_expert.shape[0]
    nf = D_FF // TF_MOE

    def expert(g, ge, ng):
        return ge[jnp.minimum(g, ng[0] - 1)]

    def ftile(g, f, ng):
        return jnp.where(g < ng[0], f, nf - 1)

    grid_spec = pltpu.PrefetchScalarGridSpec(
        num_scalar_prefetch=5,
        grid=(max_groups, nf),
        in_specs=[
            pl.BlockSpec(memory_space=pl.ANY),
            pl.BlockSpec((1, D_MODEL, TF_MOE), lambda g, f, ge, gs, gn, ng, bv: (expert(g, ge, ng), 0, ftile(g, f, ng))),
            pl.BlockSpec((1, 1, TF_MOE), lambda g, f, ge, gs, gn, ng, bv: (expert(g, ge, ng), 0, ftile(g, f, ng))),
            pl.BlockSpec((1, D_MODEL, TF_MOE), lambda g, f, ge, gs, gn, ng, bv: (expert(g, ge, ng), 0, ftile(g, f, ng))),
            pl.BlockSpec((1, 1, TF_MOE), lambda g, f, ge, gs, gn, ng, bv: (expert(g, ge, ng), 0, ftile(g, f, ng))),
            pl.BlockSpec((1, TF_MOE, D_MODEL), lambda g, f, ge, gs, gn, ng, bv: (expert(g, ge, ng), ftile(g, f, ng), 0)),
            pl.BlockSpec((1, 1, D_MODEL), lambda g, f, ge, gs, gn, ng, bv: (expert(g, ge, ng), 0, 0)),
        ],
        out_specs=pl.BlockSpec(memory_space=pl.ANY),
        scratch_shapes=[
            pltpu.VMEM((2, GT_MOE * TM_MOE, D_MODEL // 2), jnp.uint32),
            pltpu.VMEM((GT_MOE * TM_MOE, D_MODEL), F32),
            pltpu.SemaphoreType.DMA((2,)),
            pltpu.SemaphoreType.DMA((GT_MOE,)),
        ],
    )
    return pl.pallas_call(
        _ffn_kernel,
        grid_spec=grid_spec,
        out_shape=jax.ShapeDtypeStruct((n_rows, D_MODEL), F32),
        compiler_params=pltpu.CompilerParams(
            dimension_semantics=("arbitrary", "arbitrary"), vmem_limit_bytes=VMEM_LIMIT),
        name="expert_ffn",
    )(grp_expert, grp_start, grp_ntiles, n_groups, blk_valid, xs,
      w_gate, b_gate[:, None, :], w_up, b_up[:, None, :], w_down, b_down[:, None, :])


def _combine_kernel(*refs):
    dest_now, dest_next = refs[:TOP_K], refs[TOP_K:2 * TOP_K]
    h_ref, gate_ref, ys_ref, o_ref, buf_ref, sem = refs[2 * TOP_K:]
    tc = TC_COMB
    i = pl.program_id(0)
    slot = i % 2

    def gather(dest_refs, slot):
        def issue(j, carry):
            for u in range(ROW_DMA_UNROLL):
                r = j * ROW_DMA_UNROLL + u
                for k in range(TOP_K):
                    d = dest_refs[k][r]
                    pltpu.make_async_copy(ys_ref.at[pl.ds(d, 1)], buf_ref.at[slot, k, pl.ds(r, 1)],
                                          sem.at[slot]).start(priority=k % 2)
            return carry

        lax.fori_loop(0, tc // ROW_DMA_UNROLL, issue, 0)

    pl.when(i == 0)(lambda: gather(dest_now, 0))
    pl.when(i + 1 < pl.num_programs(0))(lambda: gather(dest_next, 1 - slot))
    for k in range(TOP_K):
        pltpu.make_async_copy(ys_ref.at[pl.ds(0, tc)], buf_ref.at[slot, k], sem.at[slot]).wait()
    gates = gate_ref[...]
    y = h_ref[...]
    for k in range(TOP_K):
        y = y + gates[:, k:k + 1] * buf_ref[slot, k]
    o_ref[...] = y


def _combine(h, gates, dest_flat, ys, tok_off, n_out):
    tc = TC_COMB
    off = tok_off // tc
    n_all = h.shape[0]
    n_steps = n_out // tc

    def dest_spec(k, ahead):
        return pl.BlockSpec((tc,), lambda i: (k * (n_all // tc) + jnp.minimum(i + ahead, n_steps - 1) + off,),
                            memory_space=pltpu.SMEM)

    return pl.pallas_call(
        _combine_kernel,
        grid=(n_steps,),
        in_specs=[dest_spec(k, 0) for k in range(TOP_K)] + [dest_spec(k, 1) for k in range(TOP_K)] + [
            pl.BlockSpec((tc, D_MODEL), lambda i: (i + off, 0)),
            pl.BlockSpec((tc, TOP_K), lambda i: (i + off, 0)),
            pl.BlockSpec(memory_space=pl.ANY),
        ],
        out_specs=pl.BlockSpec((tc, D_MODEL), lambda i: (i, 0)),
        out_shape=jax.ShapeDtypeStruct((n_out, D_MODEL), F32),
        scratch_shapes=[pltpu.VMEM((2, TOP_K, tc, D_MODEL), F32), pltpu.SemaphoreType.DMA((2,))],
        compiler_params=pltpu.CompilerParams(dimension_semantics=("arbitrary",), vmem_limit_bytes=VMEM_LIMIT),
        name="combine",
    )(*([dest_flat] * (2 * TOP_K)), h, gates, ys)


def kernel(x_prompt, x_sample, norm1_g, w_in, q_norm_g, k_norm_g, lambda_q1, lambda_k1, lambda_q2, lambda_k2,
           subln_g, rel_bias, w_pool_grp, pool_scale, w_att_out, w_pool_out, w_o, norm2_g, w_router, b_router,
           w_gate, b_gate, w_up, b_up, w_down, b_down):
    bp, sp, _ = x_prompt.shape
    bs, ss, _ = x_sample.shape
    n_p, n_s = bp * sp, bs * ss
    n_tok = n_p + n_s
    xp = x_prompt.reshape(n_p, D_MODEL)
    xs = x_sample.reshape(n_s, D_MODEL)

    proj = _in_projection(xp, xs, norm1_g[0], w_in[0].astype(BF16), q_norm_g[0], k_norm_g[0])
    band = _bias_band(rel_bias)
    lam_vecs = (lambda_q1[0], lambda_k1[0], lambda_q2[0], lambda_k2[0])
    att_p = _attention(proj, band, lam_vecs, subln_g[0], bp, sp, 0)
    att_s = _attention(proj, band, lam_vecs, subln_g[0], bs, ss, n_p)
    h, hn, logits = _merge_project(att_p, att_s, proj, xp, xs, sp, ss, w_pool_grp[0], pool_scale[0],
                                   w_att_out[0], w_pool_out[0], w_o[0], norm2_g[0], w_router[0], b_router[0])

    idx_rank, gates, counts = _route(logits)

    n_blocks = (n_tok * TOP_K + N_EXPERTS * (TM_MOE - 1)) // TM_MOE
    counts = counts[0].astype(I32)
    padded = (counts + TM_MOE - 1) // TM_MOE * TM_MOE
    pad_end = jnp.cumsum(padded)
    start_pad = pad_end - padded
    e_ids = jnp.arange(N_EXPERTS, dtype=I32)[:, None, None]
    dest = (jnp.sum(jnp.where(idx_rank[None, :TOP_K] == e_ids, start_pad[:, None, None], 0), axis=0)
            + idx_rank[TOP_K:]).reshape(-1)
    blk_start = jnp.arange(n_blocks, dtype=I32) * TM_MOE
    blk_expert = jnp.minimum(jnp.searchsorted(pad_end, blk_start, side="right"), N_EXPERTS - 1).astype(I32)
    blk_valid = jnp.clip(counts[blk_expert] - (blk_start - start_pad[blk_expert]), 0, TM_MOE).astype(I32)
    e_blocks = padded // TM_MOE
    e_groups = (e_blocks + GT_MOE - 1) // GT_MOE
    grp_end = jnp.cumsum(e_groups)
    max_groups = n_blocks // GT_MOE + N_EXPERTS
    gidx = jnp.arange(max_groups, dtype=I32)
    grp_expert = jnp.minimum(jnp.searchsorted(grp_end, gidx, side="right"), N_EXPERTS - 1).astype(I32)
    in_expert = gidx - (grp_end - e_groups)[grp_expert]
    base = (e_blocks // jnp.maximum(e_groups, 1))[grp_expert]
    rem = e_blocks[grp_expert] - base * e_groups[grp_expert]
    grp_start = (start_pad[grp_expert] // TM_MOE + in_expert * base + jnp.minimum(in_expert, rem)).astype(I32)
    grp_ntiles = (base + (in_expert < rem)).astype(I32)
    n_groups = grp_end[-1:].astype(I32)

    n_rows = (n_blocks + GT_MOE - 1) * TM_MOE
    xs_sorted = _dispatch(hn, dest, n_rows)
    ys = _expert_ffn(xs_sorted, grp_expert, grp_start, grp_ntiles, n_groups, blk_valid,
                     w_gate[0], b_gate[0], w_up[0], b_up[0], w_down[0], b_down[0])
    y_p = _combine(h, gates, dest, ys, 0, n_p)
    y_s = _combine(h, gates, dest, ys, n_p, n_s)
    return (y_p.reshape(bp, sp, D_MODEL), y_s.reshape(bs, ss, D_MODEL))
```

```python
import functools
import math

import jax
import jax.numpy as jnp
from jax import lax
from jax.experimental import pallas as pl
from jax.experimental.pallas import tpu as pltpu

F32 = jnp.float32
BF16 = jnp.bfloat16
I32 = jnp.int32

D_MODEL = 2048
N_HEADS = 8
HEAD_DIM = 64
V_DIM = 2 * HEAD_DIM
ATT_W = N_HEADS * V_DIM
POOL_WINDOWS = (2, 4, 8, 16)
POOL_W = D_MODEL // 2
POOL_GROUP_W = POOL_W // len(POOL_WINDOWS)
IN_COLS = 3 * ATT_W + POOL_W + 2 * D_MODEL
N_BUCKETS = 32
MAX_DISTANCE = 128
N_EXPERTS = 32
TOP_K = 4
D_FF = D_MODEL
SWIGLU_LIMIT = 7.0
SWIGLU_ALPHA = 1.702
NORM_EPS = 1e-6
LAM_INIT = 0.8 - 0.6 * math.exp(-0.3 * 0)
LOG2E = math.log2(math.e)

LANES = 128
VMEM_LIMIT = 56 * 1024 * 1024

TM_IN = 512
TN_IN = 1024
T_ATT = 256
G_ATT = 4
KV_PAIRS_PER_ITER = 4
V_AUG = V_DIM + 16
TM_OUT = 256
POOL_HALO = 16
TR_ROUTE = 512
TM_MOE = 528
TF_MOE = 256
GT_MOE = 4
TD_DISP = 256
TC_COMB = 256
ROW_DMA_UNROLL = 8


def _t5_thresholds():
    half = N_BUCKETS // 2
    max_exact = half // 2
    steps = half - max_exact
    ratio = MAX_DISTANCE // max_exact
    out = []
    for k in range(1, steps):
        n = max_exact
        while n ** steps < (max_exact ** steps) * (ratio ** k):
            n += 1
        out.append(n)
    return tuple(out)


T5_THRESHOLDS = _t5_thresholds()
T5_FAR = T5_THRESHOLDS[-1]


def _inproj_kernel(n_prompt_tiles, xp_ref, xs_ref, g1_ref, w_ref, gsum_ref, qg_ref, kg_ref, o_ref, xn_ref):
    i = pl.program_id(0)
    j = pl.program_id(1)

    @pl.when(j == 0)
    def _():
        x = jnp.where(i < n_prompt_tiles, xp_ref[...], xs_ref[...])
        ms = jnp.mean(x * x, axis=-1, keepdims=True)
        xn_ref[...] = (x * lax.rsqrt(ms + NORM_EPS) * g1_ref[...]).astype(BF16)

    acc = jnp.dot(xn_ref[...], w_ref[...], preferred_element_type=F32)

    def head_norm(g_ref, scale):
        sq = (acc * acc).astype(BF16)
        gw = gsum_ref.shape[0]
        parts = [jnp.dot(sq[:, c * gw:(c + 1) * gw], gsum_ref[...], preferred_element_type=F32)
                 for c in range(TN_IN // gw)]
        ss = jnp.concatenate(parts, axis=1)
        return acc * lax.rsqrt(ss * (1.0 / HEAD_DIM) + NORM_EPS) * (g_ref[...] * scale)

    @pl.when(j == 0)
    def _():
        o_ref[...] = head_norm(qg_ref, HEAD_DIM ** -0.5 * LOG2E).astype(BF16)

    @pl.when(j == 1)
    def _():
        o_ref[...] = head_norm(kg_ref, 1.0).astype(BF16)

    @pl.when((j == 2) | (j == 3))
    def _():
        o_ref[...] = acc.astype(BF16)

    @pl.when(j >= 4)
    def _():
        o_ref[...] = (0.5 * jnp.tanh(0.5 * acc) + 0.5).astype(BF16)


def _in_projection(xp, xs, norm1_g, w_in_bf, q_norm_g, k_norm_g):
    n_p, n_s = xp.shape[0], xs.shape[0]
    n_tok = n_p + n_s
    npt, nst = n_p // TM_IN, n_s // TM_IN
    gw = 2 * LANES
    gid = jnp.arange(gw) // HEAD_DIM
    gsum = (gid[:, None] == gid[None, :]).astype(BF16)
    qg = jnp.tile(q_norm_g.astype(F32), TN_IN // HEAD_DIM)[None, :]
    kg = jnp.tile(k_norm_g.astype(F32), TN_IN // HEAD_DIM)[None, :]
    return pl.pallas_call(
        functools.partial(_inproj_kernel, npt),
        grid=(n_tok // TM_IN, IN_COLS // TN_IN),
        in_specs=[
            pl.BlockSpec((TM_IN, D_MODEL), lambda i, j: (jnp.minimum(i, npt - 1), 0)),
            pl.BlockSpec((TM_IN, D_MODEL), lambda i, j: (jnp.maximum(i - npt, 0), 0)),
            pl.BlockSpec((1, D_MODEL), lambda i, j: (0, 0)),
            pl.BlockSpec((D_MODEL, TN_IN), lambda i, j: (0, j)),
            pl.BlockSpec((gw, gw), lambda i, j: (0, 0)),
            pl.BlockSpec((1, TN_IN), lambda i, j: (0, 0)),
            pl.BlockSpec((1, TN_IN), lambda i, j: (0, 0)),
        ],
        out_specs=pl.BlockSpec((TM_IN, TN_IN), lambda i, j: (i, j)),
        out_shape=jax.ShapeDtypeStruct((n_tok, IN_COLS), BF16),
        scratch_shapes=[pltpu.VMEM((TM_IN, D_MODEL), BF16)],
        compiler_params=pltpu.CompilerParams(
            dimension_semantics=("arbitrary", "arbitrary"), vmem_limit_bytes=VMEM_LIMIT),
        name="in_projection",
    )(xp, xs, norm1_g.astype(F32)[None, :], w_in_bf, gsum, qg, kg)


N_BAND = 5


def _band_kernel(rb_ref, o_ref):
    h = pl.program_id(0)
    t = T_ATT
    kj = lax.broadcasted_iota(I32, (t, t), 0)
    qi = lax.broadcasted_iota(I32, (t, t), 1)
    half = N_BUCKETS // 2
    max_exact = half // 2
    for d in range(N_BAND):
        rel = (d - N_BAND // 2) * t + kj - qi
        n = jnp.abs(rel)
        large = jnp.full((t, t), max_exact, I32)
        for th in T5_THRESHOLDS:
            large = large + jnp.where(n >= th, 1, 0)
        bucket = jnp.where(rel > 0, half, 0) + jnp.where(n < max_exact, n, large)
        val = jnp.zeros((t, t), F32)
        for b in range(N_BUCKETS):
            val = jnp.where(bucket == b, rb_ref[b, h], val)
        o_ref[0, d] = val * LOG2E


def _bias_band(rel_bias):
    assert T_ATT >= T5_FAR, "tiles two or more away from the diagonal must lie in the constant-bias region"
    return pl.pallas_call(
        _band_kernel,
        grid=(N_HEADS,),
        in_specs=[pl.BlockSpec(memory_space=pltpu.SMEM)],
        out_specs=pl.BlockSpec((1, N_BAND, T_ATT, T_ATT), lambda h: (h, 0, 0, 0)),
        out_shape=jax.ShapeDtypeStruct((N_HEADS, N_BAND, T_ATT, T_ATT), F32),
        compiler_params=pltpu.CompilerParams(dimension_semantics=("arbitrary",)),
        name="bias_band",
    )(rel_bias.astype(F32))


def _attn_kernel(seq, q_ref, k_ref, v_ref, band_ref, lq1_ref, lk1_ref, lq2_ref, lk2_ref, sg_ref,
                 o_ref, vt_ref, qz_ref, m_ref, acc_ref, s0_ref, s1_ref, c0_ref, c1_ref):
    qi = pl.program_id(2)
    t = T_ATT
    nk = seq // t
    heads = range(G_ATT)

    def cols(g):
        return slice(g * V_DIM, (g + 1) * V_DIM)

    @pl.when(qi == 0)
    def _():
        for g in heads:
            for c in range(nk):
                vt_ref[g, c, 0:V_DIM, :] = v_ref[c * t:(c + 1) * t, cols(g)].astype(F32).T.astype(BF16)
                vt_ref[g, c, V_DIM:V_AUG, :] = jnp.ones((V_AUG - V_DIM, t), BF16)

    for g in heads:
        q = q_ref[:, cols(g)]
        lane = lax.broadcasted_iota(I32, q.shape, 1)
        qz_ref[g, 0:t, :] = jnp.where(lane < HEAD_DIM, q, jnp.zeros_like(q))
        qz_ref[g, t:2 * t, :] = jnp.where(lane >= HEAD_DIM, q, jnp.zeros_like(q))
    m_ref[...] = jnp.full(m_ref.shape, -1e30, F32)
    acc_ref[...] = jnp.zeros(acc_ref.shape, F32)

    def score_tiles(ki, s_ref, c_ref):
        band = jnp.clip(ki - qi + N_BAND // 2, 0, N_BAND - 1)
        for g in heads:
            k_c = k_ref[pl.ds(pl.multiple_of(ki * t, t), t), cols(g)]
            bias = band_ref[g, band]
            s = lax.dot_general(k_c, qz_ref[g], (((1,), (1,)), ((), ())), preferred_element_type=F32)
            s = s + jnp.concatenate([bias, bias], axis=1)
            s_ref[g] = s
            c_ref[g] = jnp.max(s, axis=0, keepdims=True)

    def softmax_pv(ki, s_ref, c_ref):
        for g in heads:
            m_old = m_ref[g]
            m_new = jnp.maximum(m_old, c_ref[g])
            m_ref[g] = m_new
            alpha = jnp.exp2(m_old - m_new)
            p = jnp.exp2(s_ref[g] - m_new).astype(BF16)
            pv = jnp.dot(vt_ref[g, ki], p, preferred_element_type=F32)
            acc_ref[g] = acc_ref[g] * alpha + pv

    score_tiles(0, s0_ref, c0_ref)

    def pairs(j, carry):
        for u in range(KV_PAIRS_PER_ITER):
            k0 = 2 * (j * KV_PAIRS_PER_ITER + u)
            score_tiles(k0 + 1, s1_ref, c1_ref)
            softmax_pv(k0, s0_ref, c0_ref)
            score_tiles(jnp.minimum(k0 + 2, nk - 1), s0_ref, c0_ref)
            softmax_pv(k0 + 1, s1_ref, c1_ref)
        return carry

    lax.fori_loop(0, nk // (2 * KV_PAIRS_PER_ITER), pairs, 0)

    lam = (jnp.exp(jnp.sum(lq1_ref[...] * lk1_ref[...], axis=1, keepdims=True))
           - jnp.exp(jnp.sum(lq2_ref[...] * lk2_ref[...], axis=1, keepdims=True)) + LAM_INIT)
    for g in heads:
        acc = acc_ref[g, 0:V_DIM, :]
        l = acc_ref[g, V_DIM:V_DIM + 1, :]
        o = acc[:, 0:t] / l[:, 0:t] - lam * (acc[:, t:2 * t] / l[:, t:2 * t])
        ms = jnp.mean(o * o, axis=0, keepdims=True)
        on = o * lax.rsqrt(ms + NORM_EPS) * sg_ref[...] * (1.0 - LAM_INIT)
        o_ref[:, cols(g)] = on.T.astype(BF16)


def _attention(proj, band, lam_vecs, subln_g, batch, seq, row_off):
    t = T_ATT
    nq = seq // t
    assert nq % (2 * KV_PAIRS_PER_ITER) == 0, "key tiles are processed in pairs"
    gw = G_ATT * V_DIM
    sec = ATT_W // gw
    lq1, lk1, lq2, lk2 = [v.astype(F32)[None, :] for v in lam_vecs]
    vec_spec = pl.BlockSpec((1, HEAD_DIM), lambda b, h, qi: (0, 0))
    return pl.pallas_call(
        functools.partial(_attn_kernel, seq),
        grid=(batch, N_HEADS // G_ATT, nq),
        in_specs=[
            pl.BlockSpec((t, gw), lambda b, h, qi: (row_off // t + b * nq + qi, h)),
            pl.BlockSpec((seq, gw), lambda b, h, qi: (row_off // seq + b, sec + h)),
            pl.BlockSpec((seq, gw), lambda b, h, qi: (row_off // seq + b, 2 * sec + h)),
            pl.BlockSpec((G_ATT, N_BAND, t, t), lambda b, h, qi: (h, 0, 0, 0)),
            vec_spec, vec_spec, vec_spec, vec_spec,
            pl.BlockSpec((V_DIM, 1), lambda b, h, qi: (0, 0)),
        ],
        out_specs=pl.BlockSpec((t, gw), lambda b, h, qi: (b * nq + qi, h)),
        out_shape=jax.ShapeDtypeStruct((batch * seq, ATT_W), BF16),
        scratch_shapes=[
            pltpu.VMEM((G_ATT, seq // t, V_AUG, t), BF16),
            pltpu.VMEM((G_ATT, 2 * t, V_DIM), BF16),
            pltpu.VMEM((G_ATT, 1, 2 * t), F32),
            pltpu.VMEM((G_ATT, V_AUG, 2 * t), F32),
            pltpu.VMEM((G_ATT, t, 2 * t), F32),
            pltpu.VMEM((G_ATT, t, 2 * t), F32),
            pltpu.VMEM((G_ATT, 1, 2 * t), F32),
            pltpu.VMEM((G_ATT, 1, 2 * t), F32),
        ],
        compiler_params=pltpu.CompilerParams(
            dimension_semantics=("arbitrary", "arbitrary", "arbitrary"), vmem_limit_bytes=VMEM_LIMIT),
        name=f"diff_attention_s{seq}",
    )(proj, proj, proj, band, lq1, lk1, lq2, lk2, subln_g.astype(F32)[:, None])


def _pack_bf16_pairs(x):
    n = x.shape[1] // 2
    lo = lax.bitcast_convert_type(x[:, :n].astype(F32), jnp.uint32)
    hi = lax.bitcast_convert_type(x[:, n:].astype(F32), jnp.uint32)
    return lax.shift_right_logical(lo, jnp.uint32(16)) | (hi & jnp.uint32(0xFFFF0000))


def _unpack_bf16_pairs(w):
    lo = lax.bitcast_convert_type(lax.shift_left(w, jnp.uint32(16)), F32)
    hi = lax.bitcast_convert_type(w & jnp.uint32(0xFFFF0000), F32)
    return jnp.concatenate([lo.astype(BF16), hi.astype(BF16)], axis=1)


def _merge_kernel(n_prompt_tiles, seq_p, seq_s,
                  attp_ref, atts_ref, u_ref, ul_ref, ur_ref, ga_ref, gb_ref, xp_ref, xs_ref,
                  wpg_ref, psc_ref, wao_ref, wpo_ref, wo_ref, g2_ref, wr_ref, br_ref,
                  h_ref, hn_ref, lg_ref):
    i = pl.program_id(0)
    tm = TM_OUT
    is_p = i < n_prompt_tiles
    seq = jnp.where(is_p, seq_p, seq_s)
    t0 = jnp.where(is_p, i, i - n_prompt_tiles) * tm
    pos0 = t0 % seq
    first = pos0 == 0
    last = pos0 + tm == seq

    u = u_ref[...]
    zero_halo = jnp.zeros(ul_ref.shape, BF16)
    ul = jnp.where(first, zero_halo, ul_ref[...])
    ur = jnp.where(last, zero_halo, ur_ref[...])
    uext = jnp.concatenate([ul, u, ur], axis=0)
    r = lax.broadcasted_iota(I32, (tm, tm + 2 * POOL_HALO), 0)
    c = lax.broadcasted_iota(I32, (tm, tm + 2 * POOL_HALO), 1) - POOL_HALO
    pos = pos0 + lax.broadcasted_iota(I32, (tm, 1), 0)
    mixed = []
    for gi, w in enumerate(POOL_WINDOWS):
        sl = slice(gi * POOL_GROUP_W, (gi + 1) * POOL_GROUP_W)
        band = jnp.where(c >= r - w // 2, jnp.where(c < r + (w - w // 2), 1.0, 0.0), 0.0).astype(BF16)
        wsum = jnp.dot(band, uext[:, sl], preferred_element_type=F32)
        cnt = (jnp.minimum(pos + (w - w // 2), seq) - jnp.maximum(pos - w // 2, 0)).astype(F32)
        pooled = wsum / cnt - u[:, sl].astype(F32)
        mg = jnp.dot(pooled.astype(BF16), wpg_ref[gi], preferred_element_type=F32)
        mixed.append((mg * psc_ref[:, sl]).astype(BF16))
    mixed = jnp.concatenate(mixed, axis=1)

    att = jnp.where(is_p, attp_ref[...], atts_ref[...])
    y_a = jnp.dot(att, wao_ref[...], preferred_element_type=F32)
    y_b = jnp.dot(mixed, wpo_ref[...], preferred_element_type=F32)
    merged = ga_ref[...].astype(F32) * y_a + gb_ref[...].astype(F32) * y_b
    x = jnp.where(is_p, xp_ref[...], xs_ref[...])
    h = x + jnp.dot(merged.astype(BF16), wo_ref[...], preferred_element_type=F32)
    h_ref[...] = h
    ms = jnp.mean(h * h, axis=-1, keepdims=True)
    hn = (h * lax.rsqrt(ms + NORM_EPS) * g2_ref[...]).astype(BF16)
    hn_ref[...] = _pack_bf16_pairs(hn)
    lg_ref[...] = jnp.dot(hn, wr_ref[...], preferred_element_type=F32) + br_ref[...]


def _merge_project(att_p, att_s, proj, xp, xs, seq_p, seq_s, w_pool_grp, pool_scale, w_att_out, w_pool_out,
                   w_o, norm2_g, w_router, b_router):
    tm = TM_OUT
    n_p, n_s = xp.shape[0], xs.shape[0]
    n_tok = n_p + n_s
    npt = n_p // tm
    hb = tm // POOL_HALO
    n_hblk = n_tok // POOL_HALO
    u_col = 3 * ATT_W // POOL_W
    const = dict(pipeline_mode=pl.Buffered(1))
    return pl.pallas_call(
        functools.partial(_merge_kernel, npt, seq_p, seq_s),
        grid=(n_tok // tm,),
        in_specs=[
            pl.BlockSpec((tm, ATT_W), lambda i: (jnp.minimum(i, npt - 1), 0)),
            pl.BlockSpec((tm, ATT_W), lambda i: (jnp.maximum(i - npt, 0), 0)),
            pl.BlockSpec((tm, POOL_W), lambda i: (i, u_col)),
            pl.BlockSpec((POOL_HALO, POOL_W), lambda i: (jnp.maximum(i * hb - 1, 0), u_col)),
            pl.BlockSpec((POOL_HALO, POOL_W), lambda i: (jnp.minimum((i + 1) * hb, n_hblk - 1), u_col)),
            pl.BlockSpec((tm, D_MODEL), lambda i: (i, 2)),
            pl.BlockSpec((tm, D_MODEL), lambda i: (i, 3)),
            pl.BlockSpec((tm, D_MODEL), lambda i: (jnp.minimum(i, npt - 1), 0)),
            pl.BlockSpec((tm, D_MODEL), lambda i: (jnp.maximum(i - npt, 0), 0)),
            pl.BlockSpec((len(POOL_WINDOWS), POOL_GROUP_W, POOL_GROUP_W), lambda i: (0, 0, 0), **const),
            pl.BlockSpec((1, POOL_W), lambda i: (0, 0), **const),
            pl.BlockSpec((ATT_W, D_MODEL), lambda i: (0, 0), **const),
            pl.BlockSpec((POOL_W, D_MODEL), lambda i: (0, 0), **const),
            pl.BlockSpec((D_MODEL, D_MODEL), lambda i: (0, 0), **const),
            pl.BlockSpec((1, D_MODEL), lambda i: (0, 0), **const),
            pl.BlockSpec((D_MODEL, N_EXPERTS), lambda i: (0, 0), **const),
            pl.BlockSpec((1, N_EXPERTS), lambda i: (0, 0), **const),
        ],
        out_specs=[
            pl.BlockSpec((tm, D_MODEL), lambda i: (i, 0)),
            pl.BlockSpec((tm, D_MODEL // 2), lambda i: (i, 0)),
            pl.BlockSpec((tm, N_EXPERTS), lambda i: (i, 0)),
        ],
        out_shape=[
            jax.ShapeDtypeStruct((n_tok, D_MODEL), F32),
            jax.ShapeDtypeStruct((n_tok, D_MODEL // 2), jnp.uint32),
            jax.ShapeDtypeStruct((n_tok, N_EXPERTS), F32),
        ],
        compiler_params=pltpu.CompilerParams(dimension_semantics=("arbitrary",), vmem_limit_bytes=VMEM_LIMIT),
        name="merge_project",
    )(att_p, att_s, proj, proj, proj, proj, proj, xp, xs,
      w_pool_grp.astype(BF16), pool_scale.astype(F32)[None, :], w_att_out.astype(BF16),
      w_pool_out.astype(BF16), w_o.astype(BF16), norm2_g.astype(F32)[None, :],
      w_router.astype(BF16), b_router.astype(F32)[None, :])


def _route_kernel(lg_ref, ir_ref, gate_ref, cnt_ref, run_ref):
    i = pl.program_id(0)
    tr = TR_ROUTE

    @pl.when(i == 0)
    def _():
        run_ref[...] = jnp.zeros(run_ref.shape, F32)

    cur = lg_ref[...]
    e_iota = lax.broadcasted_iota(I32, cur.shape, 1).astype(F32)
    member = jnp.zeros(cur.shape, F32)
    vals, idxs = [], []
    for _ in range(TOP_K):
        mx = jnp.max(cur, axis=1, keepdims=True)
        am = jnp.min(jnp.where(cur == mx, e_iota, float(N_EXPERTS)), axis=1, keepdims=True)
        hit = e_iota == am
        vals.append(mx)
        idxs.append(am)
        member = member + jnp.where(hit, 1.0, 0.0)
        cur = jnp.where(hit, -jnp.inf, cur)
    exps = [jnp.exp(v - vals[0]) for v in vals]
    denom = exps[0]
    for e in exps[1:]:
        denom = denom + e

    rr = lax.broadcasted_iota(I32, (tr, tr), 0)
    cc = lax.broadcasted_iota(I32, (tr, tr), 1)
    tri = jnp.where(cc < rr, 1.0, 0.0).astype(BF16)
    before = jnp.dot(tri, member.astype(BF16), preferred_element_type=F32) + run_ref[...]
    run_ref[...] = run_ref[...] + jnp.sum(member, axis=0, keepdims=True)
    cnt_ref[...] = run_ref[...]

    lane = lax.broadcasted_iota(I32, (tr, TOP_K), 1)
    wide = lax.broadcasted_iota(I32, (tr, LANES), 1)
    gate_out = jnp.zeros((tr, TOP_K), F32)
    ir = jnp.zeros((tr, LANES), F32)
    for k in range(TOP_K):
        rk = jnp.sum(jnp.where(e_iota == idxs[k], before, 0.0), axis=1, keepdims=True)
        gate_out = jnp.where(lane == k, exps[k] / denom, gate_out)
        ir = jnp.where(wide == k, idxs[k], jnp.where(wide == TOP_K + k, rk, ir))
    gate_ref[...] = gate_out
    ir_ref[...] = ir.T[0:2 * TOP_K, :].astype(I32)


def _route(logits):
    n_tok = logits.shape[0]
    tr = TR_ROUTE
    return pl.pallas_call(
        _route_kernel,
        grid=(n_tok // tr,),
        in_specs=[pl.BlockSpec((tr, N_EXPERTS), lambda i: (i, 0))],
        out_specs=[
            pl.BlockSpec((2 * TOP_K, tr), lambda i: (0, i)),
            pl.BlockSpec((tr, TOP_K), lambda i: (i, 0)),
            pl.BlockSpec((1, N_EXPERTS), lambda i: (0, 0)),
        ],
        out_shape=[
            jax.ShapeDtypeStruct((2 * TOP_K, n_tok), I32),
            jax.ShapeDtypeStruct((n_tok, TOP_K), F32),
            jax.ShapeDtypeStruct((1, N_EXPERTS), F32),
        ],
        scratch_shapes=[pltpu.VMEM((1, N_EXPERTS), F32)],
        compiler_params=pltpu.CompilerParams(dimension_semantics=("arbitrary",)),
        name="route",
    )(logits)


def _dispatch_kernel(*refs):
    dest_refs, (hn_ref, xs_ref, sem) = refs[:TOP_K], refs[TOP_K:]
    td = TD_DISP

    def issue(i, carry):
        for u in range(ROW_DMA_UNROLL):
            r = i * ROW_DMA_UNROLL + u
            for k in range(TOP_K):
                d = dest_refs[k][r]
                pltpu.make_async_copy(hn_ref.at[pl.ds(r, 1)], xs_ref.at[pl.ds(d, 1)], sem).start(priority=k % 2)
        return carry

    lax.fori_loop(0, td // ROW_DMA_UNROLL, issue, 0)
    for _ in range(TOP_K):
        pltpu.make_async_copy(hn_ref, xs_ref.at[pl.ds(0, td)], sem).wait()


def _dispatch(hn, dest_flat, n_rows):
    n_tok = hn.shape[0]
    td = TD_DISP
    return pl.pallas_call(
        _dispatch_kernel,
        grid=(n_tok // td,),
        in_specs=[pl.BlockSpec((td,), lambda i, k=k: (k * (n_tok // td) + i,), memory_space=pltpu.SMEM)
                  for k in range(TOP_K)] + [
            pl.BlockSpec((td, hn.shape[1]), lambda i: (i, 0)),
        ],
        out_specs=pl.BlockSpec(memory_space=pl.ANY),
        out_shape=jax.ShapeDtypeStruct((n_rows, hn.shape[1]), hn.dtype),
        scratch_shapes=[pltpu.SemaphoreType.DMA(())],
        compiler_params=pltpu.CompilerParams(dimension_semantics=("arbitrary",)),
        name="dispatch",
    )(*([dest_flat] * TOP_K), hn)


def _ffn_kernel(ge_ref, gs_ref, gn_ref, ng_ref, bv_ref,
                xs_hbm, wg_ref, bg_ref, wu_ref, bu_ref, wd_ref, bd_ref,
                ys_hbm, slab_ref, acc_ref, sem_x, sem_o):
    grp = pl.program_id(0)
    f = pl.program_id(1)
    nf = pl.num_programs(1)
    tm = TM_MOE
    n_grp = ng_ref[0]

    def load(g, slot):
        row0 = pl.multiple_of(gs_ref[g] * tm, tm)
        return pltpu.make_async_copy(xs_hbm.at[pl.ds(row0, GT_MOE * tm)], slab_ref.at[slot], sem_x.at[slot])

    def store(g, r):
        row0 = pl.multiple_of((gs_ref[g] + r) * tm, tm)
        return pltpu.make_async_copy(acc_ref.at[pl.ds(r * tm, tm)], ys_hbm.at[pl.ds(row0, tm)], sem_o.at[r])

    @pl.when(grp < n_grp)
    def _():
        start = gs_ref[grp]
        nt = gn_ref[grp]
        slot = grp % 2

        @pl.when(f == 0)
        def _():
            pl.when(grp == 0)(lambda: load(0, 0).start())

            @pl.when(grp > 0)
            def _():
                for r in range(GT_MOE):
                    pl.when((r < gn_ref[grp - 1]) & (r >= nt))(lambda r=r: store(grp - 1, r).wait())

            load(grp, slot).wait()
            pl.when(grp + 1 < n_grp)(lambda: load(grp + 1, 1 - slot).start())

        def row_tiles(tiles):
            @pl.when((f == 0) & (grp > 0))
            def _():
                for r in tiles:
                    pl.when(r < gn_ref[grp - 1])(lambda r=r: store(grp - 1, r).wait())

            wg = wg_ref[0].astype(BF16)
            wu = wu_ref[0].astype(BF16)
            wd = wd_ref[0].astype(BF16)
            row = lax.broadcasted_iota(I32, (tm, 1), 0)
            gates, ups = [], []
            for r in tiles:
                words = jnp.where(row < bv_ref[start + r], slab_ref[slot, r * tm:(r + 1) * tm, :], jnp.uint32(0))
                xb = _unpack_bf16_pairs(words)
                gates.append(jnp.dot(xb, wg, preferred_element_type=F32) + bg_ref[0])
                ups.append(jnp.dot(xb, wu, preferred_element_type=F32) + bu_ref[0])
            hidden = []
            for g, up in zip(gates, ups):
                g = jnp.minimum(g, SWIGLU_LIMIT)
                up = jnp.clip(up, -SWIGLU_LIMIT, SWIGLU_LIMIT)
                hidden.append(((up + 1.0) * (g * jax.nn.sigmoid(SWIGLU_ALPHA * g))).astype(BF16))
            for r, hmid in zip(tiles, hidden):
                rows = slice(r * tm, (r + 1) * tm)
                base = jnp.where(f == 0, jnp.broadcast_to(bd_ref[0], (tm, D_MODEL)), acc_ref[rows, :])
                acc_ref[rows, :] = base + jnp.dot(hmid, wd, preferred_element_type=F32)

            @pl.when(f == nf - 1)
            def _():
                for r in tiles:
                    store(grp, r).start()

        assert GT_MOE == 4
        for n in range(1, GT_MOE + 1):
            pl.when(nt == n)(lambda n=n: row_tiles(tuple(range(n))))

        @pl.when((f == nf - 1) & (grp == n_grp - 1))
        def _():
            for r in range(GT_MOE):
                pl.when(r < nt)(lambda r=r: store(grp, r).wait())


def _expert_ffn(xs, grp_expert, grp_start, grp_ntiles, n_groups, blk_valid,
                w_gate, b_gate, w_up, b_up, w_down, b_down):
    n_rows = xs.shape[0]
    max_groups = grp_expert.shape[0]
    nf = D_FF // TF_MOE

    def expert(g, ge, ng):
        return ge[jnp.minimum(g, ng[0] - 1)]

    def ftile(g, f, ng):
        return jnp.where(g < ng[0], f, nf - 1)

    grid_spec = pltpu.PrefetchScalarGridSpec(
        num_scalar_prefetch=5,
        grid=(max_groups, nf),
        in_specs=[
            pl.BlockSpec(memory_space=pl.ANY),
            pl.BlockSpec((1, D_MODEL, TF_MOE), lambda g, f, ge, gs, gn, ng, bv: (expert(g, ge, ng), 0, ftile(g, f, ng))),
            pl.BlockSpec((1, 1, TF_MOE), lambda g, f, ge, gs, gn, ng, bv: (expert(g, ge, ng), 0, ftile(g, f, ng))),
            pl.BlockSpec((1, D_MODEL, TF_MOE), lambda g, f, ge, gs, gn, ng, bv: (expert(g, ge, ng), 0, ftile(g, f, ng))),
            pl.BlockSpec((1, 1, TF_MOE), lambda g, f, ge, gs, gn, ng, bv: (expert(g, ge, ng), 0, ftile(g, f, ng))),
            pl.BlockSpec((1, TF_MOE, D_MODEL), lambda g, f, ge, gs, gn, ng, bv: (expert(g, ge, ng), ftile(g, f, ng), 0)),
            pl.BlockSpec((1, 1, D_MODEL), lambda g, f, ge, gs, gn, ng, bv: (expert(g, ge, ng), 0, 0)),
        ],
        out_specs=pl.BlockSpec(memory_space=pl.ANY),
        scratch_shapes=[
            pltpu.VMEM((2, GT_MOE * TM_MOE, D_MODEL // 2), jnp.uint32),
            pltpu.VMEM((GT_MOE * TM_MOE, D_MODEL), F32),
            pltpu.SemaphoreType.DMA((2,)),
            pltpu.SemaphoreType.DMA((GT_MOE,)),
        ],
    )
    return pl.pallas_call(
        _ffn_kernel,
        grid_spec=grid_spec,
        out_shape=jax.ShapeDtypeStruct((n_rows, D_MODEL), F32),
        compiler_params=pltpu.CompilerParams(
            dimension_semantics=("arbitrary", "arbitrary"), vmem_limit_bytes=VMEM_LIMIT),
        name="expert_ffn",
    )(grp_expert, grp_start, grp_ntiles, n_groups, blk_valid, xs,
      w_gate, b_gate[:, None, :], w_up, b_up[:, None, :], w_down, b_down[:, None, :])


def _combine_kernel(*refs):
    dest_now, dest_next = refs[:TOP_K], refs[TOP_K:2 * TOP_K]
    h_ref, gate_ref, ys_ref, o_ref, buf_ref, sem = refs[2 * TOP_K:]
    tc = TC_COMB
    i = pl.program_id(0)
    slot = i % 2

    def gather(dest_refs, slot):
        def issue(j, carry):
            for u in range(ROW_DMA_UNROLL):
                r = j * ROW_DMA_UNROLL + u
                for k in range(TOP_K):
                    d = dest_refs[k][r]
                    pltpu.make_async_copy(ys_ref.at[pl.ds(d, 1)], buf_ref.at[slot, k, pl.ds(r, 1)],
                                          sem.at[slot]).start(priority=k % 2)
            return carry

        lax.fori_loop(0, tc // ROW_DMA_UNROLL, issue, 0)

    pl.when(i == 0)(lambda: gather(dest_now, 0))
    pl.when(i + 1 < pl.num_programs(0))(lambda: gather(dest_next, 1 - slot))
    for k in range(TOP_K):
        pltpu.make_async_copy(ys_ref.at[pl.ds(0, tc)], buf_ref.at[slot, k], sem.at[slot]).wait()
    gates = gate_ref[...]
    y = h_ref[...]
    for k in range(TOP_K):
        y = y + gates[:, k:k + 1] * buf_ref[slot, k]
    o_ref[...] = y


def _combine(h, gates, dest_flat, ys, tok_off, n_out):
    tc = TC_COMB
    off = tok_off // tc
    n_all = h.shape[0]
    n_steps = n_out // tc

    def dest_spec(k, ahead):
        return pl.BlockSpec((tc,), lambda i: (k * (n_all // tc) + jnp.minimum(i + ahead, n_steps - 1) + off,),
                            memory_space=pltpu.SMEM)

    return pl.pallas_call(
        _combine_kernel,
        grid=(n_steps,),
        in_specs=[dest_spec(k, 0) for k in range(TOP_K)] + [dest_spec(k, 1) for k in range(TOP_K)] + [
            pl.BlockSpec((tc, D_MODEL), lambda i: (i + off, 0)),
            pl.BlockSpec((tc, TOP_K), lambda i: (i + off, 0)),
            pl.BlockSpec(memory_space=pl.ANY),
        ],
        out_specs=pl.BlockSpec((tc, D_MODEL), lambda i: (i, 0)),
        out_shape=jax.ShapeDtypeStruct((n_out, D_MODEL), F32),
        scratch_shapes=[pltpu.VMEM((2, TOP_K, tc, D_MODEL), F32), pltpu.SemaphoreType.DMA((2,))],
        compiler_params=pltpu.CompilerParams(dimension_semantics=("arbitrary",), vmem_limit_bytes=VMEM_LIMIT),
        name="combine",
    )(*([dest_flat] * (2 * TOP_K)), h, gates, ys)


def kernel(x_prompt, x_sample, norm1_g, w_in, q_norm_g, k_norm_g, lambda_q1, lambda_k1, lambda_q2, lambda_k2,
           subln_g, rel_bias, w_pool_grp, pool_scale, w_att_out, w_pool_out, w_o, norm2_g, w_router, b_router,
           w_gate, b_gate, w_up, b_up, w_down, b_down):
    bp, sp, _ = x_prompt.shape
    bs, ss, _ = x_sample.shape
    n_p, n_s = bp * sp, bs * ss
    n_tok = n_p + n_s
    xp = x_prompt.reshape(n_p, D_MODEL)
    xs = x_sample.reshape(n_s, D_MODEL)

    proj = _in_projection(xp, xs, norm1_g[0], w_in[0].astype(BF16), q_norm_g[0], k_norm_g[0])
    band = _bias_band(rel_bias)
    lam_vecs = (lambda_q1[0], lambda_k1[0], lambda_q2[0], lambda_k2[0])
    att_p = _attention(proj, band, lam_vecs, subln_g[0], bp, sp, 0)
    att_s = _attention(proj, band, lam_vecs, subln_g[0], bs, ss, n_p)
    h, hn, logits = _merge_project(att_p, att_s, proj, xp, xs, sp, ss, w_pool_grp[0], pool_scale[0],
                                   w_att_out[0], w_pool_out[0], w_o[0], norm2_g[0], w_router[0], b_router[0])

    idx_rank, gates, counts = _route(logits)

    n_blocks = (n_tok * TOP_K + N_EXPERTS * (TM_MOE - 1)) // TM_MOE
    counts = counts[0].astype(I32)
    padded = (counts + TM_MOE - 1) // TM_MOE * TM_MOE
    pad_end = jnp.cumsum(padded)
    start_pad = pad_end - padded
    e_ids = jnp.arange(N_EXPERTS, dtype=I32)[:, None, None]
    dest = (jnp.sum(jnp.where(idx_rank[None, :TOP_K] == e_ids, start_pad[:, None, None], 0), axis=0)
            + idx_rank[TOP_K:]).reshape(-1)
    blk_start = jnp.arange(n_blocks, dtype=I32) * TM_MOE
    blk_expert = jnp.minimum(jnp.searchsorted(pad_end, blk_start, side="right"), N_EXPERTS - 1).astype(I32)
    blk_valid = jnp.clip(counts[blk_expert] - (blk_start - start_pad[blk_expert]), 0, TM_MOE).astype(I32)
    e_blocks = padded // TM_MOE
    e_groups = (e_blocks + GT_MOE - 1) // GT_MOE
    grp_end = jnp.cumsum(e_groups)
    max_groups = n_blocks // GT_MOE + N_EXPERTS
    gidx = jnp.arange(max_groups, dtype=I32)
    grp_expert = jnp.minimum(jnp.searchsorted(grp_end, gidx, side="right"), N_EXPERTS - 1).astype(I32)
    in_expert = gidx - (grp_end - e_groups)[grp_expert]
    base = (e_blocks // jnp.maximum(e_groups, 1))[grp_expert]
    rem = e_blocks[grp_expert] - base * e_groups[grp_expert]
    grp_start = (start_pad[grp_expert] // TM_MOE + in_expert * base + jnp.minimum(in_expert, rem)).astype(I32)
    grp_ntiles = (base + (in_expert < rem)).astype(I32)
    n_groups = grp_end[-1:].astype(I32)

    n_rows = (n_blocks + GT_MOE - 1) * TM_MOE
    xs_sorted = _dispatch(hn, dest, n_rows)
    ys = _expert_ffn(xs_sorted, grp_expert, grp_start, grp_ntiles, n_groups, blk_valid,
                     w_gate[0], b_gate[0], w_up[0], b_up[0], w_down[0], b_down[0])
    y_p = _combine(h, gates, dest, ys, 0, n_p)
    y_s = _combine(h, gates, dest, ys, n_p, n_s)
    return (y_p.reshape(bp, sp, D_MODEL), y_s.reshape(bs, ss, D_MODEL))
```

```python
import functools
import math

import jax
import jax.numpy as jnp
from jax import lax
from jax.experimental import pallas as pl
from jax.experimental.pallas import tpu as pltpu

F32 = jnp.float32
BF16 = jnp.bfloat16
I32 = jnp.int32

D_MODEL = 2048
N_HEADS = 8
HEAD_DIM = 64
V_DIM = 2 * HEAD_DIM
ATT_W = N_HEADS * V_DIM
POOL_WINDOWS = (2, 4, 8, 16)
POOL_W = D_MODEL // 2
POOL_GROUP_W = POOL_W // len(POOL_WINDOWS)
IN_COLS = 3 * ATT_W + POOL_W + 2 * D_MODEL
N_BUCKETS = 32
MAX_DISTANCE = 128
N_EXPERTS = 32
TOP_K = 4
D_FF = D_MODEL
SWIGLU_LIMIT = 7.0
SWIGLU_ALPHA = 1.702
NORM_EPS = 1e-6
LAM_INIT = 0.8 - 0.6 * math.exp(-0.3 * 0)
LOG2E = math.log2(math.e)

LANES = 128
VMEM_LIMIT = 56 * 1024 * 1024

TM_IN = 512
TN_IN = 2048
T_ATT = 256
G_ATT = 4
KV_PAIRS_PER_ITER = 4
V_AUG = V_DIM + 16
TM_OUT = 256
POOL_HALO = 16
TR_ROUTE = 512
TM_MOE = 528
TF_MOE = 256
GT_MOE = 4
TD_DISP = 256
TC_COMB = 256
ROW_DMA_UNROLL = 8


def _t5_thresholds():
    half = N_BUCKETS // 2
    max_exact = half // 2
    steps = half - max_exact
    ratio = MAX_DISTANCE // max_exact
    out = []
    for k in range(1, steps):
        n = max_exact
        while n ** steps < (max_exact ** steps) * (ratio ** k):
            n += 1
        out.append(n)
    return tuple(out)


T5_THRESHOLDS = _t5_thresholds()
T5_FAR = T5_THRESHOLDS[-1]


def _inproj_kernel(n_prompt_tiles, xp_ref, xs_ref, g1_ref, w_ref, gsum_ref, qg_ref, kg_ref, o_ref, xn_ref):
    i = pl.program_id(0)
    j = pl.program_id(1)

    @pl.when(j == 0)
    def _():
        x = jnp.where(i < n_prompt_tiles, xp_ref[...], xs_ref[...])
        ms = jnp.mean(x * x, axis=-1, keepdims=True)
        xn_ref[...] = (x * lax.rsqrt(ms + NORM_EPS) * g1_ref[...]).astype(BF16)

    half = TN_IN // 2

    def sections():
        xn = xn_ref[...]
        return [jnp.dot(xn, w_ref[:, c * half:(c + 1) * half], preferred_element_type=F32) for c in range(2)]

    def head_norm(acc, g_ref, scale):
        sq = (acc * acc).astype(BF16)
        gw = gsum_ref.shape[0]
        parts = [jnp.dot(sq[:, c * gw:(c + 1) * gw], gsum_ref[...], preferred_element_type=F32)
                 for c in range(half // gw)]
        ss = jnp.concatenate(parts, axis=1)
        return acc * lax.rsqrt(ss * (1.0 / HEAD_DIM) + NORM_EPS) * (g_ref[...] * scale)

    @pl.when(j == 0)
    def _():
        q, k = sections()
        o_ref[:, 0:half] = head_norm(q, qg_ref, HEAD_DIM ** -0.5 * LOG2E).astype(BF16)
        o_ref[:, half:TN_IN] = head_norm(k, kg_ref, 1.0).astype(BF16)

    @pl.when(j == 1)
    def _():
        for c, acc in enumerate(sections()):
            o_ref[:, c * half:(c + 1) * half] = acc.astype(BF16)

    @pl.when(j >= 2)
    def _():
        for c, acc in enumerate(sections()):
            o_ref[:, c * half:(c + 1) * half] = (0.5 * jnp.tanh(0.5 * acc) + 0.5).astype(BF16)


def _in_projection(xp, xs, norm1_g, w_in_bf, q_norm_g, k_norm_g):
    n_p, n_s = xp.shape[0], xs.shape[0]
    n_tok = n_p + n_s
    npt, nst = n_p // TM_IN, n_s // TM_IN
    gw = 2 * LANES
    gid = jnp.arange(gw) // HEAD_DIM
    gsum = (gid[:, None] == gid[None, :]).astype(BF16)
    assert TN_IN // 2 == ATT_W == POOL_W and D_MODEL % TN_IN == 0, "one q/k/v/u section per half column tile"
    qg = jnp.tile(q_norm_g.astype(F32), ATT_W // HEAD_DIM)[None, :]
    kg = jnp.tile(k_norm_g.astype(F32), ATT_W // HEAD_DIM)[None, :]
    return pl.pallas_call(
        functools.partial(_inproj_kernel, npt),
        grid=(n_tok // TM_IN, IN_COLS // TN_IN),
        in_specs=[
            pl.BlockSpec((TM_IN, D_MODEL), lambda i, j: (jnp.minimum(i, npt - 1), 0)),
            pl.BlockSpec((TM_IN, D_MODEL), lambda i, j: (jnp.maximum(i - npt, 0), 0)),
            pl.BlockSpec((1, D_MODEL), lambda i, j: (0, 0)),
            pl.BlockSpec((D_MODEL, TN_IN), lambda i, j: (0, j)),
            pl.BlockSpec((gw, gw), lambda i, j: (0, 0)),
            pl.BlockSpec((1, ATT_W), lambda i, j: (0, 0)),
            pl.BlockSpec((1, ATT_W), lambda i, j: (0, 0)),
        ],
        out_specs=pl.BlockSpec((TM_IN, TN_IN), lambda i, j: (i, j)),
        out_shape=jax.ShapeDtypeStruct((n_tok, IN_COLS), BF16),
        scratch_shapes=[pltpu.VMEM((TM_IN, D_MODEL), BF16)],
        compiler_params=pltpu.CompilerParams(
            dimension_semantics=("arbitrary", "arbitrary"), vmem_limit_bytes=VMEM_LIMIT),
        name="in_projection",
    )(xp, xs, norm1_g.astype(F32)[None, :], w_in_bf, gsum, qg, kg)


N_BAND = 5


def _band_kernel(rb_ref, o_ref):
    h = pl.program_id(0)
    t = T_ATT
    kj = lax.broadcasted_iota(I32, (t, t), 0)
    qi = lax.broadcasted_iota(I32, (t, t), 1)
    half = N_BUCKETS // 2
    max_exact = half // 2
    for d in range(N_BAND):
        rel = (d - N_BAND // 2) * t + kj - qi
        n = jnp.abs(rel)
        large = jnp.full((t, t), max_exact, I32)
        for th in T5_THRESHOLDS:
            large = large + jnp.where(n >= th, 1, 0)
        bucket = jnp.where(rel > 0, half, 0) + jnp.where(n < max_exact, n, large)
        val = jnp.zeros((t, t), F32)
        for b in range(N_BUCKETS):
            val = jnp.where(bucket == b, rb_ref[b, h], val)
        o_ref[0, d] = val * LOG2E


def _bias_band(rel_bias):
    assert T_ATT >= T5_FAR, "tiles two or more away from the diagonal must lie in the constant-bias region"
    return pl.pallas_call(
        _band_kernel,
        grid=(N_HEADS,),
        in_specs=[pl.BlockSpec(memory_space=pltpu.SMEM)],
        out_specs=pl.BlockSpec((1, N_BAND, T_ATT, T_ATT), lambda h: (h, 0, 0, 0)),
        out_shape=jax.ShapeDtypeStruct((N_HEADS, N_BAND, T_ATT, T_ATT), F32),
        compiler_params=pltpu.CompilerParams(dimension_semantics=("arbitrary",)),
        name="bias_band",
    )(rel_bias.astype(F32))


def _attn_kernel(seq, q_ref, k_ref, v_ref, band_ref, lq1_ref, lk1_ref, lq2_ref, lk2_ref, sg_ref,
                 o_ref, vt_ref, qz_ref, m_ref, acc_ref, s0_ref, s1_ref, c0_ref, c1_ref):
    qi = pl.program_id(2)
    t = T_ATT
    nk = seq // t
    heads = range(G_ATT)

    def cols(g):
        return slice(g * V_DIM, (g + 1) * V_DIM)

    @pl.when(qi == 0)
    def _():
        for g in heads:
            for c in range(nk):
                vt_ref[g, c, 0:V_DIM, :] = v_ref[c * t:(c + 1) * t, cols(g)].astype(F32).T.astype(BF16)
                vt_ref[g, c, V_DIM:V_AUG, :] = jnp.ones((V_AUG - V_DIM, t), BF16)

    for g in heads:
        q = q_ref[:, cols(g)]
        lane = lax.broadcasted_iota(I32, q.shape, 1)
        qz_ref[g, 0:t, :] = jnp.where(lane < HEAD_DIM, q, jnp.zeros_like(q))
        qz_ref[g, t:2 * t, :] = jnp.where(lane >= HEAD_DIM, q, jnp.zeros_like(q))
    m_ref[...] = jnp.full(m_ref.shape, -1e30, F32)
    acc_ref[...] = jnp.zeros(acc_ref.shape, F32)

    def score_tiles(ki, s_ref, c_ref):
        band = jnp.clip(ki - qi + N_BAND // 2, 0, N_BAND - 1)
        for g in heads:
            k_c = k_ref[pl.ds(pl.multiple_of(ki * t, t), t), cols(g)]
            bias = band_ref[g, band]
            s = lax.dot_general(k_c, qz_ref[g], (((1,), (1,)), ((), ())), preferred_element_type=F32)
            s = s + jnp.concatenate([bias, bias], axis=1)
            s_ref[g] = s
            c_ref[g] = jnp.max(s, axis=0, keepdims=True)

    def softmax_pv(ki, s_ref, c_ref):
        for g in heads:
            m_old = m_ref[g]
            m_new = jnp.maximum(m_old, c_ref[g])
            m_ref[g] = m_new
            alpha = jnp.exp2(m_old - m_new)
            p = jnp.exp2(s_ref[g] - m_new).astype(BF16)
            pv = jnp.dot(vt_ref[g, ki], p, preferred_element_type=F32)
            acc_ref[g] = acc_ref[g] * alpha + pv

    score_tiles(0, s0_ref, c0_ref)

    def pairs(j, carry):
        for u in range(KV_PAIRS_PER_ITER):
            k0 = 2 * (j * KV_PAIRS_PER_ITER + u)
            score_tiles(k0 + 1, s1_ref, c1_ref)
            softmax_pv(k0, s0_ref, c0_ref)
            score_tiles(jnp.minimum(k0 + 2, nk - 1), s0_ref, c0_ref)
            softmax_pv(k0 + 1, s1_ref, c1_ref)
        return carry

    lax.fori_loop(0, nk // (2 * KV_PAIRS_PER_ITER), pairs, 0)

    lam = (jnp.exp(jnp.sum(lq1_ref[...] * lk1_ref[...], axis=1, keepdims=True))
           - jnp.exp(jnp.sum(lq2_ref[...] * lk2_ref[...], axis=1, keepdims=True)) + LAM_INIT)
    for g in heads:
        acc = acc_ref[g, 0:V_DIM, :]
        l = acc_ref[g, V_DIM:V_DIM + 1, :]
        o = acc[:, 0:t] / l[:, 0:t] - lam * (acc[:, t:2 * t] / l[:, t:2 * t])
        ms = jnp.mean(o * o, axis=0, keepdims=True)
        on = o * lax.rsqrt(ms + NORM_EPS) * sg_ref[...] * (1.0 - LAM_INIT)
        o_ref[:, cols(g)] = on.T.astype(BF16)


def _attention(proj, band, lam_vecs, subln_g, batch, seq, row_off):
    t = T_ATT
    nq = seq // t
    assert nq % (2 * KV_PAIRS_PER_ITER) == 0, "key tiles are processed in pairs"
    gw = G_ATT * V_DIM
    sec = ATT_W // gw
    lq1, lk1, lq2, lk2 = [v.astype(F32)[None, :] for v in lam_vecs]
    vec_spec = pl.BlockSpec((1, HEAD_DIM), lambda b, h, qi: (0, 0))
    return pl.pallas_call(
        functools.partial(_attn_kernel, seq),
        grid=(batch, N_HEADS // G_ATT, nq),
        in_specs=[
            pl.BlockSpec((t, gw), lambda b, h, qi: (row_off // t + b * nq + qi, h)),
            pl.BlockSpec((seq, gw), lambda b, h, qi: (row_off // seq + b, sec + h)),
            pl.BlockSpec((seq, gw), lambda b, h, qi: (row_off // seq + b, 2 * sec + h)),
            pl.BlockSpec((G_ATT, N_BAND, t, t), lambda b, h, qi: (h, 0, 0, 0)),
            vec_spec, vec_spec, vec_spec, vec_spec,
            pl.BlockSpec((V_DIM, 1), lambda b, h, qi: (0, 0)),
        ],
        out_specs=pl.BlockSpec((t, gw), lambda b, h, qi: (b * nq + qi, h)),
        out_shape=jax.ShapeDtypeStruct((batch * seq, ATT_W), BF16),
        scratch_shapes=[
            pltpu.VMEM((G_ATT, seq // t, V_AUG, t), BF16),
            pltpu.VMEM((G_ATT, 2 * t, V_DIM), BF16),
            pltpu.VMEM((G_ATT, 1, 2 * t), F32),
            pltpu.VMEM((G_ATT, V_AUG, 2 * t), F32),
            pltpu.VMEM((G_ATT, t, 2 * t), F32),
            pltpu.VMEM((G_ATT, t, 2 * t), F32),
            pltpu.VMEM((G_ATT, 1, 2 * t), F32),
            pltpu.VMEM((G_ATT, 1, 2 * t), F32),
        ],
        compiler_params=pltpu.CompilerParams(
            dimension_semantics=("arbitrary", "arbitrary", "arbitrary"), vmem_limit_bytes=VMEM_LIMIT),
        name=f"diff_attention_s{seq}",
    )(proj, proj, proj, band, lq1, lk1, lq2, lk2, subln_g.astype(F32)[:, None])


def _pack_bf16_pairs(x):
    n = x.shape[1] // 2
    lo = lax.bitcast_convert_type(x[:, :n].astype(F32), jnp.uint32)
    hi = lax.bitcast_convert_type(x[:, n:].astype(F32), jnp.uint32)
    return lax.shift_right_logical(lo, jnp.uint32(16)) | (hi & jnp.uint32(0xFFFF0000))


def _unpack_bf16_pairs(w):
    lo = lax.bitcast_convert_type(lax.shift_left(w, jnp.uint32(16)), F32)
    hi = lax.bitcast_convert_type(w & jnp.uint32(0xFFFF0000), F32)
    return jnp.concatenate([lo.astype(BF16), hi.astype(BF16)], axis=1)


def _merge_kernel(n_prompt_tiles, seq_p, seq_s,
                  attp_ref, atts_ref, u_ref, ul_ref, ur_ref, ga_ref, gb_ref, xp_ref, xs_ref,
                  wpg_ref, psc_ref, wao_ref, wpo_ref, wo_ref, g2_ref, wr_ref, br_ref,
                  h_ref, hn_ref, lg_ref):
    i = pl.program_id(0)
    tm = TM_OUT
    is_p = i < n_prompt_tiles
    seq = jnp.where(is_p, seq_p, seq_s)
    t0 = jnp.where(is_p, i, i - n_prompt_tiles) * tm
    pos0 = t0 % seq
    first = pos0 == 0
    last = pos0 + tm == seq

    u = u_ref[...]
    zero_halo = jnp.zeros(ul_ref.shape, BF16)
    ul = jnp.where(first, zero_halo, ul_ref[...])
    ur = jnp.where(last, zero_halo, ur_ref[...])
    uext = jnp.concatenate([ul, u, ur], axis=0)
    r = lax.broadcasted_iota(I32, (tm, tm + 2 * POOL_HALO), 0)
    c = lax.broadcasted_iota(I32, (tm, tm + 2 * POOL_HALO), 1) - POOL_HALO
    pos = pos0 + lax.broadcasted_iota(I32, (tm, 1), 0)
    mixed = []
    for gi, w in enumerate(POOL_WINDOWS):
        sl = slice(gi * POOL_GROUP_W, (gi + 1) * POOL_GROUP_W)
        band = jnp.where(c >= r - w // 2, jnp.where(c < r + (w - w // 2), 1.0, 0.0), 0.0).astype(BF16)
        wsum = jnp.dot(band, uext[:, sl], preferred_element_type=F32)
        cnt = (jnp.minimum(pos + (w - w // 2), seq) - jnp.maximum(pos - w // 2, 0)).astype(F32)
        pooled = wsum / cnt - u[:, sl].astype(F32)
        mg = jnp.dot(pooled.astype(BF16), wpg_ref[gi], preferred_element_type=F32)
        mixed.append((mg * psc_ref[:, sl]).astype(BF16))
    mixed = jnp.concatenate(mixed, axis=1)

    att = jnp.where(is_p, attp_ref[...], atts_ref[...])
    y_a = jnp.dot(att, wao_ref[...], preferred_element_type=F32)
    y_b = jnp.dot(mixed, wpo_ref[...], preferred_element_type=F32)
    merged = ga_ref[...].astype(F32) * y_a + gb_ref[...].astype(F32) * y_b
    x = jnp.where(is_p, xp_ref[...], xs_ref[...])
    h = x + jnp.dot(merged.astype(BF16), wo_ref[...], preferred_element_type=F32)
    h_ref[...] = h
    ms = jnp.mean(h * h, axis=-1, keepdims=True)
    hn = (h * lax.rsqrt(ms + NORM_EPS) * g2_ref[...]).astype(BF16)
    hn_ref[...] = _pack_bf16_pairs(hn)
    lg_ref[...] = jnp.dot(hn, wr_ref[...], preferred_element_type=F32) + br_ref[...]


def _merge_project(att_p, att_s, proj, xp, xs, seq_p, seq_s, w_pool_grp, pool_scale, w_att_out, w_pool_out,
                   w_o, norm2_g, w_router, b_router):
    tm = TM_OUT
    n_p, n_s = xp.shape[0], xs.shape[0]
    n_tok = n_p + n_s
    npt = n_p // tm
    hb = tm // POOL_HALO
    n_hblk = n_tok // POOL_HALO
    u_col = 3 * ATT_W // POOL_W
    const = dict(pipeline_mode=pl.Buffered(1))
    return pl.pallas_call(
        functools.partial(_merge_kernel, npt, seq_p, seq_s),
        grid=(n_tok // tm,),
        in_specs=[
            pl.BlockSpec((tm, ATT_W), lambda i: (jnp.minimum(i, npt - 1), 0)),
            pl.BlockSpec((tm, ATT_W), lambda i: (jnp.maximum(i - npt, 0), 0)),
            pl.BlockSpec((tm, POOL_W), lambda i: (i, u_col)),
            pl.BlockSpec((POOL_HALO, POOL_W), lambda i: (jnp.maximum(i * hb - 1, 0), u_col)),
            pl.BlockSpec((POOL_HALO, POOL_W), lambda i: (jnp.minimum((i + 1) * hb, n_hblk - 1), u_col)),
            pl.BlockSpec((tm, D_MODEL), lambda i: (i, 2)),
            pl.BlockSpec((tm, D_MODEL), lambda i: (i, 3)),
            pl.BlockSpec((tm, D_MODEL), lambda i: (jnp.minimum(i, npt - 1), 0)),
            pl.BlockSpec((tm, D_MODEL), lambda i: (jnp.maximum(i - npt, 0), 0)),
            pl.BlockSpec((len(POOL_WINDOWS), POOL_GROUP_W, POOL_GROUP_W), lambda i: (0, 0, 0), **const),
            pl.BlockSpec((1, POOL_W), lambda i: (0, 0), **const),
            pl.BlockSpec((ATT_W, D_MODEL), lambda i: (0, 0), **const),
            pl.BlockSpec((POOL_W, D_MODEL), lambda i: (0, 0), **const),
            pl.BlockSpec((D_MODEL, D_MODEL), lambda i: (0, 0), **const),
            pl.BlockSpec((1, D_MODEL), lambda i: (0, 0), **const),
            pl.BlockSpec((D_MODEL, N_EXPERTS), lambda i: (0, 0), **const),
            pl.BlockSpec((1, N_EXPERTS), lambda i: (0, 0), **const),
        ],
        out_specs=[
            pl.BlockSpec((tm, D_MODEL), lambda i: (i, 0)),
            pl.BlockSpec((tm, D_MODEL // 2), lambda i: (i, 0)),
            pl.BlockSpec((tm, N_EXPERTS), lambda i: (i, 0)),
        ],
        out_shape=[
            jax.ShapeDtypeStruct((n_tok, D_MODEL), F32),
            jax.ShapeDtypeStruct((n_tok, D_MODEL // 2), jnp.uint32),
            jax.ShapeDtypeStruct((n_tok, N_EXPERTS), F32),
        ],
        compiler_params=pltpu.CompilerParams(dimension_semantics=("arbitrary",), vmem_limit_bytes=VMEM_LIMIT),
        name="merge_project",
    )(att_p, att_s, proj, proj, proj, proj, proj, xp, xs,
      w_pool_grp.astype(BF16), pool_scale.astype(F32)[None, :], w_att_out.astype(BF16),
      w_pool_out.astype(BF16), w_o.astype(BF16), norm2_g.astype(F32)[None, :],
      w_router.astype(BF16), b_router.astype(F32)[None, :])


def _route_kernel(lg_ref, ir_ref, gate_ref, cnt_ref, run_ref):
    i = pl.program_id(0)
    tr = TR_ROUTE

    @pl.when(i == 0)
    def _():
        run_ref[...] = jnp.zeros(run_ref.shape, F32)

    cur = lg_ref[...]
    e_iota = lax.broadcasted_iota(I32, cur.shape, 1).astype(F32)
    member = jnp.zeros(cur.shape, F32)
    vals, idxs = [], []
    for _ in range(TOP_K):
        mx = jnp.max(cur, axis=1, keepdims=True)
        am = jnp.min(jnp.where(cur == mx, e_iota, float(N_EXPERTS)), axis=1, keepdims=True)
        hit = e_iota == am
        vals.append(mx)
        idxs.append(am)
        member = member + jnp.where(hit, 1.0, 0.0)
        cur = jnp.where(hit, -jnp.inf, cur)
    exps = [jnp.exp(v - vals[0]) for v in vals]
    denom = exps[0]
    for e in exps[1:]:
        denom = denom + e

    rr = lax.broadcasted_iota(I32, (tr, tr), 0)
    cc = lax.broadcasted_iota(I32, (tr, tr), 1)
    tri = jnp.where(cc < rr, 1.0, 0.0).astype(BF16)
    before = jnp.dot(tri, member.astype(BF16), preferred_element_type=F32) + run_ref[...]
    run_ref[...] = run_ref[...] + jnp.sum(member, axis=0, keepdims=True)
    cnt_ref[...] = run_ref[...]

    lane = lax.broadcasted_iota(I32, (tr, TOP_K), 1)
    wide = lax.broadcasted_iota(I32, (tr, LANES), 1)
    gate_out = jnp.zeros((tr, TOP_K), F32)
    ir = jnp.zeros((tr, LANES), F32)
    for k in range(TOP_K):
        rk = jnp.sum(jnp.where(e_iota == idxs[k], before, 0.0), axis=1, keepdims=True)
        gate_out = jnp.where(lane == k, exps[k] / denom, gate_out)
        ir = jnp.where(wide == k, idxs[k], jnp.where(wide == TOP_K + k, rk, ir))
    gate_ref[...] = gate_out
    ir_ref[...] = ir.T[0:2 * TOP_K, :].astype(I32)


def _route(logits):
    n_tok = logits.shape[0]
    tr = TR_ROUTE
    return pl.pallas_call(
        _route_kernel,
        grid=(n_tok // tr,),
        in_specs=[pl.BlockSpec((tr, N_EXPERTS), lambda i: (i, 0))],
        out_specs=[
            pl.BlockSpec((2 * TOP_K, tr), lambda i: (0, i)),
            pl.BlockSpec((tr, TOP_K), lambda i: (i, 0)),
            pl.BlockSpec((1, N_EXPERTS), lambda i: (0, 0)),
        ],
        out_shape=[
            jax.ShapeDtypeStruct((2 * TOP_K, n_tok), I32),
            jax.ShapeDtypeStruct((n_tok, TOP_K), F32),
            jax.ShapeDtypeStruct((1, N_EXPERTS), F32),
        ],
        scratch_shapes=[pltpu.VMEM((1, N_EXPERTS), F32)],
        compiler_params=pltpu.CompilerParams(dimension_semantics=("arbitrary",)),
        name="route",
    )(logits)


def _dispatch_kernel(*refs):
    dest_refs, (hn_ref, xs_ref, sem) = refs[:TOP_K], refs[TOP_K:]
    td = TD_DISP

    def issue(i, carry):
        for u in range(ROW_DMA_UNROLL):
            r = i * ROW_DMA_UNROLL + u
            for k in range(TOP_K):
                d = dest_refs[k][r]
                pltpu.make_async_copy(hn_ref.at[pl.ds(r, 1)], xs_ref.at[pl.ds(d, 1)], sem).start(priority=k % 2)
        return carry

    lax.fori_loop(0, td // ROW_DMA_UNROLL, issue, 0)
    for _ in range(TOP_K):
        pltpu.make_async_copy(hn_ref, xs_ref.at[pl.ds(0, td)], sem).wait()


def _dispatch(hn, dest_flat, n_rows):
    n_tok = hn.shape[0]
    td = TD_DISP
    return pl.pallas_call(
        _dispatch_kernel,
        grid=(n_tok // td,),
        in_specs=[pl.BlockSpec((td,), lambda i, k=k: (k * (n_tok // td) + i,), memory_space=pltpu.SMEM)
                  for k in range(TOP_K)] + [
            pl.BlockSpec((td, hn.shape[1]), lambda i: (i, 0)),
        ],
        out_specs=pl.BlockSpec(memory_space=pl.ANY),
        out_shape=jax.ShapeDtypeStruct((n_rows, hn.shape[1]), hn.dtype),
        scratch_shapes=[pltpu.SemaphoreType.DMA(())],
        compiler_params=pltpu.CompilerParams(dimension_semantics=("arbitrary",)),
        name="dispatch",
    )(*([dest_flat] * TOP_K), hn)


def _ffn_kernel(ge_ref, gs_ref, gn_ref, ng_ref, bv_ref,
                xs_hbm, wg_ref, bg_ref, wu_ref, bu_ref, wd_ref, bd_ref,
                ys_hbm, slab_ref, acc_ref, sem_x, sem_o):
    grp = pl.program_id(0)
    f = pl.program_id(1)
    nf = pl.num_programs(1)
    tm = TM_MOE
    n_grp = ng_ref[0]

    def load(g, slot):
        row0 = pl.multiple_of(gs_ref[g] * tm, tm)
        return pltpu.make_async_copy(xs_hbm.at[pl.ds(row0, GT_MOE * tm)], slab_ref.at[slot], sem_x.at[slot])

    def store(g, r):
        row0 = pl.multiple_of((gs_ref[g] + r) * tm, tm)
        return pltpu.make_async_copy(acc_ref.at[pl.ds(r * tm, tm)], ys_hbm.at[pl.ds(row0, tm)], sem_o.at[r])

    @pl.when(grp < n_grp)
    def _():
        start = gs_ref[grp]
        nt = gn_ref[grp]
        slot = grp % 2

        @pl.when(f == 0)
        def _():
            pl.when(grp == 0)(lambda: load(0, 0).start())

            @pl.when(grp > 0)
            def _():
                for r in range(GT_MOE):
                    pl.when((r < gn_ref[grp - 1]) & (r >= nt))(lambda r=r: store(grp - 1, r).wait())

            load(grp, slot).wait()
            pl.when(grp + 1 < n_grp)(lambda: load(grp + 1, 1 - slot).start())

        def row_tiles(tiles):
            @pl.when((f == 0) & (grp > 0))
            def _():
                for r in tiles:
                    pl.when(r < gn_ref[grp - 1])(lambda r=r: store(grp - 1, r).wait())

            wg = wg_ref[0].astype(BF16)
            wu = wu_ref[0].astype(BF16)
            wd = wd_ref[0].astype(BF16)
            row = lax.broadcasted_iota(I32, (tm, 1), 0)
            gates, ups = [], []
            for r in tiles:
                words = jnp.where(row < bv_ref[start + r], slab_ref[slot, r * tm:(r + 1) * tm, :], jnp.uint32(0))
                xb = _unpack_bf16_pairs(words)
                gates.append(jnp.dot(xb, wg, preferred_element_type=F32) + bg_ref[0])
                ups.append(jnp.dot(xb, wu, preferred_element_type=F32) + bu_ref[0])
            hidden = []
            for g, up in zip(gates, ups):
                g = jnp.minimum(g, SWIGLU_LIMIT)
                up = jnp.clip(up, -SWIGLU_LIMIT, SWIGLU_LIMIT)
                hidden.append(((up + 1.0) * (g * jax.nn.sigmoid(SWIGLU_ALPHA * g))).astype(BF16))
            for r, hmid in zip(tiles, hidden):
                rows = slice(r * tm, (r + 1) * tm)
                base = jnp.where(f == 0, jnp.broadcast_to(bd_ref[0], (tm, D_MODEL)), acc_ref[rows, :])
                acc_ref[rows, :] = base + jnp.dot(hmid, wd, preferred_element_type=F32)

            @pl.when(f == nf - 1)
            def _():
                for r in tiles:
                    store(grp, r).start()

        assert GT_MOE == 4
        pl.when(nt >= 2)(lambda: row_tiles((0, 1)))
        pl.when(nt == 1)(lambda: row_tiles((0,)))
        pl.when(nt == 4)(lambda: row_tiles((2, 3)))
        pl.when(nt == 3)(lambda: row_tiles((2,)))

        @pl.when((f == nf - 1) & (grp == n_grp - 1))
        def _():
            for r in range(GT_MOE):
                pl.when(r < nt)(lambda r=r: store(grp, r).wait())


def _expert_ffn(xs, grp_expert, grp_start, grp_ntiles, n_groups, blk_valid,
                w_gate, b_gate, w_up, b_up, w_down, b_down):
    n_rows = xs.shape[0]
    max_groups = grp_expert.shape[0]
    nf = D_FF // TF_MOE

    def expert(g, ge, ng):
        return ge[jnp.minimum(g, ng[0] - 1)]

    def ftile(g, f, ng):
        return jnp.where(g < ng[0], f, nf - 1)

    grid_spec = pltpu.PrefetchScalarGridSpec(
        num_scalar_prefetch=5,
        grid=(max_groups, nf),
        in_specs=[
            pl.BlockSpec(memory_space=pl.ANY),
            pl.BlockSpec((1, D_MODEL, TF_MOE), lambda g, f, ge, gs, gn, ng, bv: (expert(g, ge, ng), 0, ftile(g, f, ng))),
            pl.BlockSpec((1, 1, TF_MOE), lambda g, f, ge, gs, gn, ng, bv: (expert(g, ge, ng), 0, ftile(g, f, ng))),
            pl.BlockSpec((1, D_MODEL, TF_MOE), lambda g, f, ge, gs, gn, ng, bv: (expert(g, ge, ng), 0, ftile(g, f, ng))),
            pl.BlockSpec((1, 1, TF_MOE), lambda g, f, ge, gs, gn, ng, bv: (expert(g, ge, ng), 0, ftile(g, f, ng))),
            pl.BlockSpec((1, TF_MOE, D_MODEL), lambda g, f, ge, gs, gn, ng, bv: (expert(g, ge, ng), ftile(g, f, ng), 0)),
            pl.BlockSpec((1, 1, D_MODEL), lambda g, f, ge, gs, gn, ng, bv: (expert(g, ge, ng), 0, 0)),
        ],
        out_specs=pl.BlockSpec(memory_space=pl.ANY),
        scratch_shapes=[
            pltpu.VMEM((2, GT_MOE * TM_MOE, D_MODEL // 2), jnp.uint32),
            pltpu.VMEM((GT_MOE * TM_MOE, D_MODEL), F32),
            pltpu.SemaphoreType.DMA((2,)),
            pltpu.SemaphoreType.DMA((GT_MOE,)),
        ],
    )
    return pl.pallas_call(
        _ffn_kernel,
        grid_spec=grid_spec,
        out_shape=jax.ShapeDtypeStruct((n_rows, D_MODEL), F32),
        compiler_params=pltpu.CompilerParams(
            dimension_semantics=("arbitrary", "arbitrary"), vmem_limit_bytes=VMEM_LIMIT),
        name="expert_ffn",
    )(grp_expert, grp_start, grp_ntiles, n_groups, blk_valid, xs,
      w_gate, b_gate[:, None, :], w_up, b_up[:, None, :], w_down, b_down[:, None, :])


def _combine_kernel(*refs):
    dest_now, dest_next = refs[:TOP_K], refs[TOP_K:2 * TOP_K]
    h_ref, gate_ref, ys_ref, o_ref, buf_ref, sem = refs[2 * TOP_K:]
    tc = TC_COMB
    i = pl.program_id(0)
    slot = i % 2

    def gather(dest_refs, slot):
        def issue(j, carry):
            for u in range(ROW_DMA_UNROLL):
                r = j * ROW_DMA_UNROLL + u
                for k in range(TOP_K):
                    d = dest_refs[k][r]
                    pltpu.make_async_copy(ys_ref.at[pl.ds(d, 1)], buf_ref.at[slot, k, pl.ds(r, 1)],
                                          sem.at[slot]).start(priority=k % 2)
            return carry

        lax.fori_loop(0, tc // ROW_DMA_UNROLL, issue, 0)

    pl.when(i == 0)(lambda: gather(dest_now, 0))
    pl.when(i + 1 < pl.num_programs(0))(lambda: gather(dest_next, 1 - slot))
    for k in range(TOP_K):
        pltpu.make_async_copy(ys_ref.at[pl.ds(0, tc)], buf_ref.at[slot, k], sem.at[slot]).wait()
    gates = gate_ref[...]
    y = h_ref[...]
    for k in range(TOP_K):
        y = y + gates[:, k:k + 1] * buf_ref[slot, k]
    o_ref[...] = y


def _combine(h, gates, dest_flat, ys, tok_off, n_out):
    tc = TC_COMB
    off = tok_off // tc
    n_all = h.shape[0]
    n_steps = n_out // tc

    def dest_spec(k, ahead):
        return pl.BlockSpec((tc,), lambda i: (k * (n_all // tc) + jnp.minimum(i + ahead, n_steps - 1) + off,),
                            memory_space=pltpu.SMEM)

    return pl.pallas_call(
        _combine_kernel,
        grid=(n_steps,),
        in_specs=[dest_spec(k, 0) for k in range(TOP_K)] + [dest_spec(k, 1) for k in range(TOP_K)] + [
            pl.BlockSpec((tc, D_MODEL), lambda i: (i + off, 0)),
            pl.BlockSpec((tc, TOP_K), lambda i: (i + off, 0)),
            pl.BlockSpec(memory_space=pl.ANY),
        ],
        out_specs=pl.BlockSpec((tc, D_MODEL), lambda i: (i, 0)),
        out_shape=jax.ShapeDtypeStruct((n_out, D_MODEL), F32),
        scratch_shapes=[pltpu.VMEM((2, TOP_K, tc, D_MODEL), F32), pltpu.SemaphoreType.DMA((2,))],
        compiler_params=pltpu.CompilerParams(dimension_semantics=("arbitrary",), vmem_limit_bytes=VMEM_LIMIT),
        name="combine",
    )(*([dest_flat] * (2 * TOP_K)), h, gates, ys)


def kernel(x_prompt, x_sample, norm1_g, w_in, q_norm_g, k_norm_g, lambda_q1, lambda_k1, lambda_q2, lambda_k2,
           subln_g, rel_bias, w_pool_grp, pool_scale, w_att_out, w_pool_out, w_o, norm2_g, w_router, b_router,
           w_gate, b_gate, w_up, b_up, w_down, b_down):
    bp, sp, _ = x_prompt.shape
    bs, ss, _ = x_sample.shape
    n_p, n_s = bp * sp, bs * ss
    n_tok = n_p + n_s
    xp = x_prompt.reshape(n_p, D_MODEL)
    xs = x_sample.reshape(n_s, D_MODEL)

    proj = _in_projection(xp, xs, norm1_g[0], w_in[0].astype(BF16), q_norm_g[0], k_norm_g[0])
    band = _bias_band(rel_bias)
    lam_vecs = (lambda_q1[0], lambda_k1[0], lambda_q2[0], lambda_k2[0])
    att_p = _attention(proj, band, lam_vecs, subln_g[0], bp, sp, 0)
    att_s = _attention(proj, band, lam_vecs, subln_g[0], bs, ss, n_p)
    h, hn, logits = _merge_project(att_p, att_s, proj, xp, xs, sp, ss, w_pool_grp[0], pool_scale[0],
                                   w_att_out[0], w_pool_out[0], w_o[0], norm2_g[0], w_router[0], b_router[0])

    idx_rank, gates, counts = _route(logits)

    n_blocks = (n_tok * TOP_K + N_EXPERTS * (TM_MOE - 1)) // TM_MOE
    counts = counts[0].astype(I32)
    padded = (counts + TM_MOE - 1) // TM_MOE * TM_MOE
    pad_end = jnp.cumsum(padded)
    start_pad = pad_end - padded
    e_ids = jnp.arange(N_EXPERTS, dtype=I32)[:, None, None]
    dest = (jnp.sum(jnp.where(idx_rank[None, :TOP_K] == e_ids, start_pad[:, None, None], 0), axis=0)
            + idx_rank[TOP_K:]).reshape(-1)
    blk_start = jnp.arange(n_blocks, dtype=I32) * TM_MOE
    blk_expert = jnp.minimum(jnp.searchsorted(pad_end, blk_start, side="right"), N_EXPERTS - 1).astype(I32)
    blk_valid = jnp.clip(counts[blk_expert] - (blk_start - start_pad[blk_expert]), 0, TM_MOE).astype(I32)
    e_blocks = padded // TM_MOE
    e_groups = (e_blocks + GT_MOE - 1) // GT_MOE
    grp_end = jnp.cumsum(e_groups)
    max_groups = n_blocks // GT_MOE + N_EXPERTS
    gidx = jnp.arange(max_groups, dtype=I32)
    grp_expert = jnp.minimum(jnp.searchsorted(grp_end, gidx, side="right"), N_EXPERTS - 1).astype(I32)
    in_expert = gidx - (grp_end - e_groups)[grp_expert]
    base = (e_blocks // jnp.maximum(e_groups, 1))[grp_expert]
    rem = e_blocks[grp_expert] - base * e_groups[grp_expert]
    grp_start = (start_pad[grp_expert] // TM_MOE + in_expert * base + jnp.minimum(in_expert, rem)).astype(I32)
    grp_ntiles = (base + (in_expert < rem)).astype(I32)
    n_groups = grp_end[-1:].astype(I32)

    n_rows = (n_blocks + GT_MOE - 1) * TM_MOE
    xs_sorted = _dispatch(hn, dest, n_rows)
    ys = _expert_ffn(xs_sorted, grp_expert, grp_start, grp_ntiles, n_groups, blk_valid,
                     w_gate[0], b_gate[0], w_up[0], b_up[0], w_down[0], b_down[0])
    y_p = _combine(h, gates, dest, ys, 0, n_p)
    y_s = _combine(h, gates, dest, ys, n_p, n_s)
    return (y_p.reshape(bp, sp, D_MODEL), y_s.reshape(bs, ss, D_MODEL))
```

```python
import functools
import math

import jax
import jax.numpy as jnp
from jax import lax
from jax.experimental import pallas as pl
from jax.experimental.pallas import tpu as pltpu

F32 = jnp.float32
BF16 = jnp.bfloat16
I32 = jnp.int32

D_MODEL = 2048
N_HEADS = 8
HEAD_DIM = 64
V_DIM = 2 * HEAD_DIM
ATT_W = N_HEADS * V_DIM
POOL_WINDOWS = (2, 4, 8, 16)
POOL_W = D_MODEL // 2
POOL_GROUP_W = POOL_W // len(POOL_WINDOWS)
IN_COLS = 3 * ATT_W + POOL_W + 2 * D_MODEL
N_BUCKETS = 32
MAX_DISTANCE = 128
N_EXPERTS = 32
TOP_K = 4
D_FF = D_MODEL
SWIGLU_LIMIT = 7.0
SWIGLU_ALPHA = 1.702
NORM_EPS = 1e-6
LAM_INIT = 0.8 - 0.6 * math.exp(-0.3 * 0)
LOG2E = math.log2(math.e)

LANES = 128
VMEM_LIMIT = 56 * 1024 * 1024

TM_IN = 512
TN_IN = 2048
T_ATT = 256
G_ATT = 4
KV_PAIRS_PER_ITER = 4
V_AUG = V_DIM + 16
TM_OUT = 256
POOL_HALO = 16
TR_ROUTE = 512
TM_MOE = 528
TF_MOE = 256
GT_MOE = 4
TD_DISP = 256
TC_COMB = 256
ROW_DMA_UNROLL = 8


def _t5_thresholds():
    half = N_BUCKETS // 2
    max_exact = half // 2
    steps = half - max_exact
    ratio = MAX_DISTANCE // max_exact
    out = []
    for k in range(1, steps):
        n = max_exact
        while n ** steps < (max_exact ** steps) * (ratio ** k):
            n += 1
        out.append(n)
    return tuple(out)


T5_THRESHOLDS = _t5_thresholds()
T5_FAR = T5_THRESHOLDS[-1]


def _inproj_kernel(n_prompt_tiles, xp_ref, xs_ref, g1_ref, w_ref, gsum_ref, qg_ref, kg_ref, o_ref, xn_ref):
    i = pl.program_id(0)
    j = pl.program_id(1)

    @pl.when(j == 0)
    def _():
        x = jnp.where(i < n_prompt_tiles, xp_ref[...], xs_ref[...])
        ms = jnp.mean(x * x, axis=-1, keepdims=True)
        xn_ref[...] = (x * lax.rsqrt(ms + NORM_EPS) * g1_ref[...]).astype(BF16)

    half = TN_IN // 2

    def sections():
        xn = xn_ref[...]
        return [jnp.dot(xn, w_ref[:, c * half:(c + 1) * half], preferred_element_type=F32) for c in range(2)]

    def head_norm(acc, g_ref, scale):
        sq = (acc * acc).astype(BF16)
        gw = gsum_ref.shape[0]
        parts = [jnp.dot(sq[:, c * gw:(c + 1) * gw], gsum_ref[...], preferred_element_type=F32)
                 for c in range(half // gw)]
        ss = jnp.concatenate(parts, axis=1)
        return acc * lax.rsqrt(ss * (1.0 / HEAD_DIM) + NORM_EPS) * (g_ref[...] * scale)

    @pl.when(j == 0)
    def _():
        q, k = sections()
        o_ref[:, 0:half] = head_norm(q, qg_ref, HEAD_DIM ** -0.5 * LOG2E).astype(BF16)
        o_ref[:, half:TN_IN] = head_norm(k, kg_ref, 1.0).astype(BF16)

    @pl.when(j == 1)
    def _():
        for c, acc in enumerate(sections()):
            o_ref[:, c * half:(c + 1) * half] = acc.astype(BF16)

    @pl.when(j >= 2)
    def _():
        for c, acc in enumerate(sections()):
            o_ref[:, c * half:(c + 1) * half] = (0.5 * jnp.tanh(0.5 * acc) + 0.5).astype(BF16)


def _in_projection(xp, xs, norm1_g, w_in_bf, q_norm_g, k_norm_g):
    n_p, n_s = xp.shape[0], xs.shape[0]
    n_tok = n_p + n_s
    npt, nst = n_p // TM_IN, n_s // TM_IN
    gw = 2 * LANES
    gid = jnp.arange(gw) // HEAD_DIM
    gsum = (gid[:, None] == gid[None, :]).astype(BF16)
    assert TN_IN // 2 == ATT_W == POOL_W and D_MODEL % TN_IN == 0, "one q/k/v/u section per half column tile"
    qg = jnp.tile(q_norm_g.astype(F32), ATT_W // HEAD_DIM)[None, :]
    kg = jnp.tile(k_norm_g.astype(F32), ATT_W // HEAD_DIM)[None, :]
    return pl.pallas_call(
        functools.partial(_inproj_kernel, npt),
        grid=(n_tok // TM_IN, IN_COLS // TN_IN),
        in_specs=[
            pl.BlockSpec((TM_IN, D_MODEL), lambda i, j: (jnp.minimum(i, npt - 1), 0)),
            pl.BlockSpec((TM_IN, D_MODEL), lambda i, j: (jnp.maximum(i - npt, 0), 0)),
            pl.BlockSpec((1, D_MODEL), lambda i, j: (0, 0)),
            pl.BlockSpec((D_MODEL, TN_IN), lambda i, j: (0, j)),
            pl.BlockSpec((gw, gw), lambda i, j: (0, 0)),
            pl.BlockSpec((1, ATT_W), lambda i, j: (0, 0)),
            pl.BlockSpec((1, ATT_W), lambda i, j: (0, 0)),
        ],
        out_specs=pl.BlockSpec((TM_IN, TN_IN), lambda i, j: (i, j)),
        out_shape=jax.ShapeDtypeStruct((n_tok, IN_COLS), BF16),
        scratch_shapes=[pltpu.VMEM((TM_IN, D_MODEL), BF16)],
        compiler_params=pltpu.CompilerParams(
            dimension_semantics=("arbitrary", "arbitrary"), vmem_limit_bytes=VMEM_LIMIT),
        name="in_projection",
    )(xp, xs, norm1_g.astype(F32)[None, :], w_in_bf, gsum, qg, kg)


N_BAND = 5


def _band_kernel(rb_ref, o_ref):
    h = pl.program_id(0)
    t = T_ATT
    kj = lax.broadcasted_iota(I32, (t, t), 0)
    qi = lax.broadcasted_iota(I32, (t, t), 1)
    half = N_BUCKETS // 2
    max_exact = half // 2
    for d in range(N_BAND):
        rel = (d - N_BAND // 2) * t + kj - qi
        n = jnp.abs(rel)
        large = jnp.full((t, t), max_exact, I32)
        for th in T5_THRESHOLDS:
            large = large + jnp.where(n >= th, 1, 0)
        bucket = jnp.where(rel > 0, half, 0) + jnp.where(n < max_exact, n, large)
        val = jnp.zeros((t, t), F32)
        for b in range(N_BUCKETS):
            val = jnp.where(bucket == b, rb_ref[b, h], val)
        o_ref[0, d] = val * LOG2E


def _bias_band(rel_bias):
    assert T_ATT >= T5_FAR, "tiles two or more away from the diagonal must lie in the constant-bias region"
    return pl.pallas_call(
        _band_kernel,
        grid=(N_HEADS,),
        in_specs=[pl.BlockSpec(memory_space=pltpu.SMEM)],
        out_specs=pl.BlockSpec((1, N_BAND, T_ATT, T_ATT), lambda h: (h, 0, 0, 0)),
        out_shape=jax.ShapeDtypeStruct((N_HEADS, N_BAND, T_ATT, T_ATT), F32),
        compiler_params=pltpu.CompilerParams(dimension_semantics=("arbitrary",)),
        name="bias_band",
    )(rel_bias.astype(F32))


def _attn_kernel(seq, q_ref, k_ref, v_ref, band_ref, lq1_ref, lk1_ref, lq2_ref, lk2_ref, sg_ref,
                 o_ref, vt_ref, qz_ref, m_ref, acc_ref, s0_ref, s1_ref, c0_ref, c1_ref):
    qi = pl.program_id(2)
    t = T_ATT
    nk = seq // t
    heads = range(G_ATT)

    def cols(g):
        return slice(g * V_DIM, (g + 1) * V_DIM)

    @pl.when(qi == 0)
    def _():
        for g in heads:
            for c in range(nk):
                vt_ref[g, c, 0:V_DIM, :] = v_ref[c * t:(c + 1) * t, cols(g)].astype(F32).T.astype(BF16)
                vt_ref[g, c, V_DIM:V_AUG, :] = jnp.ones((V_AUG - V_DIM, t), BF16)

    for g in heads:
        q = q_ref[:, cols(g)]
        lane = lax.broadcasted_iota(I32, q.shape, 1)
        qz_ref[g, 0:t, :] = jnp.where(lane < HEAD_DIM, q, jnp.zeros_like(q))
        qz_ref[g, t:2 * t, :] = jnp.where(lane >= HEAD_DIM, q, jnp.zeros_like(q))
    m_ref[...] = jnp.full(m_ref.shape, -1e30, F32)
    acc_ref[...] = jnp.zeros(acc_ref.shape, F32)

    def score_tiles(ki, s_ref, c_ref):
        band = jnp.clip(ki - qi + N_BAND // 2, 0, N_BAND - 1)
        for g in heads:
            k_c = k_ref[pl.ds(pl.multiple_of(ki * t, t), t), cols(g)]
            bias = band_ref[g, band]
            s = lax.dot_general(k_c, qz_ref[g], (((1,), (1,)), ((), ())), preferred_element_type=F32)
            s = s + jnp.concatenate([bias, bias], axis=1)
            s_ref[g] = s
            c_ref[g] = jnp.max(s, axis=0, keepdims=True)

    def softmax_pv(ki, s_ref, c_ref):
        for g in heads:
            m_old = m_ref[g]
            m_new = jnp.maximum(m_old, c_ref[g])
            m_ref[g] = m_new
            alpha = jnp.exp2(m_old - m_new)
            p = jnp.exp2(s_ref[g] - m_new).astype(BF16)
            pv = jnp.dot(vt_ref[g, ki], p, preferred_element_type=F32)
            acc_ref[g] = acc_ref[g] * alpha + pv

    score_tiles(0, s0_ref, c0_ref)

    def pairs(j, carry):
        for u in range(KV_PAIRS_PER_ITER):
            k0 = 2 * (j * KV_PAIRS_PER_ITER + u)
            score_tiles(k0 + 1, s1_ref, c1_ref)
            softmax_pv(k0, s0_ref, c0_ref)
            score_tiles(jnp.minimum(k0 + 2, nk - 1), s0_ref, c0_ref)
            softmax_pv(k0 + 1, s1_ref, c1_ref)
        return carry

    lax.fori_loop(0, nk // (2 * KV_PAIRS_PER_ITER), pairs, 0)

    lam = (jnp.exp(jnp.sum(lq1_ref[...] * lk1_ref[...], axis=1, keepdims=True))
           - jnp.exp(jnp.sum(lq2_ref[...] * lk2_ref[...], axis=1, keepdims=True)) + LAM_INIT)
    for g in heads:
        acc = acc_ref[g, 0:V_DIM, :]
        l = acc_ref[g, V_DIM:V_DIM + 1, :]
        o = acc[:, 0:t] / l[:, 0:t] - lam * (acc[:, t:2 * t] / l[:, t:2 * t])
        ms = jnp.mean(o * o, axis=0, keepdims=True)
        on = o * lax.rsqrt(ms + NORM_EPS) * sg_ref[...] * (1.0 - LAM_INIT)
        o_ref[:, cols(g)] = on.T.astype(BF16)


def _attention(proj, band, lam_vecs, subln_g, batch, seq, row_off):
    t = T_ATT
    nq = seq // t
    assert nq % (2 * KV_PAIRS_PER_ITER) == 0, "key tiles are processed in pairs"
    gw = G_ATT * V_DIM
    sec = ATT_W // gw
    lq1, lk1, lq2, lk2 = [v.astype(F32)[None, :] for v in lam_vecs]
    vec_spec = pl.BlockSpec((1, HEAD_DIM), lambda b, h, qi: (0, 0))
    return pl.pallas_call(
        functools.partial(_attn_kernel, seq),
        grid=(batch, N_HEADS // G_ATT, nq),
        in_specs=[
            pl.BlockSpec((t, gw), lambda b, h, qi: (row_off // t + b * nq + qi, h)),
            pl.BlockSpec((seq, gw), lambda b, h, qi: (row_off // seq + b, sec + h)),
            pl.BlockSpec((seq, gw), lambda b, h, qi: (row_off // seq + b, 2 * sec + h)),
            pl.BlockSpec((G_ATT, N_BAND, t, t), lambda b, h, qi: (h, 0, 0, 0)),
            vec_spec, vec_spec, vec_spec, vec_spec,
            pl.BlockSpec((V_DIM, 1), lambda b, h, qi: (0, 0)),
        ],
        out_specs=pl.BlockSpec((t, gw), lambda b, h, qi: (b * nq + qi, h)),
        out_shape=jax.ShapeDtypeStruct((batch * seq, ATT_W), BF16),
        scratch_shapes=[
            pltpu.VMEM((G_ATT, seq // t, V_AUG, t), BF16),
            pltpu.VMEM((G_ATT, 2 * t, V_DIM), BF16),
            pltpu.VMEM((G_ATT, 1, 2 * t), F32),
            pltpu.VMEM((G_ATT, V_AUG, 2 * t), F32),
            pltpu.VMEM((G_ATT, t, 2 * t), F32),
            pltpu.VMEM((G_ATT, t, 2 * t), F32),
            pltpu.VMEM((G_ATT, 1, 2 * t), F32),
            pltpu.VMEM((G_ATT, 1, 2 * t), F32),
        ],
        compiler_params=pltpu.CompilerParams(
            dimension_semantics=("arbitrary", "arbitrary", "arbitrary"), vmem_limit_bytes=VMEM_LIMIT),
        name=f"diff_attention_s{seq}",
    )(proj, proj, proj, band, lq1, lk1, lq2, lk2, subln_g.astype(F32)[:, None])


def _pack_bf16_pairs(x):
    n = x.shape[1] // 2
    lo = lax.bitcast_convert_type(x[:, :n].astype(F32), jnp.uint32)
    hi = lax.bitcast_convert_type(x[:, n:].astype(F32), jnp.uint32)
    return lax.shift_right_logical(lo, jnp.uint32(16)) | (hi & jnp.uint32(0xFFFF0000))


def _unpack_bf16_pairs(w):
    lo = lax.bitcast_convert_type(lax.shift_left(w, jnp.uint32(16)), F32)
    hi = lax.bitcast_convert_type(w & jnp.uint32(0xFFFF0000), F32)
    return jnp.concatenate([lo.astype(BF16), hi.astype(BF16)], axis=1)


def _merge_kernel(n_prompt_tiles, seq_p, seq_s,
                  attp_ref, atts_ref, u_ref, ul_ref, ur_ref, ga_ref, gb_ref, xp_ref, xs_ref,
                  wpg_ref, psc_ref, wao_ref, wpo_ref, wo_ref, g2_ref, wr_ref, br_ref,
                  h_ref, hn_ref, lg_ref):
    i = pl.program_id(0)
    tm = TM_OUT
    is_p = i < n_prompt_tiles
    seq = jnp.where(is_p, seq_p, seq_s)
    t0 = jnp.where(is_p, i, i - n_prompt_tiles) * tm
    pos0 = t0 % seq
    first = pos0 == 0
    last = pos0 + tm == seq

    u = u_ref[...]
    zero_halo = jnp.zeros(ul_ref.shape, BF16)
    ul = jnp.where(first, zero_halo, ul_ref[...])
    ur = jnp.where(last, zero_halo, ur_ref[...])
    uext = jnp.concatenate([ul, u, ur], axis=0)
    r = lax.broadcasted_iota(I32, (tm, tm + 2 * POOL_HALO), 0)
    c = lax.broadcasted_iota(I32, (tm, tm + 2 * POOL_HALO), 1) - POOL_HALO
    pos = pos0 + lax.broadcasted_iota(I32, (tm, 1), 0)
    mixed = []
    for gi, w in enumerate(POOL_WINDOWS):
        sl = slice(gi * POOL_GROUP_W, (gi + 1) * POOL_GROUP_W)
        band = jnp.where(c >= r - w // 2, jnp.where(c < r + (w - w // 2), 1.0, 0.0), 0.0).astype(BF16)
        wsum = jnp.dot(band, uext[:, sl], preferred_element_type=F32)
        cnt = (jnp.minimum(pos + (w - w // 2), seq) - jnp.maximum(pos - w // 2, 0)).astype(F32)
        pooled = wsum / cnt - u[:, sl].astype(F32)
        mg = jnp.dot(pooled.astype(BF16), wpg_ref[gi], preferred_element_type=F32)
        mixed.append((mg * psc_ref[:, sl]).astype(BF16))
    mixed = jnp.concatenate(mixed, axis=1)

    att = jnp.where(is_p, attp_ref[...], atts_ref[...])
    y_a = jnp.dot(att, wao_ref[...], preferred_element_type=F32)
    y_b = jnp.dot(mixed, wpo_ref[...], preferred_element_type=F32)
    merged = ga_ref[...].astype(F32) * y_a + gb_ref[...].astype(F32) * y_b
    x = jnp.where(is_p, xp_ref[...], xs_ref[...])
    h = x + jnp.dot(merged.astype(BF16), wo_ref[...], preferred_element_type=F32)
    h_ref[...] = h
    ms = jnp.mean(h * h, axis=-1, keepdims=True)
    hn = (h * lax.rsqrt(ms + NORM_EPS) * g2_ref[...]).astype(BF16)
    hn_ref[...] = _pack_bf16_pairs(hn)
    lg_ref[...] = jnp.dot(hn, wr_ref[...], preferred_element_type=F32) + br_ref[...]


def _merge_project(att_p, att_s, proj, xp, xs, seq_p, seq_s, w_pool_grp, pool_scale, w_att_out, w_pool_out,
                   w_o, norm2_g, w_router, b_router):
    tm = TM_OUT
    n_p, n_s = xp.shape[0], xs.shape[0]
    n_tok = n_p + n_s
    npt = n_p // tm
    hb = tm // POOL_HALO
    n_hblk = n_tok // POOL_HALO
    u_col = 3 * ATT_W // POOL_W
    const = dict(pipeline_mode=pl.Buffered(1))
    return pl.pallas_call(
        functools.partial(_merge_kernel, npt, seq_p, seq_s),
        grid=(n_tok // tm,),
        in_specs=[
            pl.BlockSpec((tm, ATT_W), lambda i: (jnp.minimum(i, npt - 1), 0)),
            pl.BlockSpec((tm, ATT_W), lambda i: (jnp.maximum(i - npt, 0), 0)),
            pl.BlockSpec((tm, POOL_W), lambda i: (i, u_col)),
            pl.BlockSpec((POOL_HALO, POOL_W), lambda i: (jnp.maximum(i * hb - 1, 0), u_col)),
            pl.BlockSpec((POOL_HALO, POOL_W), lambda i: (jnp.minimum((i + 1) * hb, n_hblk - 1), u_col)),
            pl.BlockSpec((tm, D_MODEL), lambda i: (i, 2)),
            pl.BlockSpec((tm, D_MODEL), lambda i: (i, 3)),
            pl.BlockSpec((tm, D_MODEL), lambda i: (jnp.minimum(i, npt - 1), 0)),
            pl.BlockSpec((tm, D_MODEL), lambda i: (jnp.maximum(i - npt, 0), 0)),
            pl.BlockSpec((len(POOL_WINDOWS), POOL_GROUP_W, POOL_GROUP_W), lambda i: (0, 0, 0), **const),
            pl.BlockSpec((1, POOL_W), lambda i: (0, 0), **const),
            pl.BlockSpec((ATT_W, D_MODEL), lambda i: (0, 0), **const),
            pl.BlockSpec((POOL_W, D_MODEL), lambda i: (0, 0), **const),
            pl.BlockSpec((D_MODEL, D_MODEL), lambda i: (0, 0), **const),
            pl.BlockSpec((1, D_MODEL), lambda i: (0, 0), **const),
            pl.BlockSpec((D_MODEL, N_EXPERTS), lambda i: (0, 0), **const),
            pl.BlockSpec((1, N_EXPERTS), lambda i: (0, 0), **const),
        ],
        out_specs=[
            pl.BlockSpec((tm, D_MODEL), lambda i: (i, 0)),
            pl.BlockSpec((tm, D_MODEL // 2), lambda i: (i, 0)),
            pl.BlockSpec((tm, N_EXPERTS), lambda i: (i, 0)),
        ],
        out_shape=[
            jax.ShapeDtypeStruct((n_tok, D_MODEL), F32),
            jax.ShapeDtypeStruct((n_tok, D_MODEL // 2), jnp.uint32),
            jax.ShapeDtypeStruct((n_tok, N_EXPERTS), F32),
        ],
        compiler_params=pltpu.CompilerParams(dimension_semantics=("arbitrary",), vmem_limit_bytes=VMEM_LIMIT),
        name="merge_project",
    )(att_p, att_s, proj, proj, proj, proj, proj, xp, xs,
      w_pool_grp.astype(BF16), pool_scale.astype(F32)[None, :], w_att_out.astype(BF16),
      w_pool_out.astype(BF16), w_o.astype(BF16), norm2_g.astype(F32)[None, :],
      w_router.astype(BF16), b_router.astype(F32)[None, :])


def _route_kernel(lg_ref, ir_ref, gate_ref, cnt_ref, run_ref):
    i = pl.program_id(0)
    tr = TR_ROUTE

    @pl.when(i == 0)
    def _():
        run_ref[...] = jnp.zeros(run_ref.shape, F32)

    cur = lg_ref[...]
    e_iota = lax.broadcasted_iota(I32, cur.shape, 1).astype(F32)
    member = jnp.zeros(cur.shape, F32)
    vals, idxs = [], []
    for _ in range(TOP_K):
        mx = jnp.max(cur, axis=1, keepdims=True)
        am = jnp.min(jnp.where(cur == mx, e_iota, float(N_EXPERTS)), axis=1, keepdims=True)
        hit = e_iota == am
        vals.append(mx)
        idxs.append(am)
        member = member + jnp.where(hit, 1.0, 0.0)
        cur = jnp.where(hit, -jnp.inf, cur)
    exps = [jnp.exp(v - vals[0]) for v in vals]
    denom = exps[0]
    for e in exps[1:]:
        denom = denom + e

    rr = lax.broadcasted_iota(I32, (tr, tr), 0)
    cc = lax.broadcasted_iota(I32, (tr, tr), 1)
    tri = jnp.where(cc < rr, 1.0, 0.0).astype(BF16)
    before = jnp.dot(tri, member.astype(BF16), preferred_element_type=F32) + run_ref[...]
    run_ref[...] = run_ref[...] + jnp.sum(member, axis=0, keepdims=True)
    cnt_ref[...] = run_ref[...]

    lane = lax.broadcasted_iota(I32, (tr, TOP_K), 1)
    wide = lax.broadcasted_iota(I32, (tr, LANES), 1)
    gate_out = jnp.zeros((tr, TOP_K), F32)
    ir = jnp.zeros((tr, LANES), F32)
    for k in range(TOP_K):
        rk = jnp.sum(jnp.where(e_iota == idxs[k], before, 0.0), axis=1, keepdims=True)
        gate_out = jnp.where(lane == k, exps[k] / denom, gate_out)
        ir = jnp.where(wide == k, idxs[k], jnp.where(wide == TOP_K + k, rk, ir))
    gate_ref[...] = gate_out
    ir_ref[...] = ir.T[0:2 * TOP_K, :].astype(I32)


def _route(logits):
    n_tok = logits.shape[0]
    tr = TR_ROUTE
    return pl.pallas_call(
        _route_kernel,
        grid=(n_tok // tr,),
        in_specs=[pl.BlockSpec((tr, N_EXPERTS), lambda i: (i, 0))],
        out_specs=[
            pl.BlockSpec((2 * TOP_K, tr), lambda i: (0, i)),
            pl.BlockSpec((tr, TOP_K), lambda i: (i, 0)),
            pl.BlockSpec((1, N_EXPERTS), lambda i: (0, 0)),
        ],
        out_shape=[
            jax.ShapeDtypeStruct((2 * TOP_K, n_tok), I32),
            jax.ShapeDtypeStruct((n_tok, TOP_K), F32),
            jax.ShapeDtypeStruct((1, N_EXPERTS), F32),
        ],
        scratch_shapes=[pltpu.VMEM((1, N_EXPERTS), F32)],
        compiler_params=pltpu.CompilerParams(dimension_semantics=("arbitrary",)),
        name="route",
    )(logits)


def _dispatch_kernel(*refs):
    dest_refs, (hn_ref, xs_ref, sem) = refs[:TOP_K], refs[TOP_K:]
    td = TD_DISP

    def issue(i, carry):
        for u in range(ROW_DMA_UNROLL):
            r = i * ROW_DMA_UNROLL + u
            for k in range(TOP_K):
                d = dest_refs[k][r]
                pltpu.make_async_copy(hn_ref.at[pl.ds(r, 1)], xs_ref.at[pl.ds(d, 1)], sem).start(priority=k % 2)
        return carry

    lax.fori_loop(0, td // ROW_DMA_UNROLL, issue, 0)
    for _ in range(TOP_K):
        pltpu.make_async_copy(hn_ref, xs_ref.at[pl.ds(0, td)], sem).wait()


def _dispatch(hn, dest_flat, n_rows):
    n_tok = hn.shape[0]
    td = TD_DISP
    return pl.pallas_call(
        _dispatch_kernel,
        grid=(n_tok // td,),
        in_specs=[pl.BlockSpec((td,), lambda i, k=k: (k * (n_tok // td) + i,), memory_space=pltpu.SMEM)
                  for k in range(TOP_K)] + [
            pl.BlockSpec((td, hn.shape[1]), lambda i: (i, 0)),
        ],
        out_specs=pl.BlockSpec(memory_space=pl.ANY),
        out_shape=jax.ShapeDtypeStruct((n_rows, hn.shape[1]), hn.dtype),
        scratch_shapes=[pltpu.SemaphoreType.DMA(())],
        compiler_params=pltpu.CompilerParams(dimension_semantics=("arbitrary",)),
        name="dispatch",
    )(*([dest_flat] * TOP_K), hn)


def _ffn_kernel(ge_ref, gs_ref, gn_ref, ng_ref, bv_ref,
                xs_hbm, wg_ref, bg_ref, wu_ref, bu_ref, wd_ref, bd_ref,
                ys_hbm, slab_ref, acc_ref, sem_x, sem_o):
    grp = pl.program_id(0)
    f = pl.program_id(1)
    nf = pl.num_programs(1)
    tm = TM_MOE
    n_grp = ng_ref[0]

    def load(g, slot):
        row0 = pl.multiple_of(gs_ref[g] * tm, tm)
        return pltpu.make_async_copy(xs_hbm.at[pl.ds(row0, GT_MOE * tm)], slab_ref.at[slot], sem_x.at[slot])

    def store(g, r):
        row0 = pl.multiple_of((gs_ref[g] + r) * tm, tm)
        return pltpu.make_async_copy(acc_ref.at[pl.ds(r * tm, tm)], ys_hbm.at[pl.ds(row0, tm)], sem_o.at[r])

    @pl.when(grp < n_grp)
    def _():
        start = gs_ref[grp]
        nt = gn_ref[grp]
        slot = grp % 2

        @pl.when(f == 0)
        def _():
            pl.when(grp == 0)(lambda: load(0, 0).start())

            @pl.when(grp > 0)
            def _():
                for r in range(GT_MOE):
                    pl.when((r < gn_ref[grp - 1]) & (r >= nt))(lambda r=r: store(grp - 1, r).wait())

            load(grp, slot).wait()
            pl.when(grp + 1 < n_grp)(lambda: load(grp + 1, 1 - slot).start())

        def row_tiles(tiles):
            @pl.when((f == 0) & (grp > 0))
            def _():
                for r in tiles:
                    pl.when(r < gn_ref[grp - 1])(lambda r=r: store(grp - 1, r).wait())

            wg = wg_ref[0].astype(BF16)
            wu = wu_ref[0].astype(BF16)
            wd = wd_ref[0].astype(BF16)
            row = lax.broadcasted_iota(I32, (tm, 1), 0)
            gates, ups = [], []
            for r in tiles:
                words = jnp.where(row < bv_ref[start + r], slab_ref[slot, r * tm:(r + 1) * tm, :], jnp.uint32(0))
                xb = _unpack_bf16_pairs(words)
                gates.append(jnp.dot(xb, wg, preferred_element_type=F32) + bg_ref[0])
                ups.append(jnp.dot(xb, wu, preferred_element_type=F32) + bu_ref[0])
            hidden = []
            for g, up in zip(gates, ups):
                g = jnp.minimum(g, SWIGLU_LIMIT)
                up = jnp.clip(up, -SWIGLU_LIMIT, SWIGLU_LIMIT)
                hidden.append(((up + 1.0) * (g * jax.nn.sigmoid(SWIGLU_ALPHA * g))).astype(BF16))
            for r, hmid in zip(tiles, hidden):
                rows = slice(r * tm, (r + 1) * tm)
                base = jnp.where(f == 0, jnp.broadcast_to(bd_ref[0], (tm, D_MODEL)), acc_ref[rows, :])
                acc_ref[rows, :] = base + jnp.dot(hmid, wd, preferred_element_type=F32)

            @pl.when(f == nf - 1)
            def _():
                for r in tiles:
                    store(grp, r).start()

        assert GT_MOE == 4
        pl.when(nt >= 2)(lambda: row_tiles((0, 1)))
        pl.when(nt == 1)(lambda: row_tiles((0,)))
        pl.when(nt == 4)(lambda: row_tiles((2, 3)))
        pl.when(nt == 3)(lambda: row_tiles((2,)))

        @pl.when((f == nf - 1) & (grp == n_grp - 1))
        def _():
            for r in range(GT_MOE):
                pl.when(r < nt)(lambda r=r: store(grp, r).wait())


def _expert_ffn(xs, grp_expert, grp_start, grp_ntiles, n_groups, blk_valid,
                w_gate, b_gate, w_up, b_up, w_down, b_down):
    n_rows = xs.shape[0]
    max_groups = grp_expert.shape[0]
    nf = D_FF // TF_MOE

    def expert(g, ge, ng):
        return ge[jnp.minimum(g, ng[0] - 1)]

    def ftile(g, f, ng):
        return jnp.where(g < ng[0], f, nf - 1)

    grid_spec = pltpu.PrefetchScalarGridSpec(
        num_scalar_prefetch=5,
        grid=(max_groups, nf),
        in_specs=[
            pl.BlockSpec(memory_space=pl.ANY),
            pl.BlockSpec((1, D_MODEL, TF_MOE), lambda g, f, ge, gs, gn, ng, bv: (expert(g, ge, ng), 0, ftile(g, f, ng))),
            pl.BlockSpec((1, 1, TF_MOE), lambda g, f, ge, gs, gn, ng, bv: (expert(g, ge, ng), 0, ftile(g, f, ng))),
            pl.BlockSpec((1, D_MODEL, TF_MOE), lambda g, f, ge, gs, gn, ng, bv: (expert(g, ge, ng), 0, ftile(g, f, ng))),
            pl.BlockSpec((1, 1, TF_MOE), lambda g, f, ge, gs, gn, ng, bv: (expert(g, ge, ng), 0, ftile(g, f, ng))),
            pl.BlockSpec((1, TF_MOE, D_MODEL), lambda g, f, ge, gs, gn, ng, bv: (expert(g, ge, ng), ftile(g, f, ng), 0)),
            pl.BlockSpec((1, 1, D_MODEL), lambda g, f, ge, gs, gn, ng, bv: (expert(g, ge, ng), 0, 0)),
        ],
        out_specs=pl.BlockSpec(memory_space=pl.ANY),
        scratch_shapes=[
            pltpu.VMEM((2, GT_MOE * TM_MOE, D_MODEL // 2), jnp.uint32),
            pltpu.VMEM((GT_MOE * TM_MOE, D_MODEL), F32),
            pltpu.SemaphoreType.DMA((2,)),
            pltpu.SemaphoreType.DMA((GT_MOE,)),
        ],
    )
    return pl.pallas_call(
        _ffn_kernel,
        grid_spec=grid_spec,
        out_shape=jax.ShapeDtypeStruct((n_rows, D_MODEL), F32),
        compiler_params=pltpu.CompilerParams(
            dimension_semantics=("arbitrary", "arbitrary"), vmem_limit_bytes=VMEM_LIMIT),
        name="expert_ffn",
    )(grp_expert, grp_start, grp_ntiles, n_groups, blk_valid, xs,
      w_gate, b_gate[:, None, :], w_up, b_up[:, None, :], w_down, b_down[:, None, :])


def _combine_kernel(*refs):
    dest_now, dest_next = refs[:TOP_K], refs[TOP_K:2 * TOP_K]
    h_ref, gate_ref, ys_ref, o_ref, buf_ref, sem = refs[2 * TOP_K:]
    tc = TC_COMB
    i = pl.program_id(0)
    slot = i % 2

    def gather(dest_refs, slot):
        def issue(j, carry):
            for u in range(ROW_DMA_UNROLL):
                r = j * ROW_DMA_UNROLL + u
                for k in range(TOP_K):
                    d = dest_refs[k][r]
                    pltpu.make_async_copy(ys_ref.at[pl.ds(d, 1)], buf_ref.at[slot, k, pl.ds(r, 1)],
                                          sem.at[slot]).start(priority=k % 2)
            return carry

        lax.fori_loop(0, tc // ROW_DMA_UNROLL, issue, 0)

    pl.when(i == 0)(lambda: gather(dest_now, 0))
    pl.when(i + 1 < pl.num_programs(0))(lambda: gather(dest_next, 1 - slot))
    for k in range(TOP_K):
        pltpu.make_async_copy(ys_ref.at[pl.ds(0, tc)], buf_ref.at[slot, k], sem.at[slot]).wait()
    gates = gate_ref[...]
    y = h_ref[...]
    for k in range(TOP_K):
        y = y + gates[:, k:k + 1] * buf_ref[slot, k]
    o_ref[...] = y


def _combine(h, gates, dest_flat, ys, tok_off, n_out):
    tc = TC_COMB
    off = tok_off // tc
    n_all = h.shape[0]
    n_steps = n_out // tc

    def dest_spec(k, ahead):
        return pl.BlockSpec((tc,), lambda i: (k * (n_all // tc) + jnp.minimum(i + ahead, n_steps - 1) + off,),
                            memory_space=pltpu.SMEM)

    return pl.pallas_call(
        _combine_kernel,
        grid=(n_steps,),
        in_specs=[dest_spec(k, 0) for k in range(TOP_K)] + [dest_spec(k, 1) for k in range(TOP_K)] + [
            pl.BlockSpec((tc, D_MODEL), lambda i: (i + off, 0)),
            pl.BlockSpec((tc, TOP_K), lambda i: (i + off, 0)),
            pl.BlockSpec(memory_space=pl.ANY),
        ],
        out_specs=pl.BlockSpec((tc, D_MODEL), lambda i: (i, 0)),
        out_shape=jax.ShapeDtypeStruct((n_out, D_MODEL), F32),
        scratch_shapes=[pltpu.VMEM((2, TOP_K, tc, D_MODEL), F32), pltpu.SemaphoreType.DMA((2,))],
        compiler_params=pltpu.CompilerParams(dimension_semantics=("arbitrary",), vmem_limit_bytes=VMEM_LIMIT),
        name="combine",
    )(*([dest_flat] * (2 * TOP_K)), h, gates, ys)


def kernel(x_prompt, x_sample, norm1_g, w_in, q_norm_g, k_norm_g, lambda_q1, lambda_k1, lambda_q2, lambda_k2,
           subln_g, rel_bias, w_pool_grp, pool_scale, w_att_out, w_pool_out, w_o, norm2_g, w_router, b_router,
           w_gate, b_gate, w_up, b_up, w_down, b_down):
    bp, sp, _ = x_prompt.shape
    bs, ss, _ = x_sample.shape
    n_p, n_s = bp * sp, bs * ss
    n_tok = n_p + n_s
    xp = x_prompt.reshape(n_p, D_MODEL)
    xs = x_sample.reshape(n_s, D_MODEL)

    proj = _in_projection(xp, xs, norm1_g[0], w_in[0].astype(BF16), q_norm_g[0], k_norm_g[0])
    band = _bias_band(rel_bias)
    lam_vecs = (lambda_q1[0], lambda_k1[0], lambda_q2[0], lambda_k2[0])
    att_p = _attention(proj, band, lam_vecs, subln_g[0], bp, sp, 0)
    att_s = _attention(proj, band, lam_vecs, subln_g[0], bs, ss, n_p)
    h, hn, logits = _merge_project(att_p, att_s, proj, xp, xs, sp, ss, w_pool_grp[0], pool_scale[0],
                                   w_att_out[0], w_pool_out[0], w_o[0], norm2_g[0], w_router[0], b_router[0])

    idx_rank, gates, counts = _route(logits)

    n_blocks = (n_tok * TOP_K + N_EXPERTS * (TM_MOE - 1)) // TM_MOE
    counts = counts[0].astype(I32)
    padded = (counts + TM_MOE - 1) // TM_MOE * TM_MOE
    pad_end = jnp.cumsum(padded)
    start_pad = pad_end - padded
    e_ids = jnp.arange(N_EXPERTS, dtype=I32)[:, None, None]
    dest = (jnp.sum(jnp.where(idx_rank[None, :TOP_K] == e_ids, start_pad[:, None, None], 0), axis=0)
            + idx_rank[TOP_K:]).reshape(-1)
    e_row = jnp.arange(N_EXPERTS, dtype=I32)[None, :]

    def owner(ends, pos):
        return jnp.minimum(jnp.sum(pos[:, None] >= ends[None, :], axis=1), N_EXPERTS - 1).astype(I32)

    def lookup(table, e):
        return jnp.sum(jnp.where(e[:, None] == e_row, table[None, :], 0), axis=1)

    blk_start = jnp.arange(n_blocks, dtype=I32) * TM_MOE
    blk_expert = owner(pad_end, blk_start)
    blk_valid = jnp.clip(lookup(counts, blk_expert) - (blk_start - lookup(start_pad, blk_expert)),
                         0, TM_MOE).astype(I32)
    e_blocks = padded // TM_MOE
    e_groups = (e_blocks + GT_MOE - 1) // GT_MOE
    grp_end = jnp.cumsum(e_groups)
    max_groups = n_blocks // GT_MOE + N_EXPERTS
    gidx = jnp.arange(max_groups, dtype=I32)
    grp_expert = owner(grp_end, gidx)
    in_expert = gidx - lookup(grp_end - e_groups, grp_expert)
    g_blocks = lookup(e_blocks, grp_expert)
    g_groups = lookup(e_groups, grp_expert)
    base = g_blocks // jnp.maximum(g_groups, 1)
    rem = g_blocks - base * g_groups
    grp_start = (lookup(start_pad // TM_MOE, grp_expert) + in_expert * base
                 + jnp.minimum(in_expert, rem)).astype(I32)
    grp_ntiles = (base + (in_expert < rem)).astype(I32)
    n_groups = grp_end[-1:].astype(I32)

    n_rows = (n_blocks + GT_MOE - 1) * TM_MOE
    xs_sorted = _dispatch(hn, dest, n_rows)
    ys = _expert_ffn(xs_sorted, grp_expert, grp_start, grp_ntiles, n_groups, blk_valid,
                     w_gate[0], b_gate[0], w_up[0], b_up[0], w_down[0], b_down[0])
    y_p = _combine(h, gates, dest, ys, 0, n_p)
    y_s = _combine(h, gates, dest, ys, n_p, n_s)
    return (y_p.reshape(bp, sp, D_MODEL), y_s.reshape(bs, ss, D_MODEL))
```

```python
import functools
import math

import jax
import jax.numpy as jnp
from jax import lax
from jax.experimental import pallas as pl
from jax.experimental.pallas import tpu as pltpu

F32 = jnp.float32
BF16 = jnp.bfloat16
I32 = jnp.int32

D_MODEL = 2048
N_HEADS = 8
HEAD_DIM = 64
V_DIM = 2 * HEAD_DIM
ATT_W = N_HEADS * V_DIM
POOL_WINDOWS = (2, 4, 8, 16)
POOL_W = D_MODEL // 2
POOL_GROUP_W = POOL_W // len(POOL_WINDOWS)
IN_COLS = 3 * ATT_W + POOL_W + 2 * D_MODEL
N_BUCKETS = 32
MAX_DISTANCE = 128
N_EXPERTS = 32
TOP_K = 4
D_FF = D_MODEL
SWIGLU_LIMIT = 7.0
SWIGLU_ALPHA = 1.702
NORM_EPS = 1e-6
LAM_INIT = 0.8 - 0.6 * math.exp(-0.3 * 0)
LOG2E = math.log2(math.e)

LANES = 128
VMEM_LIMIT = 56 * 1024 * 1024

TM_IN = 512
TN_IN = 2048
T_ATT = 256
G_ATT = 4
KV_PAIRS_PER_ITER = 4
V_AUG = V_DIM + 16
TM_OUT = 256
POOL_HALO = 16
TR_ROUTE = 512
TM_MOE = 528
TF_MOE = 256
GT_MOE = 4
TD_DISP = 512
TC_COMB = 256
ROW_DMA_UNROLL = 8


def _t5_thresholds():
    half = N_BUCKETS // 2
    max_exact = half // 2
    steps = half - max_exact
    ratio = MAX_DISTANCE // max_exact
    out = []
    for k in range(1, steps):
        n = max_exact
        while n ** steps < (max_exact ** steps) * (ratio ** k):
            n += 1
        out.append(n)
    return tuple(out)


T5_THRESHOLDS = _t5_thresholds()
T5_FAR = T5_THRESHOLDS[-1]


def _inproj_kernel(n_prompt_tiles, xp_ref, xs_ref, g1_ref, w_ref, gsum_ref, qg_ref, kg_ref, o_ref, xn_ref):
    i = pl.program_id(0)
    j = pl.program_id(1)

    @pl.when(j == 0)
    def _():
        x = jnp.where(i < n_prompt_tiles, xp_ref[...], xs_ref[...])
        ms = jnp.mean(x * x, axis=-1, keepdims=True)
        xn_ref[...] = (x * lax.rsqrt(ms + NORM_EPS) * g1_ref[...]).astype(BF16)

    half = TN_IN // 2

    def sections():
        xn = xn_ref[...]
        return [jnp.dot(xn, w_ref[:, c * half:(c + 1) * half], preferred_element_type=F32) for c in range(2)]

    def head_norm(acc, g_ref, scale):
        sq = (acc * acc).astype(BF16)
        gw = gsum_ref.shape[0]
        parts = [jnp.dot(sq[:, c * gw:(c + 1) * gw], gsum_ref[...], preferred_element_type=F32)
                 for c in range(half // gw)]
        ss = jnp.concatenate(parts, axis=1)
        return acc * lax.rsqrt(ss * (1.0 / HEAD_DIM) + NORM_EPS) * (g_ref[...] * scale)

    @pl.when(j == 0)
    def _():
        q, k = sections()
        o_ref[:, 0:half] = head_norm(q, qg_ref, HEAD_DIM ** -0.5 * LOG2E).astype(BF16)
        o_ref[:, half:TN_IN] = head_norm(k, kg_ref, 1.0).astype(BF16)

    @pl.when(j == 1)
    def _():
        for c, acc in enumerate(sections()):
            o_ref[:, c * half:(c + 1) * half] = acc.astype(BF16)

    @pl.when(j >= 2)
    def _():
        for c, acc in enumerate(sections()):
            o_ref[:, c * half:(c + 1) * half] = (0.5 * jnp.tanh(0.5 * acc) + 0.5).astype(BF16)


def _in_projection(xp, xs, norm1_g, w_in_bf, q_norm_g, k_norm_g):
    n_p, n_s = xp.shape[0], xs.shape[0]
    n_tok = n_p + n_s
    npt, nst = n_p // TM_IN, n_s // TM_IN
    gw = 2 * LANES
    gid = jnp.arange(gw) // HEAD_DIM
    gsum = (gid[:, None] == gid[None, :]).astype(BF16)
    assert TN_IN // 2 == ATT_W == POOL_W and D_MODEL % TN_IN == 0, "one q/k/v/u section per half column tile"
    qg = jnp.tile(q_norm_g.astype(F32), ATT_W // HEAD_DIM)[None, :]
    kg = jnp.tile(k_norm_g.astype(F32), ATT_W // HEAD_DIM)[None, :]
    return pl.pallas_call(
        functools.partial(_inproj_kernel, npt),
        grid=(n_tok // TM_IN, IN_COLS // TN_IN),
        in_specs=[
            pl.BlockSpec((TM_IN, D_MODEL), lambda i, j: (jnp.minimum(i, npt - 1), 0)),
            pl.BlockSpec((TM_IN, D_MODEL), lambda i, j: (jnp.maximum(i - npt, 0), 0)),
            pl.BlockSpec((1, D_MODEL), lambda i, j: (0, 0)),
            pl.BlockSpec((D_MODEL, TN_IN), lambda i, j: (0, j)),
            pl.BlockSpec((gw, gw), lambda i, j: (0, 0)),
            pl.BlockSpec((1, ATT_W), lambda i, j: (0, 0)),
            pl.BlockSpec((1, ATT_W), lambda i, j: (0, 0)),
        ],
        out_specs=pl.BlockSpec((TM_IN, TN_IN), lambda i, j: (i, j)),
        out_shape=jax.ShapeDtypeStruct((n_tok, IN_COLS), BF16),
        scratch_shapes=[pltpu.VMEM((TM_IN, D_MODEL), BF16)],
        compiler_params=pltpu.CompilerParams(
            dimension_semantics=("arbitrary", "arbitrary"), vmem_limit_bytes=VMEM_LIMIT),
        name="in_projection",
    )(xp, xs, norm1_g.astype(F32)[None, :], w_in_bf, gsum, qg, kg)


N_BAND = 5


def _band_kernel(rb_ref, o_ref):
    h = pl.program_id(0)
    t = T_ATT
    kj = lax.broadcasted_iota(I32, (t, t), 0)
    qi = lax.broadcasted_iota(I32, (t, t), 1)
    half = N_BUCKETS // 2
    max_exact = half // 2
    for d in range(N_BAND):
        rel = (d - N_BAND // 2) * t + kj - qi
        n = jnp.abs(rel)
        large = jnp.full((t, t), max_exact, I32)
        for th in T5_THRESHOLDS:
            large = large + jnp.where(n >= th, 1, 0)
        bucket = jnp.where(rel > 0, half, 0) + jnp.where(n < max_exact, n, large)
        val = jnp.zeros((t, t), F32)
        for b in range(N_BUCKETS):
            val = jnp.where(bucket == b, rb_ref[b, h], val)
        o_ref[0, d] = val * LOG2E


def _bias_band(rel_bias):
    assert T_ATT >= T5_FAR, "tiles two or more away from the diagonal must lie in the constant-bias region"
    return pl.pallas_call(
        _band_kernel,
        grid=(N_HEADS,),
        in_specs=[pl.BlockSpec(memory_space=pltpu.SMEM)],
        out_specs=pl.BlockSpec((1, N_BAND, T_ATT, T_ATT), lambda h: (h, 0, 0, 0)),
        out_shape=jax.ShapeDtypeStruct((N_HEADS, N_BAND, T_ATT, T_ATT), F32),
        compiler_params=pltpu.CompilerParams(dimension_semantics=("arbitrary",)),
        name="bias_band",
    )(rel_bias.astype(F32))


def _attn_kernel(seq, q_ref, k_ref, v_ref, band_ref, lq1_ref, lk1_ref, lq2_ref, lk2_ref, sg_ref,
                 o_ref, vt_ref, qz_ref, m_ref, acc_ref, s0_ref, s1_ref, c0_ref, c1_ref):
    qi = pl.program_id(2)
    t = T_ATT
    nk = seq // t
    heads = range(G_ATT)

    def cols(g):
        return slice(g * V_DIM, (g + 1) * V_DIM)

    @pl.when(qi == 0)
    def _():
        for g in heads:
            for c in range(nk):
                vt_ref[g, c, 0:V_DIM, :] = v_ref[c * t:(c + 1) * t, cols(g)].astype(F32).T.astype(BF16)
                vt_ref[g, c, V_DIM:V_AUG, :] = jnp.ones((V_AUG - V_DIM, t), BF16)

    for g in heads:
        q = q_ref[:, cols(g)]
        lane = lax.broadcasted_iota(I32, q.shape, 1)
        qz_ref[g, 0:t, :] = jnp.where(lane < HEAD_DIM, q, jnp.zeros_like(q))
        qz_ref[g, t:2 * t, :] = jnp.where(lane >= HEAD_DIM, q, jnp.zeros_like(q))
    m_ref[...] = jnp.full(m_ref.shape, -1e30, F32)
    acc_ref[...] = jnp.zeros(acc_ref.shape, F32)

    def score_tiles(ki, s_ref, c_ref):
        band = jnp.clip(ki - qi + N_BAND // 2, 0, N_BAND - 1)
        for g in heads:
            k_c = k_ref[pl.ds(pl.multiple_of(ki * t, t), t), cols(g)]
            bias = band_ref[g, band]
            s = lax.dot_general(k_c, qz_ref[g], (((1,), (1,)), ((), ())), preferred_element_type=F32)
            s = s + jnp.concatenate([bias, bias], axis=1)
            s_ref[g] = s
            c_ref[g] = jnp.max(s, axis=0, keepdims=True)

    def softmax_pv(ki, s_ref, c_ref):
        for g in heads:
            m_old = m_ref[g]
            m_new = jnp.maximum(m_old, c_ref[g])
            m_ref[g] = m_new
            alpha = jnp.exp2(m_old - m_new)
            p = jnp.exp2(s_ref[g] - m_new).astype(BF16)
            pv = jnp.dot(vt_ref[g, ki], p, preferred_element_type=F32)
            acc_ref[g] = acc_ref[g] * alpha + pv

    score_tiles(0, s0_ref, c0_ref)

    def pairs(j, carry):
        for u in range(KV_PAIRS_PER_ITER):
            k0 = 2 * (j * KV_PAIRS_PER_ITER + u)
            score_tiles(k0 + 1, s1_ref, c1_ref)
            softmax_pv(k0, s0_ref, c0_ref)
            score_tiles(jnp.minimum(k0 + 2, nk - 1), s0_ref, c0_ref)
            softmax_pv(k0 + 1, s1_ref, c1_ref)
        return carry

    lax.fori_loop(0, nk // (2 * KV_PAIRS_PER_ITER), pairs, 0)

    lam = (jnp.exp(jnp.sum(lq1_ref[...] * lk1_ref[...], axis=1, keepdims=True))
           - jnp.exp(jnp.sum(lq2_ref[...] * lk2_ref[...], axis=1, keepdims=True)) + LAM_INIT)
    for g in heads:
        acc = acc_ref[g, 0:V_DIM, :]
        l = acc_ref[g, V_DIM:V_DIM + 1, :]
        o = acc[:, 0:t] / l[:, 0:t] - lam * (acc[:, t:2 * t] / l[:, t:2 * t])
        ms = jnp.mean(o * o, axis=0, keepdims=True)
        on = o * lax.rsqrt(ms + NORM_EPS) * sg_ref[...] * (1.0 - LAM_INIT)
        o_ref[:, cols(g)] = on.T.astype(BF16)


def _attention(proj, band, lam_vecs, subln_g, batch, seq, row_off):
    t = T_ATT
    nq = seq // t
    assert nq % (2 * KV_PAIRS_PER_ITER) == 0, "key tiles are processed in pairs"
    gw = G_ATT * V_DIM
    sec = ATT_W // gw
    lq1, lk1, lq2, lk2 = [v.astype(F32)[None, :] for v in lam_vecs]
    vec_spec = pl.BlockSpec((1, HEAD_DIM), lambda b, h, qi: (0, 0))
    return pl.pallas_call(
        functools.partial(_attn_kernel, seq),
        grid=(batch, N_HEADS // G_ATT, nq),
        in_specs=[
            pl.BlockSpec((t, gw), lambda b, h, qi: (row_off // t + b * nq + qi, h)),
            pl.BlockSpec((seq, gw), lambda b, h, qi: (row_off // seq + b, sec + h)),
            pl.BlockSpec((seq, gw), lambda b, h, qi: (row_off // seq + b, 2 * sec + h)),
            pl.BlockSpec((G_ATT, N_BAND, t, t), lambda b, h, qi: (h, 0, 0, 0)),
            vec_spec, vec_spec, vec_spec, vec_spec,
            pl.BlockSpec((V_DIM, 1), lambda b, h, qi: (0, 0)),
        ],
        out_specs=pl.BlockSpec((t, gw), lambda b, h, qi: (b * nq + qi, h)),
        out_shape=jax.ShapeDtypeStruct((batch * seq, ATT_W), BF16),
        scratch_shapes=[
            pltpu.VMEM((G_ATT, seq // t, V_AUG, t), BF16),
            pltpu.VMEM((G_ATT, 2 * t, V_DIM), BF16),
            pltpu.VMEM((G_ATT, 1, 2 * t), F32),
            pltpu.VMEM((G_ATT, V_AUG, 2 * t), F32),
            pltpu.VMEM((G_ATT, t, 2 * t), F32),
            pltpu.VMEM((G_ATT, t, 2 * t), F32),
            pltpu.VMEM((G_ATT, 1, 2 * t), F32),
            pltpu.VMEM((G_ATT, 1, 2 * t), F32),
        ],
        compiler_params=pltpu.CompilerParams(
            dimension_semantics=("arbitrary", "arbitrary", "arbitrary"), vmem_limit_bytes=VMEM_LIMIT),
        name=f"diff_attention_s{seq}",
    )(proj, proj, proj, band, lq1, lk1, lq2, lk2, subln_g.astype(F32)[:, None])


def _pack_bf16_pairs(x):
    n = x.shape[1] // 2
    lo = lax.bitcast_convert_type(x[:, :n].astype(F32), jnp.uint32)
    hi = lax.bitcast_convert_type(x[:, n:].astype(F32), jnp.uint32)
    return lax.shift_right_logical(lo, jnp.uint32(16)) | (hi & jnp.uint32(0xFFFF0000))


def _unpack_bf16_pairs(w):
    lo = lax.bitcast_convert_type(lax.shift_left(w, jnp.uint32(16)), F32)
    hi = lax.bitcast_convert_type(w & jnp.uint32(0xFFFF0000), F32)
    return jnp.concatenate([lo.astype(BF16), hi.astype(BF16)], axis=1)


def _merge_kernel(n_prompt_tiles, seq_p, seq_s,
                  attp_ref, atts_ref, u_ref, ul_ref, ur_ref, ga_ref, gb_ref, xp_ref, xs_ref,
                  wpg_ref, psc_ref, wao_ref, wpo_ref, wo_ref, g2_ref, wr_ref, br_ref,
                  h_ref, hn_ref, lg_ref):
    i = pl.program_id(0)
    tm = TM_OUT
    is_p = i < n_prompt_tiles
    seq = jnp.where(is_p, seq_p, seq_s)
    t0 = jnp.where(is_p, i, i - n_prompt_tiles) * tm
    pos0 = t0 % seq
    first = pos0 == 0
    last = pos0 + tm == seq

    u = u_ref[...]
    zero_halo = jnp.zeros(ul_ref.shape, BF16)
    ul = jnp.where(first, zero_halo, ul_ref[...])
    ur = jnp.where(last, zero_halo, ur_ref[...])
    uext = jnp.concatenate([ul, u, ur], axis=0)
    r = lax.broadcasted_iota(I32, (tm, tm + 2 * POOL_HALO), 0)
    c = lax.broadcasted_iota(I32, (tm, tm + 2 * POOL_HALO), 1) - POOL_HALO
    pos = pos0 + lax.broadcasted_iota(I32, (tm, 1), 0)
    mixed = []
    for gi, w in enumerate(POOL_WINDOWS):
        sl = slice(gi * POOL_GROUP_W, (gi + 1) * POOL_GROUP_W)
        band = jnp.where(c >= r - w // 2, jnp.where(c < r + (w - w // 2), 1.0, 0.0), 0.0).astype(BF16)
        wsum = jnp.dot(band, uext[:, sl], preferred_element_type=F32)
        cnt = (jnp.minimum(pos + (w - w // 2), seq) - jnp.maximum(pos - w // 2, 0)).astype(F32)
        pooled = wsum / cnt - u[:, sl].astype(F32)
        mg = jnp.dot(pooled.astype(BF16), wpg_ref[gi], preferred_element_type=F32)
        mixed.append((mg * psc_ref[:, sl]).astype(BF16))
    mixed = jnp.concatenate(mixed, axis=1)

    att = jnp.where(is_p, attp_ref[...], atts_ref[...])
    y_a = jnp.dot(att, wao_ref[...], preferred_element_type=F32)
    y_b = jnp.dot(mixed, wpo_ref[...], preferred_element_type=F32)
    merged = ga_ref[...].astype(F32) * y_a + gb_ref[...].astype(F32) * y_b
    x = jnp.where(is_p, xp_ref[...], xs_ref[...])
    h = x + jnp.dot(merged.astype(BF16), wo_ref[...], preferred_element_type=F32)
    h_ref[...] = h
    ms = jnp.mean(h * h, axis=-1, keepdims=True)
    hn = (h * lax.rsqrt(ms + NORM_EPS) * g2_ref[...]).astype(BF16)
    hn_ref[...] = _pack_bf16_pairs(hn)
    lg_ref[...] = jnp.dot(hn, wr_ref[...], preferred_element_type=F32) + br_ref[...]


def _merge_project(att_p, att_s, proj, xp, xs, seq_p, seq_s, w_pool_grp, pool_scale, w_att_out, w_pool_out,
                   w_o, norm2_g, w_router, b_router):
    tm = TM_OUT
    n_p, n_s = xp.shape[0], xs.shape[0]
    n_tok = n_p + n_s
    npt = n_p // tm
    hb = tm // POOL_HALO
    n_hblk = n_tok // POOL_HALO
    u_col = 3 * ATT_W // POOL_W
    const = dict(pipeline_mode=pl.Buffered(1))
    return pl.pallas_call(
        functools.partial(_merge_kernel, npt, seq_p, seq_s),
        grid=(n_tok // tm,),
        in_specs=[
            pl.BlockSpec((tm, ATT_W), lambda i: (jnp.minimum(i, npt - 1), 0)),
            pl.BlockSpec((tm, ATT_W), lambda i: (jnp.maximum(i - npt, 0), 0)),
            pl.BlockSpec((tm, POOL_W), lambda i: (i, u_col)),
            pl.BlockSpec((POOL_HALO, POOL_W), lambda i: (jnp.maximum(i * hb - 1, 0), u_col)),
            pl.BlockSpec((POOL_HALO, POOL_W), lambda i: (jnp.minimum((i + 1) * hb, n_hblk - 1), u_col)),
            pl.BlockSpec((tm, D_MODEL), lambda i: (i, 2)),
            pl.BlockSpec((tm, D_MODEL), lambda i: (i, 3)),
            pl.BlockSpec((tm, D_MODEL), lambda i: (jnp.minimum(i, npt - 1), 0)),
            pl.BlockSpec((tm, D_MODEL), lambda i: (jnp.maximum(i - npt, 0), 0)),
            pl.BlockSpec((len(POOL_WINDOWS), POOL_GROUP_W, POOL_GROUP_W), lambda i: (0, 0, 0), **const),
            pl.BlockSpec((1, POOL_W), lambda i: (0, 0), **const),
            pl.BlockSpec((ATT_W, D_MODEL), lambda i: (0, 0), **const),
            pl.BlockSpec((POOL_W, D_MODEL), lambda i: (0, 0), **const),
            pl.BlockSpec((D_MODEL, D_MODEL), lambda i: (0, 0), **const),
            pl.BlockSpec((1, D_MODEL), lambda i: (0, 0), **const),
            pl.BlockSpec((D_MODEL, N_EXPERTS), lambda i: (0, 0), **const),
            pl.BlockSpec((1, N_EXPERTS), lambda i: (0, 0), **const),
        ],
        out_specs=[
            pl.BlockSpec((tm, D_MODEL), lambda i: (i, 0)),
            pl.BlockSpec((tm, D_MODEL // 2), lambda i: (i, 0)),
            pl.BlockSpec((tm, N_EXPERTS), lambda i: (i, 0)),
        ],
        out_shape=[
            jax.ShapeDtypeStruct((n_tok, D_MODEL), F32),
            jax.ShapeDtypeStruct((n_tok, D_MODEL // 2), jnp.uint32),
            jax.ShapeDtypeStruct((n_tok, N_EXPERTS), F32),
        ],
        compiler_params=pltpu.CompilerParams(dimension_semantics=("arbitrary",), vmem_limit_bytes=VMEM_LIMIT),
        name="merge_project",
    )(att_p, att_s, proj, proj, proj, proj, proj, xp, xs,
      w_pool_grp.astype(BF16), pool_scale.astype(F32)[None, :], w_att_out.astype(BF16),
      w_pool_out.astype(BF16), w_o.astype(BF16), norm2_g.astype(F32)[None, :],
      w_router.astype(BF16), b_router.astype(F32)[None, :])


def _route_kernel(lg_ref, ir_ref, gate_ref, cnt_ref, run_ref):
    i = pl.program_id(0)
    tr = TR_ROUTE

    @pl.when(i == 0)
    def _():
        run_ref[...] = jnp.zeros(run_ref.shape, F32)

    cur = lg_ref[...]
    e_iota = lax.broadcasted_iota(I32, cur.shape, 1).astype(F32)
    member = jnp.zeros(cur.shape, F32)
    vals, idxs = [], []
    for _ in range(TOP_K):
        mx = jnp.max(cur, axis=1, keepdims=True)
        am = jnp.min(jnp.where(cur == mx, e_iota, float(N_EXPERTS)), axis=1, keepdims=True)
        hit = e_iota == am
        vals.append(mx)
        idxs.append(am)
        member = member + jnp.where(hit, 1.0, 0.0)
        cur = jnp.where(hit, -jnp.inf, cur)
    exps = [jnp.exp(v - vals[0]) for v in vals]
    denom = exps[0]
    for e in exps[1:]:
        denom = denom + e

    rr = lax.broadcasted_iota(I32, (tr, tr), 0)
    cc = lax.broadcasted_iota(I32, (tr, tr), 1)
    tri = jnp.where(cc < rr, 1.0, 0.0).astype(BF16)
    before = jnp.dot(tri, member.astype(BF16), preferred_element_type=F32) + run_ref[...]
    run_ref[...] = run_ref[...] + jnp.sum(member, axis=0, keepdims=True)
    cnt_ref[...] = run_ref[...]

    lane = lax.broadcasted_iota(I32, (tr, TOP_K), 1)
    wide = lax.broadcasted_iota(I32, (tr, LANES), 1)
    gate_out = jnp.zeros((tr, TOP_K), F32)
    ir = jnp.zeros((tr, LANES), F32)
    for k in range(TOP_K):
        rk = jnp.sum(jnp.where(e_iota == idxs[k], before, 0.0), axis=1, keepdims=True)
        gate_out = jnp.where(lane == k, exps[k] / denom, gate_out)
        ir = jnp.where(wide == k, idxs[k], jnp.where(wide == TOP_K + k, rk, ir))
    gate_ref[...] = gate_out
    ir_ref[...] = ir.T[0:2 * TOP_K, :].astype(I32)


def _route(logits):
    n_tok = logits.shape[0]
    tr = TR_ROUTE
    return pl.pallas_call(
        _route_kernel,
        grid=(n_tok // tr,),
        in_specs=[pl.BlockSpec((tr, N_EXPERTS), lambda i: (i, 0))],
        out_specs=[
            pl.BlockSpec((2 * TOP_K, tr), lambda i: (0, i)),
            pl.BlockSpec((tr, TOP_K), lambda i: (i, 0)),
            pl.BlockSpec((1, N_EXPERTS), lambda i: (0, 0)),
        ],
        out_shape=[
            jax.ShapeDtypeStruct((2 * TOP_K, n_tok), I32),
            jax.ShapeDtypeStruct((n_tok, TOP_K), F32),
            jax.ShapeDtypeStruct((1, N_EXPERTS), F32),
        ],
        scratch_shapes=[pltpu.VMEM((1, N_EXPERTS), F32)],
        compiler_params=pltpu.CompilerParams(dimension_semantics=("arbitrary",)),
        name="route",
    )(logits)


def _dispatch_kernel(*refs):
    dest_refs, (hn_ref, xs_ref, sem) = refs[:TOP_K], refs[TOP_K:]
    td = TD_DISP

    def issue(i, carry):
        for u in range(ROW_DMA_UNROLL):
            r = i * ROW_DMA_UNROLL + u
            for k in range(TOP_K):
                d = dest_refs[k][r]
                pltpu.make_async_copy(hn_ref.at[pl.ds(r, 1)], xs_ref.at[pl.ds(d, 1)], sem).start(priority=k % 2)
        return carry

    lax.fori_loop(0, td // ROW_DMA_UNROLL, issue, 0)
    for _ in range(TOP_K):
        pltpu.make_async_copy(hn_ref, xs_ref.at[pl.ds(0, td)], sem).wait()


def _dispatch(hn, dest_flat, n_rows):
    n_tok = hn.shape[0]
    td = TD_DISP
    return pl.pallas_call(
        _dispatch_kernel,
        grid=(n_tok // td,),
        in_specs=[pl.BlockSpec((td,), lambda i, k=k: (k * (n_tok // td) + i,), memory_space=pltpu.SMEM)
                  for k in range(TOP_K)] + [
            pl.BlockSpec((td, hn.shape[1]), lambda i: (i, 0)),
        ],
        out_specs=pl.BlockSpec(memory_space=pl.ANY),
        out_shape=jax.ShapeDtypeStruct((n_rows, hn.shape[1]), hn.dtype),
        scratch_shapes=[pltpu.SemaphoreType.DMA(())],
        compiler_params=pltpu.CompilerParams(dimension_semantics=("arbitrary",)),
        name="dispatch",
    )(*([dest_flat] * TOP_K), hn)


def _ffn_kernel(ge_ref, gs_ref, gn_ref, ng_ref, bv_ref,
                xs_hbm, wg_ref, bg_ref, wu_ref, bu_ref, wd_ref, bd_ref,
                ys_hbm, slab_ref, acc_ref, sem_x, sem_o):
    grp = pl.program_id(0)
    f = pl.program_id(1)
    nf = pl.num_programs(1)
    tm = TM_MOE
    n_grp = ng_ref[0]

    def load(g, slot):
        row0 = pl.multiple_of(gs_ref[g] * tm, tm)
        return pltpu.make_async_copy(xs_hbm.at[pl.ds(row0, GT_MOE * tm)], slab_ref.at[slot], sem_x.at[slot])

    def store(g, r):
        row0 = pl.multiple_of((gs_ref[g] + r) * tm, tm)
        return pltpu.make_async_copy(acc_ref.at[pl.ds(r * tm, tm)], ys_hbm.at[pl.ds(row0, tm)], sem_o.at[r])

    @pl.when(grp < n_grp)
    def _():
        start = gs_ref[grp]
        nt = gn_ref[grp]
        slot = grp % 2

        @pl.when(f == 0)
        def _():
            pl.when(grp == 0)(lambda: load(0, 0).start())

            @pl.when(grp > 0)
            def _():
                for r in range(GT_MOE):
                    pl.when((r < gn_ref[grp - 1]) & (r >= nt))(lambda r=r: store(grp - 1, r).wait())

            load(grp, slot).wait()
            pl.when(grp + 1 < n_grp)(lambda: load(grp + 1, 1 - slot).start())

        def row_tiles(tiles):
            @pl.when((f == 0) & (grp > 0))
            def _():
                for r in tiles:
                    pl.when(r < gn_ref[grp - 1])(lambda r=r: store(grp - 1, r).wait())

            wg = wg_ref[0].astype(BF16)
            wu = wu_ref[0].astype(BF16)
            wd = wd_ref[0].astype(BF16)
            row = lax.broadcasted_iota(I32, (tm, 1), 0)
            gates, ups = [], []
            for r in tiles:
                words = jnp.where(row < bv_ref[start + r], slab_ref[slot, r * tm:(r + 1) * tm, :], jnp.uint32(0))
                xb = _unpack_bf16_pairs(words)
                gates.append(jnp.dot(xb, wg, preferred_element_type=F32) + bg_ref[0])
                ups.append(jnp.dot(xb, wu, preferred_element_type=F32) + bu_ref[0])
            hidden = []
            for g, up in zip(gates, ups):
                g = jnp.minimum(g, SWIGLU_LIMIT)
                up = jnp.clip(up, -SWIGLU_LIMIT, SWIGLU_LIMIT)
                hidden.append(((up + 1.0) * (g * jax.nn.sigmoid(SWIGLU_ALPHA * g))).astype(BF16))
            for r, hmid in zip(tiles, hidden):
                rows = slice(r * tm, (r + 1) * tm)
                base = jnp.where(f == 0, jnp.broadcast_to(bd_ref[0], (tm, D_MODEL)), acc_ref[rows, :])
                acc_ref[rows, :] = base + jnp.dot(hmid, wd, preferred_element_type=F32)

            @pl.when(f == nf - 1)
            def _():
                for r in tiles:
                    store(grp, r).start()

        assert GT_MOE == 4
        pl.when(nt >= 2)(lambda: row_tiles((0, 1)))
        pl.when(nt == 1)(lambda: row_tiles((0,)))
        pl.when(nt == 4)(lambda: row_tiles((2, 3)))
        pl.when(nt == 3)(lambda: row_tiles((2,)))

        @pl.when((f == nf - 1) & (grp == n_grp - 1))
        def _():
            for r in range(GT_MOE):
                pl.when(r < nt)(lambda r=r: store(grp, r).wait())


def _expert_ffn(xs, grp_expert, grp_start, grp_ntiles, n_groups, blk_valid,
                w_gate, b_gate, w_up, b_up, w_down, b_down):
    n_rows = xs.shape[0]
    max_groups = grp_expert.shape[0]
    nf = D_FF // TF_MOE

    def expert(g, ge, ng):
        return ge[jnp.minimum(g, ng[0] - 1)]

    def ftile(g, f, ng):
        return jnp.where(g < ng[0], f, nf - 1)

    grid_spec = pltpu.PrefetchScalarGridSpec(
        num_scalar_prefetch=5,
        grid=(max_groups, nf),
        in_specs=[
            pl.BlockSpec(memory_space=pl.ANY),
            pl.BlockSpec((1, D_MODEL, TF_MOE), lambda g, f, ge, gs, gn, ng, bv: (expert(g, ge, ng), 0, ftile(g, f, ng))),
            pl.BlockSpec((1, 1, TF_MOE), lambda g, f, ge, gs, gn, ng, bv: (expert(g, ge, ng), 0, ftile(g, f, ng))),
            pl.BlockSpec((1, D_MODEL, TF_MOE), lambda g, f, ge, gs, gn, ng, bv: (expert(g, ge, ng), 0, ftile(g, f, ng))),
            pl.BlockSpec((1, 1, TF_MOE), lambda g, f, ge, gs, gn, ng, bv: (expert(g, ge, ng), 0, ftile(g, f, ng))),
            pl.BlockSpec((1, TF_MOE, D_MODEL), lambda g, f, ge, gs, gn, ng, bv: (expert(g, ge, ng), ftile(g, f, ng), 0)),
            pl.BlockSpec((1, 1, D_MODEL), lambda g, f, ge, gs, gn, ng, bv: (expert(g, ge, ng), 0, 0)),
        ],
        out_specs=pl.BlockSpec(memory_space=pl.ANY),
        scratch_shapes=[
            pltpu.VMEM((2, GT_MOE * TM_MOE, D_MODEL // 2), jnp.uint32),
            pltpu.VMEM((GT_MOE * TM_MOE, D_MODEL), F32),
            pltpu.SemaphoreType.DMA((2,)),
            pltpu.SemaphoreType.DMA((GT_MOE,)),
        ],
    )
    return pl.pallas_call(
        _ffn_kernel,
        grid_spec=grid_spec,
        out_shape=jax.ShapeDtypeStruct((n_rows, D_MODEL), F32),
        compiler_params=pltpu.CompilerParams(
            dimension_semantics=("arbitrary", "arbitrary"), vmem_limit_bytes=VMEM_LIMIT),
        name="expert_ffn",
    )(grp_expert, grp_start, grp_ntiles, n_groups, blk_valid, xs,
      w_gate, b_gate[:, None, :], w_up, b_up[:, None, :], w_down, b_down[:, None, :])


def _combine_kernel(*refs):
    dest_now, dest_next = refs[:TOP_K], refs[TOP_K:2 * TOP_K]
    h_ref, gate_ref, ys_ref, o_ref, buf_ref, sem = refs[2 * TOP_K:]
    tc = TC_COMB
    i = pl.program_id(0)
    slot = i % 2

    def gather(dest_refs, slot):
        def issue(j, carry):
            for u in range(ROW_DMA_UNROLL):
                r = j * ROW_DMA_UNROLL + u
                for k in range(TOP_K):
                    d = dest_refs[k][r]
                    pltpu.make_async_copy(ys_ref.at[pl.ds(d, 1)], buf_ref.at[slot, k, pl.ds(r, 1)],
                                          sem.at[slot]).start(priority=k % 2)
            return carry

        lax.fori_loop(0, tc // ROW_DMA_UNROLL, issue, 0)

    pl.when(i == 0)(lambda: gather(dest_now, 0))
    pl.when(i + 1 < pl.num_programs(0))(lambda: gather(dest_next, 1 - slot))
    for k in range(TOP_K):
        pltpu.make_async_copy(ys_ref.at[pl.ds(0, tc)], buf_ref.at[slot, k], sem.at[slot]).wait()
    gates = gate_ref[...]
    y = h_ref[...]
    for k in range(TOP_K):
        y = y + gates[:, k:k + 1] * buf_ref[slot, k]
    o_ref[...] = y


def _combine(h, gates, dest_flat, ys, tok_off, n_out):
    tc = TC_COMB
    off = tok_off // tc
    n_all = h.shape[0]
    n_steps = n_out // tc

    def dest_spec(k, ahead):
        return pl.BlockSpec((tc,), lambda i: (k * (n_all // tc) + jnp.minimum(i + ahead, n_steps - 1) + off,),
                            memory_space=pltpu.SMEM)

    return pl.pallas_call(
        _combine_kernel,
        grid=(n_steps,),
        in_specs=[dest_spec(k, 0) for k in range(TOP_K)] + [dest_spec(k, 1) for k in range(TOP_K)] + [
            pl.BlockSpec((tc, D_MODEL), lambda i: (i + off, 0)),
            pl.BlockSpec((tc, TOP_K), lambda i: (i + off, 0)),
            pl.BlockSpec(memory_space=pl.ANY),
        ],
        out_specs=pl.BlockSpec((tc, D_MODEL), lambda i: (i, 0)),
        out_shape=jax.ShapeDtypeStruct((n_out, D_MODEL), F32),
        scratch_shapes=[pltpu.VMEM((2, TOP_K, tc, D_MODEL), F32), pltpu.SemaphoreType.DMA((2,))],
        compiler_params=pltpu.CompilerParams(dimension_semantics=("arbitrary",), vmem_limit_bytes=VMEM_LIMIT),
        name="combine",
    )(*([dest_flat] * (2 * TOP_K)), h, gates, ys)


def kernel(x_prompt, x_sample, norm1_g, w_in, q_norm_g, k_norm_g, lambda_q1, lambda_k1, lambda_q2, lambda_k2,
           subln_g, rel_bias, w_pool_grp, pool_scale, w_att_out, w_pool_out, w_o, norm2_g, w_router, b_router,
           w_gate, b_gate, w_up, b_up, w_down, b_down):
    bp, sp, _ = x_prompt.shape
    bs, ss, _ = x_sample.shape
    n_p, n_s = bp * sp, bs * ss
    n_tok = n_p + n_s
    xp = x_prompt.reshape(n_p, D_MODEL)
    xs = x_sample.reshape(n_s, D_MODEL)

    proj = _in_projection(xp, xs, norm1_g[0], w_in[0].astype(BF16), q_norm_g[0], k_norm_g[0])
    band = _bias_band(rel_bias)
    lam_vecs = (lambda_q1[0], lambda_k1[0], lambda_q2[0], lambda_k2[0])
    att_p = _attention(proj, band, lam_vecs, subln_g[0], bp, sp, 0)
    att_s = _attention(proj, band, lam_vecs, subln_g[0], bs, ss, n_p)
    h, hn, logits = _merge_project(att_p, att_s, proj, xp, xs, sp, ss, w_pool_grp[0], pool_scale[0],
                                   w_att_out[0], w_pool_out[0], w_o[0], norm2_g[0], w_router[0], b_router[0])

    idx_rank, gates, counts = _route(logits)

    n_blocks = (n_tok * TOP_K + N_EXPERTS * (TM_MOE - 1)) // TM_MOE
    counts = counts[0].astype(I32)
    padded = (counts + TM_MOE - 1) // TM_MOE * TM_MOE
    pad_end = jnp.cumsum(padded)
    start_pad = pad_end - padded
    e_ids = jnp.arange(N_EXPERTS, dtype=I32)[:, None, None]
    dest = (jnp.sum(jnp.where(idx_rank[None, :TOP_K] == e_ids, start_pad[:, None, None], 0), axis=0)
            + idx_rank[TOP_K:]).reshape(-1)
    e_row = jnp.arange(N_EXPERTS, dtype=I32)[None, :]

    def owner(ends, pos):
        return jnp.minimum(jnp.sum(pos[:, None] >= ends[None, :], axis=1), N_EXPERTS - 1).astype(I32)

    def lookup(table, e):
        return jnp.sum(jnp.where(e[:, None] == e_row, table[None, :], 0), axis=1)

    blk_start = jnp.arange(n_blocks, dtype=I32) * TM_MOE
    blk_expert = owner(pad_end, blk_start)
    blk_valid = jnp.clip(lookup(counts, blk_expert) - (blk_start - lookup(start_pad, blk_expert)),
                         0, TM_MOE).astype(I32)
    e_blocks = padded // TM_MOE
    e_groups = (e_blocks + GT_MOE - 1) // GT_MOE
    grp_end = jnp.cumsum(e_groups)
    max_groups = n_blocks // GT_MOE + N_EXPERTS
    gidx = jnp.arange(max_groups, dtype=I32)
    grp_expert = owner(grp_end, gidx)
    in_expert = gidx - lookup(grp_end - e_groups, grp_expert)
    g_blocks = lookup(e_blocks, grp_expert)
    g_groups = lookup(e_groups, grp_expert)
    base = g_blocks // jnp.maximum(g_groups, 1)
    rem = g_blocks - base * g_groups
    grp_start = (lookup(start_pad // TM_MOE, grp_expert) + in_expert * base
                 + jnp.minimum(in_expert, rem)).astype(I32)
    grp_ntiles = (base + (in_expert < rem)).astype(I32)
    n_groups = grp_end[-1:].astype(I32)

    n_rows = (n_blocks + GT_MOE - 1) * TM_MOE
    xs_sorted = _dispatch(hn, dest, n_rows)
    ys = _expert_ffn(xs_sorted, grp_expert, grp_start, grp_ntiles, n_groups, blk_valid,
                     w_gate[0], b_gate[0], w_up[0], b_up[0], w_down[0], b_down[0])
    y_p = _combine(h, gates, dest, ys, 0, n_p)
    y_s = _combine(h, gates, dest, ys, n_p, n_s)
    return (y_p.reshape(bp, sp, D_MODEL), y_s.reshape(bs, ss, D_MODEL))
```
